```python
import jax, jax.numpy as jnp
from jax import lax
import numpy as np

D_MODEL = 1024
BATCH = 8
SEQ = 2048
DEPTH = 2
DEC_BATCH = 128
DEC_SEQ = 1
PAST_LEN = 16384
PAGE_SIZE = 128

N_META = 16
D_MIX = D_MODEL
DN_HEADS = 4
DN_DK = D_MIX // 8
DN_DV = D_MIX // 8
DN_W = DN_HEADS * DN_DV
LRU_W = D_MIX // 4
LRU_BLOCKS = 4
LRU_BD = LRU_W // LRU_BLOCKS
LRU_C = 8.0
RET_HEADS = 4
RET_DK = D_MIX // 16
RET_DV = D_MIX // 16
RET_W = RET_HEADS * RET_DV
CONV_W = 4
CHUNK = 64
D_FF = 128 * ((8 * D_MODEL // 3 + 127) // 128)
ROPE_BASE = 10000.0
EPS = 1e-6
IN_SIZES = (3 * DN_W, DN_HEADS, DN_HEADS, DN_W,
            LRU_W, LRU_W,
            RET_HEADS * RET_DK, RET_HEADS * RET_DK, RET_W, RET_W)
D_IN = sum(IN_SIZES)

kernel_name = 'hybrid_deltanet_rglru_retention_step'


def _split_offsets():
    out, s = [], 0
    for n in IN_SIZES[:-1]:
        s += n
        out.append(s)
    return out


def _rmsnorm(x, w):
    xf = x.astype(jnp.float32)
    y = xf * lax.rsqrt(jnp.mean(xf * xf, axis=-1, keepdims=True) + EPS)
    return (y * w.astype(jnp.float32)).astype(x.dtype)


def _swiglu(x, w_in, w_out):
    gate, up = jnp.split(x @ w_in, 2, axis=-1)
    return (jax.nn.silu(gate) * up) @ w_out


def _causal_conv(x, buf, w):
    T = x.shape[1]
    xp = jnp.concatenate([buf.astype(x.dtype), x], axis=1)
    y = xp[:, 0:T] * w[0]
    for j in range(1, CONV_W):
        y = y + xp[:, j:j + T] * w[j]
    return y, xp[:, T:]


def _l2norm(x):
    return x * lax.rsqrt(jnp.sum(x * x, axis=-1, keepdims=True) + EPS)


def _rope(x, pos):
    half = x.shape[-1] // 2
    inv = ROPE_BASE ** (-jnp.arange(half, dtype=jnp.float32) / half)
    ang = pos.astype(jnp.float32)[:, None] * inv[None, :]
    cos = jnp.cos(ang)[None, :, None, :]
    sin = jnp.sin(ang)[None, :, None, :]
    x1, x2 = x[..., :half], x[..., half:]
    return jnp.concatenate([x1 * cos - x2 * sin, x1 * sin + x2 * cos], axis=-1)


def _decay_masks(G):
    C = G.shape[-1]
    t = jnp.arange(C)
    diff = G[..., :, None] - G[..., None, :]
    strict = jnp.exp(jnp.where(t[:, None] > t[None, :], diff, -jnp.inf))
    incl = jnp.exp(jnp.where(t[:, None] >= t[None, :], diff, -jnp.inf))
    return strict, incl


def _delta_step(S0, q, k, v, g, beta):
    dv = v.shape[-1]
    G = jnp.cumsum(g, axis=-1)
    eG = jnp.exp(G)
    dec_strict, dec_incl = _decay_masks(G)
    A = beta[..., :, None] * jnp.einsum('bhtd,bhid->bhti', k, k) * dec_strict
    rhs = jnp.concatenate([beta[..., None] * v, (beta * eG)[..., None] * k], axis=-1)
    sol = lax.linalg.triangular_solve(A, rhs, left_side=True, lower=True, unit_diagonal=True)
    U = sol[..., :dv] - jnp.einsum('bhtd,bhde->bhte', sol[..., dv:], S0)
    o = (eG[..., None] * jnp.einsum('bhtd,bhde->bhte', q, S0)
         + jnp.einsum('bhti,bhie->bhte', jnp.einsum('bhtd,bhid->bhti', q, k) * dec_incl, U))
    GC = G[..., -1]
    S = (jnp.exp(GC)[..., None, None] * S0
         + jnp.einsum('bhid,bhie->bhde', k * jnp.exp(GC[..., None] - G)[..., None], U))
    return o, S


def _ret_step(S0, q, k, v, g):
    G = jnp.cumsum(g, axis=-1)
    _, dec_incl = _decay_masks(G)
    o = (jnp.exp(G)[..., None] * jnp.einsum('bhtd,bhde->bhte', q, S0)
         + jnp.einsum('bhti,bhie->bhte', jnp.einsum('bhtd,bhid->bhti', q, k) * dec_incl, v))
    GC = G[..., -1]
    S = (jnp.exp(GC)[..., None, None] * S0
         + jnp.einsum('bhid,bhie->bhde', k * jnp.exp(GC[..., None] - G)[..., None], v))
    return o, S


def _run_chunks(step, S0, xs):
    o_lead, S = step(S0, *[a[:, :, :N_META] for a in xs])
    rest = [a[:, :, N_META:] for a in xs]
    T = rest[0].shape[2]
    n = T // CHUNK

    def to_chunks(a):
        return jnp.moveaxis(a.reshape(a.shape[:2] + (n, CHUNK) + a.shape[3:]), 2, 0)

    def body(S_c, c):
        o_c, S_n = step(S_c, *c)
        return S_n, o_c

    S, o_rest = lax.scan(body, S, tuple(to_chunks(a) for a in rest))
    o_rest = jnp.moveaxis(o_rest, 0, 2)
    o_rest = o_rest.reshape(o_rest.shape[:2] + (T,) + o_rest.shape[4:])
    return jnp.concatenate([o_lead, o_rest], axis=2), S


def _lin_op(c1, c2):
    a1, b1 = c1
    a2, b2 = c2
    return a1 * a2, a2 * b1 + b2


def _token_mix(u, pos, st, w_in, dn_conv_w, dn_a_log, dn_dt_bias, dn_norm_w,
               lru_conv_w, lru_conv_b, lru_wa, lru_ba, lru_wx, lru_bx, lru_lambda, w_out):
    f32 = jnp.float32
    B, T, _ = u.shape
    if st is None:
        S_dn0 = jnp.zeros((B, DN_HEADS, DN_DK, DN_DV), f32)
        buf_dn = jnp.zeros((B, CONV_W - 1, 3 * DN_W), u.dtype)
        h0 = jnp.zeros((B, LRU_W), f32)
        buf_lru = jnp.zeros((B, CONV_W - 1, LRU_W), u.dtype)
        S_ret0 = jnp.zeros((B, RET_HEADS, RET_DK, RET_DV), f32)
    else:
        S_dn0, buf_dn, h0, buf_lru, S_ret0 = st
        S_dn0, h0, S_ret0 = S_dn0.astype(f32), h0.astype(f32), S_ret0.astype(f32)
    proj = u @ w_in
    (dn_qkv, dn_a, dn_b, dn_z, lru_x, lru_y,
     ret_q, ret_k, ret_v, ret_g) = jnp.split(proj, _split_offsets(), axis=-1)

    qkv, buf_dn_new = _causal_conv(dn_qkv, buf_dn, dn_conv_w)
    qkv = jax.nn.silu(qkv).astype(f32)
    q, k, v = jnp.split(qkv, 3, axis=-1)
    q = jnp.swapaxes(_l2norm(q.reshape(B, T, DN_HEADS, DN_DK)), 1, 2) * (DN_DK ** -0.5)
    k = jnp.swapaxes(_l2norm(k.reshape(B, T, DN_HEADS, DN_DK)), 1, 2)
    v = jnp.swapaxes(v.reshape(B, T, DN_HEADS, DN_DV), 1, 2)
    g = -jnp.exp(dn_a_log.astype(f32)) * jax.nn.softplus(dn_a.astype(f32) + dn_dt_bias.astype(f32))
    g = jnp.swapaxes(g, 1, 2)
    beta = jnp.swapaxes(jax.nn.sigmoid(dn_b.astype(f32)), 1, 2)
    if st is None:
        o_dn, S_dn = _run_chunks(_delta_step, S_dn0, (q, k, v, g, beta))
    else:
        o_dn, S_dn = _delta_step(S_dn0, q, k, v, g, beta)
    o_dn = jnp.swapaxes(o_dn, 1, 2)
    o_dn = o_dn * lax.rsqrt(jnp.mean(o_dn * o_dn, axis=-1, keepdims=True) + EPS) * dn_norm_w.astype(f32)
    o_dn = (o_dn * jax.nn.silu(dn_z.astype(f32).reshape(B, T, DN_HEADS, DN_DV))).reshape(B, T, DN_W)

    xl, buf_lru_new = _causal_conv(lru_x, buf_lru, lru_conv_w)
    xl = (xl + lru_conv_b).astype(f32)
    xb = xl.reshape(B, T, LRU_BLOCKS, LRU_BD)
    r = jax.nn.sigmoid(jnp.einsum('btnd,nde->btne', xb, lru_wa.astype(f32)).reshape(B, T, LRU_W) + lru_ba.astype(f32))
    i = jax.nn.sigmoid(jnp.einsum('btnd,nde->btne', xb, lru_wx.astype(f32)).reshape(B, T, LRU_W) + lru_bx.astype(f32))
    log_a = -LRU_C * r * jax.nn.softplus(-lru_lambda.astype(f32))
    a = jnp.exp(log_a)
    b = jnp.sqrt(jnp.maximum(-jnp.expm1(2.0 * log_a), 0.0)) * (i * xl)
    b = b.at[:, 0].add(a[:, 0] * h0)
    _, h = lax.associative_scan(_lin_op, (a, b), axis=1)
    h_new = h[:, -1]
    o_lru = h * jax.nn.gelu(lru_y.astype(f32))

    rq = jnp.swapaxes(_rope(ret_q.astype(f32).reshape(B, T, RET_HEADS, RET_DK), pos), 1, 2)
    rk = jnp.swapaxes(_rope(ret_k.astype(f32).reshape(B, T, RET_HEADS, RET_DK), pos), 1, 2) * (RET_DK ** -0.5)
    rv = jnp.swapaxes(ret_v.astype(f32).reshape(B, T, RET_HEADS, RET_DV), 1, 2)
    log_gamma = jnp.log1p(-jnp.exp2(-5.0 - jnp.arange(RET_HEADS, dtype=f32)))
    g_ret = jnp.broadcast_to(log_gamma[None, :, None], (B, RET_HEADS, T))
    if st is None:
        o_ret, S_ret = _run_chunks(_ret_step, S_ret0, (rq, rk, rv, g_ret))
    else:
        o_ret, S_ret = _ret_step(S_ret0, rq, rk, rv, g_ret)
    o_ret = jnp.swapaxes(o_ret, 1, 2)
    mu = jnp.mean(o_ret, axis=-1, keepdims=True)
    var = jnp.mean(jnp.square(o_ret - mu), axis=-1, keepdims=True)
    o_ret = ((o_ret - mu) * lax.rsqrt(var + EPS)).reshape(B, T, RET_W) * jax.nn.silu(ret_g.astype(f32))

    o = jnp.concatenate([o_dn, o_lru, o_ret], axis=-1).astype(u.dtype) @ w_out
    return o, (S_dn, buf_dn_new, h_new, buf_lru_new, S_ret)


def _layer(x, pos, st, n1, f1_in, f1_out, n2, w_in, dn_conv_w, dn_a_log, dn_dt_bias, dn_norm_w,
           lru_conv_w, lru_conv_b, lru_wa, lru_ba, lru_wx, lru_bx, lru_lambda, w_out, n3, f2_in, f2_out):
    x = x + 0.5 * _swiglu(_rmsnorm(x, n1), f1_in, f1_out)
    m, new_st = _token_mix(_rmsnorm(x, n2), pos, st, w_in, dn_conv_w, dn_a_log, dn_dt_bias, dn_norm_w,
                           lru_conv_w, lru_conv_b, lru_wa, lru_ba, lru_wx, lru_bx, lru_lambda, w_out)
    x = x + m
    x = x + 0.5 * _swiglu(_rmsnorm(x, n3), f2_in, f2_out)
    return x, new_st


def setup_inputs(seed: int = 0) -> dict:
    key = jax.random.key(seed)
    ks = jax.random.split(key, 32)
    f32 = jnp.float32
    nrm = lambda k, s, sc: jax.random.normal(k, s, f32) * sc
    gain = lambda k, s: 1.0 + 0.01 * jax.random.normal(k, s, f32)
    dt = jnp.exp(jax.random.uniform(ks[20], (DEPTH, DN_HEADS), f32, np.log(1e-3), np.log(1e-1)))
    a_pow = jax.random.uniform(ks[21], (DEPTH, LRU_W), f32, 0.9, 0.999)
    sig_l = a_pow ** (1.0 / LRU_C)
    return {
        'x_prompt': nrm(ks[0], (BATCH, SEQ, D_MODEL), 1.0),
        'x_sample': nrm(ks[1], (DEC_BATCH, DEC_SEQ, D_MODEL), 1.0),
        'state_dn': nrm(ks[2], (DEPTH, DEC_BATCH, DN_HEADS, DN_DK, DN_DV), DN_DK ** -0.5),
        'state_dn_conv': nrm(ks[3], (DEPTH, DEC_BATCH, CONV_W - 1, 3 * DN_W), 1.0),
        'state_lru': nrm(ks[4], (DEPTH, DEC_BATCH, LRU_W), 1.0),
        'state_lru_conv': nrm(ks[5], (DEPTH, DEC_BATCH, CONV_W - 1, LRU_W), 1.0),
        'state_ret': nrm(ks[6], (DEPTH, DEC_BATCH, RET_HEADS, RET_DK, RET_DV), 1.0),
        'meta_tokens': nrm(ks[7], (N_META, D_MODEL), 1.0),
        'norm_ffn1': gain(ks[8], (DEPTH, D_MODEL)),
        'w_ffn1_in': nrm(ks[9], (DEPTH, D_MODEL, 2 * D_FF), D_MODEL ** -0.5),
        'w_ffn1_out': nrm(ks[10], (DEPTH, D_FF, D_MODEL), D_FF ** -0.5),
        'norm_mix': gain(ks[11], (DEPTH, D_MODEL)),
        'w_in': nrm(ks[12], (DEPTH, D_MODEL, D_IN), D_MODEL ** -0.5),
        'dn_conv_w': nrm(ks[13], (DEPTH, CONV_W, 3 * DN_W), CONV_W ** -0.5),
        'dn_a_log': jnp.log(jax.random.uniform(ks[14], (DEPTH, DN_HEADS), f32, 1.0, 16.0)),
        'dn_dt_bias': dt + jnp.log(-jnp.expm1(-dt)),
        'dn_norm_w': gain(ks[15], (DEPTH, DN_DV)),
        'lru_conv_w': nrm(ks[16], (DEPTH, CONV_W, LRU_W), CONV_W ** -0.5),
        'lru_conv_b': nrm(ks[17], (DEPTH, LRU_W), 0.01),
        'lru_wa': nrm(ks[18], (DEPTH, LRU_BLOCKS, LRU_BD, LRU_BD), LRU_BD ** -0.5),
        'lru_ba': nrm(ks[19], (DEPTH, LRU_W), 0.01),
        'lru_wx': nrm(ks[22], (DEPTH, LRU_BLOCKS, LRU_BD, LRU_BD), LRU_BD ** -0.5),
        'lru_bx': nrm(ks[23], (DEPTH, LRU_W), 0.01),
        'lru_lambda': jnp.log(sig_l) - jnp.log1p(-sig_l),
        'w_out': nrm(ks[24], (DEPTH, D_MIX, D_MODEL), D_MIX ** -0.5),
        'norm_ffn2': gain(ks[25], (DEPTH, D_MODEL)),
        'w_ffn2_in': nrm(ks[26], (DEPTH, D_MODEL, 2 * D_FF), D_MODEL ** -0.5),
        'w_ffn2_out': nrm(ks[27], (DEPTH, D_FF, D_MODEL), D_FF ** -0.5),
        'norm_final': gain(ks[28], (D_MODEL,)),
    }


def reference(x_prompt, x_sample, state_dn, state_dn_conv, state_lru, state_lru_conv, state_ret,
              meta_tokens, norm_ffn1, w_ffn1_in, w_ffn1_out, norm_mix, w_in, dn_conv_w, dn_a_log,
              dn_dt_bias, dn_norm_w, lru_conv_w, lru_conv_b, lru_wa, lru_ba, lru_wx, lru_bx, lru_lambda,
              w_out, norm_ffn2, w_ffn2_in, w_ffn2_out, norm_final):
    B = x_prompt.shape[0]
    meta = jnp.broadcast_to(meta_tokens.astype(x_prompt.dtype)[None], (B, N_META, D_MODEL))
    xp = jnp.concatenate([meta, x_prompt], axis=1)
    xs = x_sample
    pos_p = jnp.arange(xp.shape[1])
    pos_s = PAST_LEN + jnp.arange(xs.shape[1])
    new_p, new_s = [], []
    for l in range(DEPTH):
        w_l = (norm_ffn1[l], w_ffn1_in[l], w_ffn1_out[l], norm_mix[l], w_in[l], dn_conv_w[l], dn_a_log[l],
               dn_dt_bias[l], dn_norm_w[l], lru_conv_w[l], lru_conv_b[l], lru_wa[l], lru_ba[l], lru_wx[l],
               lru_bx[l], lru_lambda[l], w_out[l], norm_ffn2[l], w_ffn2_in[l], w_ffn2_out[l])
        xp, st_p = _layer(xp, pos_p, None, *w_l)
        st_in = (state_dn[l], state_dn_conv[l], state_lru[l], state_lru_conv[l], state_ret[l])
        xs, st_s = _layer(xs, pos_s, st_in, *w_l)
        new_p.append(st_p)
        new_s.append(st_s)
    y_prompt = _rmsnorm(xp, norm_final)[:, N_META:]
    y_sample = _rmsnorm(xs, norm_final)
    dn_p, dn_conv_p, lru_p, lru_conv_p, ret_p = [jnp.stack([s[j] for s in new_p]) for j in range(5)]
    dn_s, dn_conv_s, lru_s, lru_conv_s, ret_s = [jnp.stack([s[j] for s in new_s]) for j in range(5)]
    return (y_prompt, y_sample, dn_p, dn_conv_p, lru_p, lru_conv_p, ret_p,
            dn_s, dn_conv_s, lru_s, lru_conv_s, ret_s)
```

```python
import functools
import math

import jax
import jax.numpy as jnp
from jax import lax
from jax.experimental import pallas as pl
from jax.experimental.pallas import tpu as pltpu

F32 = jnp.float32
BF16 = jnp.bfloat16

D_MODEL = 1024
N_META = 16
PAST_LEN = 16384
DN_HEADS = 4
DN_DK = 128
DN_W = 512
LRU_W = 256
LRU_BLOCKS = 4
LRU_C = 8.0
RET_HEADS = 4
RET_DK = 64
RET_W = 256
CONV_W = 4
CHUNK = 64
D_FF = 2816
ROPE_BASE = 10000.0
EPS = 1e-6
SUB = 16
LOG_GAMMA = tuple(math.log1p(-2.0 ** (-5.0 - h)) for h in range(RET_HEADS))

VMEM_LIMIT_BYTES = 56 * 1024 * 1024


def _mm(a, b):
    return jnp.dot(a.astype(BF16), b.astype(BF16), preferred_element_type=F32)


def _mm_nt(a, b):
    return lax.dot_general(a.astype(BF16), b.astype(BF16), (((1,), (1,)), ((), ())),
                           preferred_element_type=F32)


def _mm_tn(a, b):
    return lax.dot_general(a.astype(BF16), b.astype(BF16), (((0,), (0,)), ((), ())),
                           preferred_element_type=F32)


def _mm_f32(a, b):
    return jnp.dot(a, b, precision=lax.Precision.HIGHEST, preferred_element_type=F32)


def _mm_split(x, m_bf16):
    hi = x.astype(BF16)
    lo = (x - hi.astype(F32)).astype(BF16)
    return (jnp.dot(hi, m_bf16, preferred_element_type=F32)
            + jnp.dot(lo, m_bf16, preferred_element_type=F32))


def _rmsnorm(x, w):
    return x * lax.rsqrt(jnp.mean(x * x, axis=-1, keepdims=True) + EPS) * w


def _silu(x):
    return x * jax.nn.sigmoid(x)


def _softplus(x):
    return jnp.maximum(x, 0.0) + jnp.log1p(jnp.exp(-jnp.abs(x)))


def _iota(shape, dim):
    return lax.broadcasted_iota(jnp.int32, shape, dim)


def _lane_log_gamma(shape, dim, width):
    head = _iota(shape, dim) // width
    out = jnp.full(shape, LOG_GAMMA[0], F32)
    for h in range(1, RET_HEADS):
        out = jnp.where(head == h, LOG_GAMMA[h], out)
    return out


def _rope(x, cos, sin_signed):
    n = x.shape[1]
    half = RET_DK // 2
    first = (_iota(x.shape, 1) % RET_DK) < half
    swapped = jnp.where(first, pltpu.roll(x, n - half, 1), pltpu.roll(x, half, 1))
    return x * cos + swapped * sin_signed


def _ret_norm_gate(o, gate, avg_bf16):
    mu = _mm_split(o, avg_bf16)
    d = o - mu
    var = _mm_split(d * d, avg_bf16)
    return d * lax.rsqrt(var + EPS) * _silu(gate)


def _head_avg_matrix():
    r = _iota((RET_W, RET_W), 0) // RET_DK
    c = _iota((RET_W, RET_W), 1) // RET_DK
    return jnp.where(r == c, 1.0 / RET_DK, 0.0).astype(BF16)


def _lru_coeffs(xl, wgate, bgate, lam):
    gates = _mm(xl, wgate) + bgate
    r = jax.nn.sigmoid(gates[:, :LRU_W])
    i = jax.nn.sigmoid(gates[:, LRU_W:])
    log_a = -LRU_C * r * _softplus(-lam)
    a = jnp.exp(log_a)
    b = jnp.sqrt(jnp.maximum(-jnp.tanh(log_a) * (a * a + 1.0), 0.0)) * (i * xl)
    return a, b


def _ffn_body(*refs, final):
    if final:
        x_ref, nw_ref, wg_ref, wu_ref, wo_ref, fw_ref, o_ref, u_ref, acc_ref = refs
    else:
        x_ref, nw_ref, wg_ref, wu_ref, wo_ref, o_ref, u_ref, acc_ref = refs
    j = pl.program_id(1)

    @pl.when(j == 0)
    def _():
        u_ref[...] = _rmsnorm(x_ref[...], nw_ref[...]).astype(BF16)
        acc_ref[...] = jnp.zeros_like(acc_ref)

    u = u_ref[...]
    gate = jnp.dot(u, wg_ref[...], preferred_element_type=F32)
    up = jnp.dot(u, wu_ref[...], preferred_element_type=F32)
    h = (_silu(gate) * up).astype(BF16)
    acc_ref[...] += jnp.dot(h, wo_ref[...], preferred_element_type=F32)

    @pl.when(j == pl.num_programs(1) - 1)
    def _():
        y = x_ref[...] + 0.5 * acc_ref[...]
        if final:
            y = _rmsnorm(y, fw_ref[...])
        o_ref[...] = y


def _ffn(x, norm_w, w_in, w_out, final_w=None, *, tm, tf):
    n = x.shape[0]
    nf = D_FF // tf
    final = final_w is not None
    in_specs = [
        pl.BlockSpec((tm, D_MODEL), lambda i, j: (i, 0)),
        pl.BlockSpec((1, D_MODEL), lambda i, j: (0, 0)),
        pl.BlockSpec((D_MODEL, tf), lambda i, j: (0, j)),
        pl.BlockSpec((D_MODEL, tf), lambda i, j: (0, j + nf)),
        pl.BlockSpec((tf, D_MODEL), lambda i, j: (j, 0)),
    ]
    args = [x, norm_w.reshape(1, D_MODEL), w_in, w_in, w_out]
    if final:
        in_specs.append(pl.BlockSpec((1, D_MODEL), lambda i, j: (0, 0)))
        args.append(final_w.reshape(1, D_MODEL))
    return pl.pallas_call(
        functools.partial(_ffn_body, final=final),
        grid=(n // tm, nf),
        in_specs=in_specs,
        out_specs=pl.BlockSpec((tm, D_MODEL), lambda i, j: (i, 0)),
        out_shape=jax.ShapeDtypeStruct((n, D_MODEL), F32),
        scratch_shapes=[pltpu.VMEM((tm, D_MODEL), BF16), pltpu.VMEM((tm, D_MODEL), F32)],
        compiler_params=pltpu.CompilerParams(
            dimension_semantics=("arbitrary", "arbitrary"), vmem_limit_bytes=VMEM_LIMIT_BYTES),
        name="ffn_final" if final else "ffn",
    )(*args)


def _tri_inv(a, n_sub):
    m = a.shape[0]
    r = _iota((m, m), 0)
    c = _iota((m, m), 1)
    eye = jnp.where(r == c, 1.0, 0.0).astype(F32)
    x = jnp.where((r // SUB) == (c // SUB), a, 0.0)
    p = eye - x
    for _ in range(3):
        x = _mm(x, x)
        p = p + _mm(p, x)
    if n_sub == 1:
        return p
    n = _mm(p, a - jnp.where((r // SUB) == (c // SUB), a, 0.0))
    n2 = _mm(n, n)
    rr = eye - n + n2 - _mm(n, n2)
    return _mm(rr, p)


def _mix_seq_body(x_ref, nw_ref, wqkv_ref, wab_ref, wabt_ref, wrest_ref, dncw_ref,
                  alog_r_ref, dtb_r_ref, alog_c_ref, dtb_c_ref, dnnw_ref,
                  lcw_ref, lcb_ref, wgate_ref, bgate_ref, lam_ref, cos_ref, sin_ref, wout_ref,
                  sdn0_ref, dnt0_ref, h0_ref, lrt0_ref, sret0_ref,
                  y_ref, sdn_o_ref, dnt_o_ref, h_o_ref, lrt_o_ref, sretbd_o_ref, sret_o_ref,
                  sdn, sret, hst, qkvbuf, lrubuf, q_s, k_s, v_s, gcol_s, bcol_s, grow_s,
                  odn_s, rq_s, rk_s, rv_s, oret_s, omix_s, *, tc, ck):
    t = pl.program_id(1)
    nc = tc // ck
    hc = DN_HEADS * ck
    n_sub = ck // SUB

    @pl.when(t == 0)
    def _():
        sdn[...] = sdn0_ref[...]
        sret[...] = sret0_ref[...]
        hst[...] = h0_ref[...]
        qkvbuf[0:8, :] = dnt0_ref[...]
        lrubuf[0:8, :] = lrt0_ref[...]

    x = x_ref[...]
    u = _rmsnorm(x, nw_ref[...]).astype(BF16)

    qkvbuf[8:8 + tc, :] = jnp.dot(u, wqkv_ref[...], preferred_element_type=F32)
    cw = dncw_ref[...]
    conv = qkvbuf[pl.ds(5, tc), :] * cw[0:1, :]
    for j in range(1, CONV_W):
        conv = conv + qkvbuf[pl.ds(5 + j, tc), :] * cw[j:j + 1, :]
    qkvbuf[0:8, :] = qkvbuf[tc:tc + 8, :]
    qkv = _silu(conv)
    for h in range(DN_HEADS):
        qh = qkv[:, h * DN_DK:(h + 1) * DN_DK]
        kh = qkv[:, DN_W + h * DN_DK:DN_W + (h + 1) * DN_DK]
        q_s[h] = qh * lax.rsqrt(jnp.sum(qh * qh, axis=-1, keepdims=True) + EPS) * (DN_DK ** -0.5)
        k_s[h] = kh * lax.rsqrt(jnp.sum(kh * kh, axis=-1, keepdims=True) + EPS)
        v_s[h] = qkv[:, 2 * DN_W + h * DN_DK:2 * DN_W + (h + 1) * DN_DK]

    ab = jnp.dot(u, wab_ref[...], preferred_element_type=F32)
    g_col = -jnp.exp(alog_r_ref[...]) * _softplus(ab[:, :128] + dtb_r_ref[...])
    bcol_s[...] = jax.nn.sigmoid(ab[:, 128:])
    rr = _iota((tc, tc), 0)
    cc = _iota((tc, tc), 1)
    same_chunk = (rr // ck) == (cc // ck)
    lower = jnp.where(same_chunk & (cc <= rr), 1.0, 0.0).astype(F32)
    upper = jnp.where(same_chunk & (rr <= cc), 1.0, 0.0).astype(F32)
    gcol_s[...] = _mm_f32(lower, g_col)
    abt = lax.dot_general(wabt_ref[...], u, (((1,), (1,)), ((), ())), preferred_element_type=F32)
    g_row = -jnp.exp(alog_c_ref[...]) * _softplus(abt + dtb_c_ref[...])
    big_g_row = _mm_f32(g_row, upper)
    for c in range(nc):
        grow_s[c] = jnp.concatenate(
            [big_g_row[h:h + 1, c * ck:(c + 1) * ck] for h in range(DN_HEADS)], axis=1)

    rest = jnp.dot(u, wrest_ref[...], preferred_element_type=F32)
    dn_z = rest[:, 0:512]
    lru_x = rest[:, 512:768]
    lru_y = rest[:, 768:1024]
    ret_g = rest[:, 1792:2048]

    lrubuf[8:8 + tc, :] = lru_x
    lw = lcw_ref[...]
    xl = lrubuf[pl.ds(5, tc), :] * lw[0:1, :]
    for j in range(1, CONV_W):
        xl = xl + lrubuf[pl.ds(5 + j, tc), :] * lw[j:j + 1, :]
    lrubuf[0:8, :] = lrubuf[tc:tc + 8, :]
    xl = xl + lcb_ref[...]
    a, b = _lru_coeffs(xl, wgate_ref[...], bgate_ref[...], lam_ref[...])
    row = _iota((tc, LRU_W), 0)
    b = b + jnp.where(row == 0, a * hst[...], 0.0)
    s = 1
    while s < tc:
        keep = row >= s
        b = a * jnp.where(keep, pltpu.roll(b, s, 0), 0.0) + b
        a = a * jnp.where(keep, pltpu.roll(a, s, 0), 1.0)
        s *= 2
    hst[...] = b[tc - 1:tc, :]
    omix_s[:, 512:768] = (b * jax.nn.gelu(lru_y)).astype(BF16)

    cos = cos_ref[...]
    sin = sin_ref[...]
    rq_s[...] = _rope(rest[:, 1024:1280], cos, sin)
    rk_s[...] = _rope(rest[:, 1280:1536], cos, sin) * (RET_DK ** -0.5)
    rv_s[...] = rest[:, 1536:1792]

    sr = _iota((hc, hc), 0)
    sc = _iota((hc, hc), 1)
    same_head = (sr // ck) == (sc // ck)
    strict = same_head & (sr > sc)
    eye_hc = jnp.where(sr == sc, 1.0, 0.0).astype(F32)
    lg_rows = _lane_log_gamma((hc, ck), 0, ck)
    tpos = _iota((hc, ck), 0) % ck
    ipos = _iota((hc, ck), 1)
    causal = tpos >= ipos
    ret_dec = jnp.where(causal, jnp.exp(jnp.where(causal, (tpos - ipos).astype(F32) * lg_rows, 0.0)), 0.0)
    head_sel = (_iota((hc, RET_W), 0) // ck) == (_iota((hc, RET_W), 1) // RET_DK)
    lg_lane = _lane_log_gamma((ck, RET_W), 1, RET_DK)
    tl = _iota((ck, RET_W), 0).astype(F32)
    ret_eg = jnp.exp((tl + 1.0) * lg_lane)
    ret_kdec = jnp.exp((ck - 1.0 - tl) * lg_lane)
    ret_gc = jnp.exp(ck * _lane_log_gamma((1, RET_W), 1, RET_DK))
    bd = (_iota((RET_W, RET_W), 0) // RET_DK) == (_iota((RET_W, RET_W), 1) // RET_DK)

    def chunk(c, carry):
        r0 = pl.multiple_of(c * ck, ck)
        kst = jnp.concatenate([k_s[h, pl.ds(r0, ck), :] for h in range(DN_HEADS)], axis=0)
        qst = jnp.concatenate([q_s[h, pl.ds(r0, ck), :] for h in range(DN_HEADS)], axis=0)
        vst = jnp.concatenate([v_s[h, pl.ds(r0, ck), :] for h in range(DN_HEADS)], axis=0)
        gc = gcol_s[pl.ds(r0, ck), :]
        bc = bcol_s[pl.ds(r0, ck), :]
        gst = jnp.concatenate([gc[:, h:h + 1] for h in range(DN_HEADS)], axis=0)
        bst = jnp.concatenate([bc[:, h:h + 1] for h in range(DN_HEADS)], axis=0)
        diff = gst - grow_s[c]
        dstrict = jnp.where(strict, jnp.exp(jnp.where(strict, diff, 0.0)), 0.0)
        eg = jnp.exp(gst)
        amat = bst * _mm_nt(kst, kst) * dstrict
        pmat = _mm_nt(qst, kst) * (dstrict + eye_hc)
        tinv = _tri_inv(amat, n_sub)
        sol = _mm(tinv, jnp.concatenate([bst * vst, (bst * eg) * kst], axis=1))
        u_heads, qs_heads = [], []
        for h in range(DN_HEADS):
            lo, hi = h * ck, (h + 1) * ck
            s0 = sdn[h]
            u_heads.append(sol[lo:hi, :DN_DK] - _mm(sol[lo:hi, DN_DK:], s0))
            qs_heads.append(_mm(eg[lo:hi] * qst[lo:hi], s0))
        ust = jnp.concatenate(u_heads, axis=0)
        ost = jnp.concatenate(qs_heads, axis=0) + _mm(pmat, ust)
        for h in range(DN_HEADS):
            lo, hi = h * ck, (h + 1) * ck
            odn_s[pl.ds(r0, ck), h * DN_DK:(h + 1) * DN_DK] = ost[lo:hi]
            g_last = gst[hi - 1:hi]
            kd = kst[lo:hi] * jnp.exp(g_last - gst[lo:hi])
            sdn[h] = jnp.exp(g_last) * sdn[h] + _mm_tn(kd, u_heads[h])
        qc = rq_s[pl.ds(r0, ck), :]
        kc = rk_s[pl.ds(r0, ck), :]
        vc = rv_s[pl.ds(r0, ck), :]
        s0 = sret[...]
        qp = jnp.where(head_sel, jnp.concatenate([qc] * RET_HEADS, axis=0), 0.0)
        op = jnp.where(head_sel, _mm(_mm_nt(qp, kc) * ret_dec, vc), 0.0)
        o_intra = op[0:ck]
        for h in range(1, RET_HEADS):
            o_intra = o_intra + op[h * ck:(h + 1) * ck]
        oret_s[pl.ds(r0, ck), :] = ret_eg * _mm(qc, s0) + o_intra
        sret[...] = ret_gc * s0 + jnp.where(bd, _mm_tn(kc * ret_kdec, vc), 0.0)
        return carry

    lax.fori_loop(0, nc, chunk, 0)

    nw = dnnw_ref[...]
    for h in range(DN_HEADS):
        oh = odn_s[:, h * DN_DK:(h + 1) * DN_DK]
        oh = oh * lax.rsqrt(jnp.mean(oh * oh, axis=-1, keepdims=True) + EPS) * nw
        omix_s[:, h * DN_DK:(h + 1) * DN_DK] = (oh * _silu(dn_z[:, h * DN_DK:(h + 1) * DN_DK])).astype(BF16)
    omix_s[:, 768:1024] = _ret_norm_gate(oret_s[...], ret_g, _head_avg_matrix()).astype(BF16)
    y_ref[...] = x + jnp.dot(omix_s[...], wout_ref[...], preferred_element_type=F32)

    @pl.when(t == pl.num_programs(1) - 1)
    def _():
        sdn_o_ref[0] = sdn[...]
        dnt_o_ref[0] = qkvbuf[0:8, :]
        h_o_ref[0] = hst[...]
        lrt_o_ref[0] = lrubuf[0:8, :]
        s_bd = sret[...]
        sretbd_o_ref[0] = s_bd
        for h in range(RET_HEADS):
            sret_o_ref[0, h] = s_bd[h * RET_DK:(h + 1) * RET_DK, h * RET_DK:(h + 1) * RET_DK]


def _mix_seq(x, lw, cos, sin, init, *, batch, seq, tc, ck):
    nt = seq // tc
    nc = tc // ck
    hc = DN_HEADS * ck
    const = lambda shape: pl.BlockSpec(shape, lambda b, t: (0,) * len(shape))
    in_specs = [
        pl.BlockSpec((tc, D_MODEL), lambda b, t: (b * nt + t, 0)),
        const((1, D_MODEL)), const((D_MODEL, 3 * DN_W)), const((D_MODEL, 256)), const((8, D_MODEL)),
        const((D_MODEL, 2048)), const((CONV_W, 3 * DN_W)),
        const((1, 128)), const((1, 128)), const((8, 1)), const((8, 1)), const((1, DN_DK)),
        const((CONV_W, LRU_W)), const((1, LRU_W)), const((LRU_W, 2 * LRU_W)), const((1, 2 * LRU_W)),
        const((1, LRU_W)),
        pl.BlockSpec((tc, RET_W), lambda b, t: (t, 0)), pl.BlockSpec((tc, RET_W), lambda b, t: (t, 0)),
        const((D_MODEL, D_MODEL)),
        const((DN_HEADS, DN_DK, DN_DK)), const((8, 3 * DN_W)), const((1, LRU_W)), const((8, LRU_W)),
        const((RET_W, RET_W)),
    ]
    out_shape = [
        jax.ShapeDtypeStruct((batch * seq, D_MODEL), F32),
        jax.ShapeDtypeStruct((batch, DN_HEADS, DN_DK, DN_DK), F32),
        jax.ShapeDtypeStruct((batch, 8, 3 * DN_W), F32),
        jax.ShapeDtypeStruct((batch, 1, LRU_W), F32),
        jax.ShapeDtypeStruct((batch, 8, LRU_W), F32),
        jax.ShapeDtypeStruct((batch, RET_W, RET_W), F32),
        jax.ShapeDtypeStruct((batch, RET_HEADS, RET_DK, RET_DK), F32),
    ]
    out_specs = [
        pl.BlockSpec((tc, D_MODEL), lambda b, t: (b * nt + t, 0)),
        pl.BlockSpec((1, DN_HEADS, DN_DK, DN_DK), lambda b, t: (b, 0, 0, 0)),
        pl.BlockSpec((1, 8, 3 * DN_W), lambda b, t: (b, 0, 0)),
        pl.BlockSpec((1, 1, LRU_W), lambda b, t: (b, 0, 0)),
        pl.BlockSpec((1, 8, LRU_W), lambda b, t: (b, 0, 0)),
        pl.BlockSpec((1, RET_W, RET_W), lambda b, t: (b, 0, 0)),
        pl.BlockSpec((1, RET_HEADS, RET_DK, RET_DK), lambda b, t: (b, 0, 0, 0)),
    ]
    scratch = [
        pltpu.VMEM((DN_HEADS, DN_DK, DN_DK), F32),
        pltpu.VMEM((RET_W, RET_W), F32),
        pltpu.VMEM((1, LRU_W), F32),
        pltpu.VMEM((tc + 8, 3 * DN_W), F32),
        pltpu.VMEM((tc + 8, LRU_W), F32),
        pltpu.VMEM((DN_HEADS, tc, DN_DK), F32),
        pltpu.VMEM((DN_HEADS, tc, DN_DK), F32),
        pltpu.VMEM((DN_HEADS, tc, DN_DK), F32),
        pltpu.VMEM((tc, 128), F32),
        pltpu.VMEM((tc, 128), F32),
        pltpu.VMEM((nc, 1, hc), F32),
        pltpu.VMEM((tc, DN_W), F32),
        pltpu.VMEM((tc, RET_W), F32),
        pltpu.VMEM((tc, RET_W), F32),
        pltpu.VMEM((tc, RET_W), F32),
        pltpu.VMEM((tc, RET_W), F32),
        pltpu.VMEM((tc, D_MODEL), BF16),
    ]
    args = [x, lw["norm_mix"], lw["wqkv"], lw["wab"], lw["wabt"], lw["wrest"], lw["dn_conv_w"],
            lw["alog_r"], lw["dtb_r"], lw["alog_c"], lw["dtb_c"], lw["dn_norm_w"],
            lw["lru_conv_w"], lw["lru_conv_b"], lw["wgate"], lw["bgate"], lw["lam"], cos, sin,
            lw["w_out"], *init]
    return pl.pallas_call(
        functools.partial(_mix_seq_body, tc=tc, ck=ck),
        grid=(batch, nt),
        in_specs=in_specs,
        out_specs=out_specs,
        out_shape=out_shape,
        scratch_shapes=scratch,
        compiler_params=pltpu.CompilerParams(
            dimension_semantics=("arbitrary", "arbitrary"), vmem_limit_bytes=VMEM_LIMIT_BYTES),
        name=f"mix_seq_c{ck}",
    )(*args)


def _mix_step_body(x_ref, nw_ref, wqkv_ref, wab_ref, wrest_ref, dncw_ref, alog_r_ref, dtb_r_ref,
                   dnnw_ref, lcw_ref, lcb_ref, wgate_ref, bgate_ref, lam_ref, cos_ref, sin_ref,
                   wout_ref, sdn_ref, dnc_ref, h_ref, lrc_ref, sret_ref,
                   y_ref, sdn_o_ref, dnc_o_ref, h_o_ref, lrc_o_ref, sret_o_ref,
                   q_s, k_s, v_s, eg_s, beta_s, odn_s, rq_s, rk_s, rv_s, oret_s, *, bb):
    x = x_ref[...]
    u = _rmsnorm(x, nw_ref[...]).astype(BF16)

    qkv_pre = jnp.dot(u, wqkv_ref[...], preferred_element_type=F32)
    cw = dncw_ref[...]
    conv = (dnc_ref[0] * cw[0:1, :] + dnc_ref[1] * cw[1:2, :] + dnc_ref[2] * cw[2:3, :]
            + qkv_pre * cw[3:4, :])
    dnc_o_ref[0] = dnc_ref[1]
    dnc_o_ref[1] = dnc_ref[2]
    dnc_o_ref[2] = qkv_pre
    qkv = _silu(conv)
    for h in range(DN_HEADS):
        sl = slice(h * DN_DK, (h + 1) * DN_DK)
        qh = qkv[:, sl]
        kh = qkv[:, DN_W + h * DN_DK:DN_W + (h + 1) * DN_DK]
        q_s[:, sl] = qh * lax.rsqrt(jnp.sum(qh * qh, axis=-1, keepdims=True) + EPS) * (DN_DK ** -0.5)
        k_s[:, sl] = kh * lax.rsqrt(jnp.sum(kh * kh, axis=-1, keepdims=True) + EPS)
    v_s[...] = qkv[:, 2 * DN_W:]
    ab = jnp.dot(u, wab_ref[...], preferred_element_type=F32)
    eg_s[...] = jnp.exp(-jnp.exp(alog_r_ref[...]) * _softplus(ab[:, :128] + dtb_r_ref[...]))
    beta_s[...] = jax.nn.sigmoid(ab[:, 128:])

    rest = jnp.dot(u, wrest_ref[...], preferred_element_type=F32)
    dn_z = rest[:, 0:512]
    lru_x = rest[:, 512:768]
    lru_y = rest[:, 768:1024]
    ret_g = rest[:, 1792:2048]

    lw = lcw_ref[...]
    xl = (lrc_ref[0] * lw[0:1, :] + lrc_ref[1] * lw[1:2, :] + lrc_ref[2] * lw[2:3, :]
          + lru_x * lw[3:4, :] + lcb_ref[...])
    lrc_o_ref[0] = lrc_ref[1]
    lrc_o_ref[1] = lrc_ref[2]
    lrc_o_ref[2] = lru_x
    a, b = _lru_coeffs(xl, wgate_ref[...], bgate_ref[...], lam_ref[...])
    h_new = a * h_ref[...] + b
    h_o_ref[...] = h_new
    o_lru = h_new * jax.nn.gelu(lru_y)

    cos = cos_ref[...]
    sin = sin_ref[...]
    rq_s[...] = _rope(rest[:, 1024:1280], cos, sin)
    rk_s[...] = _rope(rest[:, 1280:1536], cos, sin) * (RET_DK ** -0.5)
    rv_s[...] = rest[:, 1536:1792]

    def per_tile(i8, carry):
        r0 = pl.multiple_of(i8 * 8, 8)
        eg_t = eg_s[pl.ds(r0, 8), :]
        beta_t = beta_s[pl.ds(r0, 8), :]
        for h in range(DN_HEADS):
            sl = slice(h * DN_DK, (h + 1) * DN_DK)
            kt = k_s[pl.ds(r0, 8), sl].T
            qt = q_s[pl.ds(r0, 8), sl].T
            v_t = v_s[pl.ds(r0, 8), sl]
            rows = []
            for j in range(8):
                kcol = kt[:, j:j + 1]
                qcol = qt[:, j:j + 1]
                s0 = sdn_ref[r0 + j, h]
                eg = eg_t[j:j + 1, h:h + 1]
                ks = jnp.sum(s0 * kcol, axis=0, keepdims=True)
                qs = jnp.sum(s0 * qcol, axis=0, keepdims=True)
                qk = jnp.sum(qcol * kcol, axis=0, keepdims=True)
                uu = beta_t[j:j + 1, h:h + 1] * (v_t[j:j + 1, :] - eg * ks)
                rows.append(eg * qs + qk * uu)
                sdn_o_ref[r0 + j, h] = eg * s0 + kcol * uu
            odn_s[pl.ds(r0, 8), sl] = jnp.concatenate(rows, axis=0)
        kt = rk_s[pl.ds(r0, 8), :].T
        qt = rq_s[pl.ds(r0, 8), :].T
        v_t = rv_s[pl.ds(r0, 8), :]
        rows = []
        for j in range(8):
            pieces = []
            for h in range(RET_HEADS):
                sl = slice(h * RET_DK, (h + 1) * RET_DK)
                kcol = kt[sl, j:j + 1]
                qcol = qt[sl, j:j + 1]
                s0 = sret_ref[r0 + j, h]
                gamma = math.exp(LOG_GAMMA[h])
                qs = jnp.sum(s0 * qcol, axis=0, keepdims=True)
                qk = jnp.sum(qcol * kcol, axis=0, keepdims=True)
                pieces.append(gamma * qs + qk * v_t[j:j + 1, sl])
                sret_o_ref[r0 + j, h] = gamma * s0 + kcol * v_t[j:j + 1, sl]
            rows.append(jnp.concatenate(pieces, axis=1))
        oret_s[pl.ds(r0, 8), :] = jnp.concatenate(rows, axis=0)
        return carry

    lax.fori_loop(0, bb // 8, per_tile, 0)

    nw = dnnw_ref[...]
    parts = []
    for h in range(DN_HEADS):
        sl = slice(h * DN_DK, (h + 1) * DN_DK)
        oh = odn_s[:, sl]
        oh = oh * lax.rsqrt(jnp.mean(oh * oh, axis=-1, keepdims=True) + EPS) * nw
        parts.append((oh * _silu(dn_z[:, sl])).astype(BF16))
    parts.append(o_lru.astype(BF16))
    parts.append(_ret_norm_gate(oret_s[...], ret_g, _head_avg_matrix()).astype(BF16))
    y_ref[...] = x + jnp.dot(jnp.concatenate(parts, axis=1), wout_ref[...], preferred_element_type=F32)


def _mix_step(x, lw, cos, sin, sdn, dnc, hl, lrc, sret, *, bb):
    n = x.shape[0]
    const = lambda shape: pl.BlockSpec(shape, lambda i: (0,) * len(shape))
    in_specs = [
        pl.BlockSpec((bb, D_MODEL), lambda i: (i, 0)),
        const((1, D_MODEL)), const((D_MODEL, 3 * DN_W)), const((D_MODEL, 256)), const((D_MODEL, 2048)),
        const((CONV_W, 3 * DN_W)), const((1, 128)), const((1, 128)), const((1, DN_DK)),
        const((CONV_W, LRU_W)), const((1, LRU_W)), const((LRU_W, 2 * LRU_W)), const((1, 2 * LRU_W)),
        const((1, LRU_W)), const((1, RET_W)), const((1, RET_W)), const((D_MODEL, D_MODEL)),
        pl.BlockSpec((bb, DN_HEADS, DN_DK, DN_DK), lambda i: (i, 0, 0, 0)),
        pl.BlockSpec((CONV_W - 1, bb, 3 * DN_W), lambda i: (0, i, 0)),
        pl.BlockSpec((bb, LRU_W), lambda i: (i, 0)),
        pl.BlockSpec((CONV_W - 1, bb, LRU_W), lambda i: (0, i, 0)),
        pl.BlockSpec((bb, RET_HEADS, RET_DK, RET_DK), lambda i: (i, 0, 0, 0)),
    ]
    out_shape = [
        jax.ShapeDtypeStruct((n, D_MODEL), F32),
        jax.ShapeDtypeStruct((n, DN_HEADS, DN_DK, DN_DK), F32),
        jax.ShapeDtypeStruct((CONV_W - 1, n, 3 * DN_W), F32),
        jax.ShapeDtypeStruct((n, LRU_W), F32),
        jax.ShapeDtypeStruct((CONV_W - 1, n, LRU_W), F32),
        jax.ShapeDtypeStruct((n, RET_HEADS, RET_DK, RET_DK), F32),
    ]
    out_specs = [
        pl.BlockSpec((bb, D_MODEL), lambda i: (i, 0)),
        pl.BlockSpec((bb, DN_HEADS, DN_DK, DN_DK), lambda i: (i, 0, 0, 0)),
        pl.BlockSpec((CONV_W - 1, bb, 3 * DN_W), lambda i: (0, i, 0)),
        pl.BlockSpec((bb, LRU_W), lambda i: (i, 0)),
        pl.BlockSpec((CONV_W - 1, bb, LRU_W), lambda i: (0, i, 0)),
        pl.BlockSpec((bb, RET_HEADS, RET_DK, RET_DK), lambda i: (i, 0, 0, 0)),
    ]
    scratch = [
        pltpu.VMEM((bb, DN_W), F32), pltpu.VMEM((bb, DN_W), F32), pltpu.VMEM((bb, DN_W), F32),
        pltpu.VMEM((bb, 128), F32), pltpu.VMEM((bb, 128), F32), pltpu.VMEM((bb, DN_W), F32),
        pltpu.VMEM((bb, RET_W), F32), pltpu.VMEM((bb, RET_W), F32), pltpu.VMEM((bb, RET_W), F32),
        pltpu.VMEM((bb, RET_W), F32),
    ]
    args = [x, lw["norm_mix"], lw["wqkv"], lw["wab"], lw["wrest"], lw["dn_conv_w"], lw["alog_r"],
            lw["dtb_r"], lw["dn_norm_w"], lw["lru_conv_w"], lw["lru_conv_b"], lw["wgate"], lw["bgate"],
            lw["lam"], cos, sin, lw["w_out"], sdn, dnc, hl, lrc, sret]
    return pl.pallas_call(
        functools.partial(_mix_step_body, bb=bb),
        grid=(n // bb,),
        in_specs=in_specs,
        out_specs=out_specs,
        out_shape=out_shape,
        scratch_shapes=scratch,
        compiler_params=pltpu.CompilerParams(
            dimension_semantics=("arbitrary",), vmem_limit_bytes=VMEM_LIMIT_BYTES),
        name="mix_step",
    )(*args)


def _rope_tables(pos):
    half = RET_DK // 2
    inv = ROPE_BASE ** (-jnp.arange(half, dtype=F32) / half)
    ang = pos.astype(F32)[:, None] * inv[None, :]
    cos = jnp.cos(ang)
    sin = jnp.sin(ang)
    cos_full = jnp.tile(jnp.concatenate([cos, cos], axis=-1), (1, RET_HEADS))
    sin_signed = jnp.tile(jnp.concatenate([-sin, sin], axis=-1), (1, RET_HEADS))
    return cos_full, sin_signed


def _block_diag(w):
    n, d, e = w.shape
    eye = jnp.eye(n, dtype=w.dtype)
    return (eye[:, None, :, None] * w[:, :, None, :]).reshape(n * d, n * e)


def _layer_weights(l, norm_mix, w_in, dn_conv_w, dn_a_log, dn_dt_bias, dn_norm_w, lru_conv_w, lru_conv_b,
                   lru_wa, lru_ba, lru_wx, lru_bx, lru_lambda, w_out):
    wl = w_in[l]
    o_a = 3 * DN_W
    wa = wl[:, o_a:o_a + DN_HEADS]
    wb = wl[:, o_a + DN_HEADS:o_a + 2 * DN_HEADS]
    pad_cols = lambda w: jnp.pad(w, ((0, 0), (0, 128 - DN_HEADS)))
    pad_lane = lambda v: jnp.pad(v, (0, 128 - DN_HEADS)).reshape(1, 128)
    pad_sub = lambda v: jnp.pad(v, (0, 8 - DN_HEADS)).reshape(8, 1)
    return {
        "norm_mix": norm_mix[l].reshape(1, D_MODEL),
        "wqkv": wl[:, :o_a].astype(BF16),
        "wab": jnp.concatenate([pad_cols(wa), pad_cols(wb)], axis=1).astype(BF16),
        "wabt": jnp.pad(wa.T, ((0, 8 - DN_HEADS), (0, 0))).astype(BF16),
        "wrest": wl[:, o_a + 2 * DN_HEADS:].astype(BF16),
        "dn_conv_w": dn_conv_w[l],
        "alog_r": pad_lane(dn_a_log[l]), "dtb_r": pad_lane(dn_dt_bias[l]),
        "alog_c": pad_sub(dn_a_log[l]), "dtb_c": pad_sub(dn_dt_bias[l]),
        "dn_norm_w": dn_norm_w[l].reshape(1, DN_DK),
        "lru_conv_w": lru_conv_w[l], "lru_conv_b": lru_conv_b[l].reshape(1, LRU_W),
        "wgate": jnp.concatenate([_block_diag(lru_wa[l]), _block_diag(lru_wx[l])], axis=1).astype(BF16),
        "bgate": jnp.concatenate([lru_ba[l], lru_bx[l]]).reshape(1, 2 * LRU_W),
        "lam": lru_lambda[l].reshape(1, LRU_W),
        "w_out": w_out[l].astype(BF16),
    }


def kernel(x_prompt, x_sample, state_dn, state_dn_conv, state_lru, state_lru_conv, state_ret, meta_tokens, norm_ffn1, w_ffn1_in, w_ffn1_out, norm_mix, w_in, dn_conv_w, dn_a_log, dn_dt_bias, dn_norm_w, lru_conv_w, lru_conv_b, lru_wa, lru_ba, lru_wx, lru_bx, lru_lambda, w_out, norm_ffn2, w_ffn2_in, w_ffn2_out, norm_final):
    batch, seq, _ = x_prompt.shape
    n_dec = x_sample.shape[0]
    depth = w_in.shape[0]
    f1_in, f1_out = w_ffn1_in.astype(BF16), w_ffn1_out.astype(BF16)
    f2_in, f2_out = w_ffn2_in.astype(BF16), w_ffn2_out.astype(BF16)
    lws = [_layer_weights(l, norm_mix, w_in, dn_conv_w, dn_a_log, dn_dt_bias, dn_norm_w, lru_conv_w,
                          lru_conv_b, lru_wa, lru_ba, lru_wx, lru_bx, lru_lambda, w_out)
           for l in range(depth)]
    cos_m, sin_m = _rope_tables(jnp.arange(N_META))
    cos_p, sin_p = _rope_tables(N_META + jnp.arange(seq))
    cos_s, sin_s = _rope_tables(PAST_LEN + jnp.arange(1))

    n_small = n_dec + N_META
    xs = jnp.concatenate([x_sample[:, 0, :], meta_tokens.astype(F32)], axis=0)
    zero_init = (jnp.zeros((DN_HEADS, DN_DK, DN_DK), F32), jnp.zeros((8, 3 * DN_W), F32),
                 jnp.zeros((1, LRU_W), F32), jnp.zeros((8, LRU_W), F32), jnp.zeros((RET_W, RET_W), F32))
    new_s, meta_state = [], []
    for l in range(depth):
        last = l == depth - 1
        xs = _ffn(xs, norm_ffn1[l], f1_in[l], f1_out[l], tm=n_small, tf=D_FF // 2)
        ys, sdn, dnc, hl, lrc, sret = _mix_step(
            xs[:n_dec], lws[l], cos_s, sin_s, state_dn[l], jnp.swapaxes(state_dn_conv[l], 0, 1),
            state_lru[l], jnp.swapaxes(state_lru_conv[l], 0, 1), state_ret[l], bb=16)
        ym, m_sdn, m_dnt, m_h, m_lrt, m_sretbd, _ = _mix_seq(
            xs[n_dec:], lws[l], cos_m, sin_m, zero_init, batch=1, seq=N_META, tc=N_META, ck=N_META)
        new_s.append((sdn, jnp.swapaxes(dnc, 0, 1), hl, jnp.swapaxes(lrc, 0, 1), sret))
        meta_state.append((m_sdn[0], m_dnt[0], m_h[0], m_lrt[0], m_sretbd[0]))
        xs = jnp.concatenate([ys, ym], axis=0)
        xs = _ffn(xs, norm_ffn2[l], f2_in[l], f2_out[l], norm_final if last else None,
                  tm=n_small, tf=D_FF // 2)
    y_sample = xs[:n_dec].reshape(n_dec, 1, D_MODEL)

    xp = x_prompt.reshape(batch * seq, D_MODEL)
    new_p = []
    for l in range(depth):
        last = l == depth - 1
        xp = _ffn(xp, norm_ffn1[l], f1_in[l], f1_out[l], tm=512, tf=D_FF // 2)
        xp, sdn, dnt, hl, lrt, _, sret = _mix_seq(
            xp, lws[l], cos_p, sin_p, meta_state[l], batch=batch, seq=seq, tc=256, ck=CHUNK)
        new_p.append((sdn, dnt[:, 5:8], hl[:, 0], lrt[:, 5:8], sret))
        xp = _ffn(xp, norm_ffn2[l], f2_in[l], f2_out[l], norm_final if last else None,
                  tm=512, tf=D_FF // 2)
    y_prompt = xp.reshape(batch, seq, D_MODEL)

    outs_p = [jnp.stack([s[j] for s in new_p]) for j in range(5)]
    outs_s = [jnp.stack([s[j] for s in new_s]) for j in range(5)]
    return (y_prompt, y_sample, *outs_p, *outs_s)
```

```python
import functools
import math

import jax
import jax.numpy as jnp
from jax import lax
from jax.experimental import pallas as pl
from jax.experimental.pallas import tpu as pltpu

F32 = jnp.float32
BF16 = jnp.bfloat16

D_MODEL = 1024
N_META = 16
PAST_LEN = 16384
DN_HEADS = 4
DN_DK = 128
DN_W = 512
LRU_W = 256
LRU_BLOCKS = 4
LRU_C = 8.0
RET_HEADS = 4
RET_DK = 64
RET_W = 256
CONV_W = 4
CHUNK = 64
D_FF = 2816
ROPE_BASE = 10000.0
EPS = 1e-6
SUB = 16
LOG_GAMMA = tuple(math.log1p(-2.0 ** (-5.0 - h)) for h in range(RET_HEADS))

VMEM_LIMIT_BYTES = 56 * 1024 * 1024


def _mm(a, b):
    return jnp.dot(a.astype(BF16), b.astype(BF16), preferred_element_type=F32)


def _mm_nt(a, b):
    return lax.dot_general(a.astype(BF16), b.astype(BF16), (((1,), (1,)), ((), ())),
                           preferred_element_type=F32)


def _mm_tn(a, b):
    return lax.dot_general(a.astype(BF16), b.astype(BF16), (((0,), (0,)), ((), ())),
                           preferred_element_type=F32)


def _split3(x):
    hi = x.astype(BF16)
    r = x - hi.astype(F32)
    mid = r.astype(BF16)
    lo = (r - mid.astype(F32)).astype(BF16)
    return hi, mid, lo


def _mm_01_left(m01, x):
    m = m01.astype(BF16)
    return sum(jnp.dot(m, p, preferred_element_type=F32) for p in _split3(x))


def _mm_01_right(x, m01):
    m = m01.astype(BF16)
    return sum(jnp.dot(p, m, preferred_element_type=F32) for p in _split3(x))


def _mm_split(x, m_bf16):
    hi = x.astype(BF16)
    lo = (x - hi.astype(F32)).astype(BF16)
    return (jnp.dot(hi, m_bf16, preferred_element_type=F32)
            + jnp.dot(lo, m_bf16, preferred_element_type=F32))


def _rmsnorm(x, w):
    return x * lax.rsqrt(jnp.mean(x * x, axis=-1, keepdims=True) + EPS) * w


def _silu(x):
    return x * jax.nn.sigmoid(x)


def _softplus(x):
    return jnp.maximum(x, 0.0) + jnp.log1p(jnp.exp(-jnp.abs(x)))


def _iota(shape, dim):
    return lax.broadcasted_iota(jnp.int32, shape, dim)


def _lane_log_gamma(shape, dim, width):
    head = _iota(shape, dim) // width
    out = jnp.full(shape, LOG_GAMMA[0], F32)
    for h in range(1, RET_HEADS):
        out = jnp.where(head == h, LOG_GAMMA[h], out)
    return out


def _rope(x, cos, sin_signed):
    n = x.shape[1]
    half = RET_DK // 2
    first = (_iota(x.shape, 1) % RET_DK) < half
    swapped = jnp.where(first, pltpu.roll(x, n - half, 1), pltpu.roll(x, half, 1))
    return x * cos + swapped * sin_signed


def _ret_norm_gate(o, gate, avg_bf16):
    mu = _mm_split(o, avg_bf16)
    d = o - mu
    var = _mm_split(d * d, avg_bf16)
    return d * lax.rsqrt(var + EPS) * _silu(gate)


def _head_avg_matrix():
    r = _iota((RET_W, RET_W), 0) // RET_DK
    c = _iota((RET_W, RET_W), 1) // RET_DK
    return jnp.where(r == c, 1.0 / RET_DK, 0.0).astype(BF16)


def _lru_coeffs(xl, wgate, bgate, lam):
    gates = _mm(xl, wgate) + bgate
    r = jax.nn.sigmoid(gates[:, :LRU_W])
    i = jax.nn.sigmoid(gates[:, LRU_W:])
    log_a = -LRU_C * r * _softplus(-lam)
    a = jnp.exp(log_a)
    b = jnp.sqrt(jnp.maximum(-jnp.tanh(log_a) * (a * a + 1.0), 0.0)) * (i * xl)
    return a, b


def _ffn_body(*refs, final, nf):
    if final:
        x_ref, nw_ref, wi_ref, wo_ref, fw_ref, o_ref = refs
    else:
        x_ref, nw_ref, wi_ref, wo_ref, o_ref = refs
    tf = D_FF // nf
    x = x_ref[...]
    u = _rmsnorm(x, nw_ref[...]).astype(BF16)
    acc = None
    for j in range(nf):
        gate = jnp.dot(u, wi_ref[:, j * tf:(j + 1) * tf], preferred_element_type=F32)
        up = jnp.dot(u, wi_ref[:, D_FF + j * tf:D_FF + (j + 1) * tf], preferred_element_type=F32)
        h = (_silu(gate) * up).astype(BF16)
        part = jnp.dot(h, wo_ref[j * tf:(j + 1) * tf, :], preferred_element_type=F32)
        acc = part if acc is None else acc + part
    y = x + 0.5 * acc
    if final:
        y = _rmsnorm(y, fw_ref[...])
    o_ref[...] = y


def _ffn(x, norm_w, w_in, w_out, layer, final_w=None, *, tm, nf):
    n = x.shape[0]
    final = final_w is not None
    resident = pl.Buffered(1)
    in_specs = [
        pl.BlockSpec((tm, D_MODEL), lambda i: (i, 0)),
        pl.BlockSpec((1, D_MODEL), lambda i: (0, 0)),
        pl.BlockSpec((None, D_MODEL, 2 * D_FF), lambda i: (layer, 0, 0), pipeline_mode=resident),
        pl.BlockSpec((None, D_FF, D_MODEL), lambda i: (layer, 0, 0), pipeline_mode=resident),
    ]
    args = [x, norm_w.reshape(1, D_MODEL), w_in, w_out]
    if final:
        in_specs.append(pl.BlockSpec((1, D_MODEL), lambda i: (0, 0)))
        args.append(final_w.reshape(1, D_MODEL))
    return pl.pallas_call(
        functools.partial(_ffn_body, final=final, nf=nf),
        grid=(n // tm,),
        in_specs=in_specs,
        out_specs=pl.BlockSpec((tm, D_MODEL), lambda i: (i, 0)),
        out_shape=jax.ShapeDtypeStruct((n, D_MODEL), F32),
        compiler_params=pltpu.CompilerParams(
            dimension_semantics=("arbitrary",), vmem_limit_bytes=VMEM_LIMIT_BYTES),
        name="ffn_final" if final else "ffn",
    )(*args)


def _tri_inv(a, n_sub):
    m = a[0].shape[0]
    r = _iota((m, m), 0)
    c = _iota((m, m), 1)
    eye = jnp.where(r == c, 1.0, 0.0).astype(F32)
    diag = (r // SUB) == (c // SUB)
    x = [jnp.where(diag, ai, 0.0) for ai in a]
    off = [ai - xi for ai, xi in zip(a, x)]
    p = [eye - xi for xi in x]
    for _ in range(3):
        x = [_mm(xi, xi) for xi in x]
        p = [pi + _mm(pi, xi) for pi, xi in zip(p, x)]
    if n_sub == 1:
        return p
    n = [_mm(pi, oi) for pi, oi in zip(p, off)]
    n2 = [_mm(ni, ni) for ni in n]
    rr = [eye - ni + n2i - _mm(ni, n2i) for ni, n2i in zip(n, n2)]
    return [_mm(ri, pi) for ri, pi in zip(rr, p)]


def _mix_seq_body(x_ref, nw_ref, wqkv_ref, wab_ref, wabt_ref, wrest_ref, dncw_ref,
                  alog_r_ref, dtb_r_ref, alog_c_ref, dtb_c_ref, dnnw_ref,
                  lcw_ref, lcb_ref, wgate_ref, bgate_ref, lam_ref, cos_ref, sin_ref, wout_ref,
                  sdn0_ref, dnt0_ref, h0_ref, lrt0_ref, sret0_ref,
                  y_ref, sdn_o_ref, dnt_o_ref, h_o_ref, lrt_o_ref, sretbd_o_ref, sret_o_ref,
                  sdn, sret, hst, qkvbuf, lrubuf, q_s, k_s, v_s, gcol_s, bcol_s, grow_s,
                  odn_s, oret_s, omix_s, *, tc, ck):
    t = pl.program_id(1)
    nc = tc // ck
    hc = DN_HEADS * ck
    n_sub = ck // SUB

    @pl.when(t == 0)
    def _():
        sdn[...] = sdn0_ref[...]
        sret[...] = sret0_ref[...]
        hst[...] = h0_ref[...]
        qkvbuf[0:8, :] = dnt0_ref[...]
        lrubuf[0:8, :] = lrt0_ref[...]

    x = x_ref[...]
    u = _rmsnorm(x, nw_ref[...]).astype(BF16)

    qkvbuf[8:8 + tc, :] = jnp.dot(u, wqkv_ref[...], preferred_element_type=F32)
    cw = dncw_ref[...]
    conv = qkvbuf[pl.ds(5, tc), :] * cw[0:1, :]
    for j in range(1, CONV_W):
        conv = conv + qkvbuf[pl.ds(5 + j, tc), :] * cw[j:j + 1, :]
    qkvbuf[0:8, :] = qkvbuf[tc:tc + 8, :]
    qkv = _silu(conv)
    for h in range(DN_HEADS):
        qh = qkv[:, h * DN_DK:(h + 1) * DN_DK]
        kh = qkv[:, DN_W + h * DN_DK:DN_W + (h + 1) * DN_DK]
        q_s[h] = qh * lax.rsqrt(jnp.sum(qh * qh, axis=-1, keepdims=True) + EPS) * (DN_DK ** -0.5)
        k_s[h] = kh * lax.rsqrt(jnp.sum(kh * kh, axis=-1, keepdims=True) + EPS)
        v_s[h] = qkv[:, 2 * DN_W + h * DN_DK:2 * DN_W + (h + 1) * DN_DK]

    ab = jnp.dot(u, wab_ref[...], preferred_element_type=F32)
    g_col = -jnp.exp(alog_r_ref[...]) * _softplus(ab[:, :128] + dtb_r_ref[...])
    bcol_s[...] = jax.nn.sigmoid(ab[:, 128:])
    rr = _iota((tc, tc), 0)
    cc = _iota((tc, tc), 1)
    same_chunk = (rr // ck) == (cc // ck)
    lower = jnp.where(same_chunk & (cc <= rr), 1.0, 0.0).astype(F32)
    upper = jnp.where(same_chunk & (rr <= cc), 1.0, 0.0).astype(F32)
    gcol_s[...] = _mm_01_left(lower, g_col)
    abt = lax.dot_general(wabt_ref[...], u, (((1,), (1,)), ((), ())), preferred_element_type=F32)
    g_row = -jnp.exp(alog_c_ref[...]) * _softplus(abt + dtb_c_ref[...])
    big_g_row = _mm_01_right(g_row, upper)
    for c in range(nc):
        grow_s[c] = jnp.concatenate(
            [big_g_row[h:h + 1, c * ck:(c + 1) * ck] for h in range(DN_HEADS)], axis=1)

    def proj(lo, hi):
        return jnp.dot(u, wrest_ref[:, lo:hi], preferred_element_type=F32)

    sr = _iota((hc, hc), 0)
    sc = _iota((hc, hc), 1)
    same_head = (sr // ck) == (sc // ck)
    strict = same_head & (sr > sc)
    eye_hc = jnp.where(sr == sc, 1.0, 0.0).astype(F32)
    lg_rows = _lane_log_gamma((hc, ck), 0, ck)
    tpos = _iota((hc, ck), 0) % ck
    ipos = _iota((hc, ck), 1)
    causal = tpos >= ipos
    ret_dec = jnp.where(causal, jnp.exp(jnp.where(causal, (tpos - ipos).astype(F32) * lg_rows, 0.0)), 0.0)
    head_sel = (_iota((hc, RET_W), 0) // ck) == (_iota((hc, RET_W), 1) // RET_DK)
    lg_lane = _lane_log_gamma((ck, RET_W), 1, RET_DK)
    tl = _iota((ck, RET_W), 0).astype(F32)
    ret_eg = jnp.exp((tl + 1.0) * lg_lane)
    ret_kdec = jnp.exp((ck - 1.0 - tl) * lg_lane)
    ret_gc = jnp.exp(ck * _lane_log_gamma((1, RET_W), 1, RET_DK))
    bd = (_iota((RET_W, RET_W), 0) // RET_DK) == (_iota((RET_W, RET_W), 1) // RET_DK)

    chunks = range(nc)
    rows = [slice(c * ck, (c + 1) * ck) for c in chunks]

    kst = [jnp.concatenate([k_s[h, rs, :] for h in range(DN_HEADS)], axis=0) for rs in rows]
    qst = [jnp.concatenate([q_s[h, rs, :] for h in range(DN_HEADS)], axis=0) for rs in rows]
    vst = [jnp.concatenate([v_s[h, rs, :] for h in range(DN_HEADS)], axis=0) for rs in rows]
    gst = [jnp.concatenate([gcol_s[rs, h:h + 1] for h in range(DN_HEADS)], axis=0) for rs in rows]
    bst = [jnp.concatenate([bcol_s[rs, h:h + 1] for h in range(DN_HEADS)], axis=0) for rs in rows]
    dstrict = []
    for c in chunks:
        diff = gst[c] - grow_s[c]
        dstrict.append(jnp.where(strict, jnp.exp(jnp.where(strict, diff, 0.0)), 0.0))
    eg = [jnp.exp(g) for g in gst]
    amat = [bst[c] * _mm_nt(kst[c], kst[c]) * dstrict[c] for c in chunks]
    pmat = [(_mm_nt(qst[c], kst[c]) * (dstrict[c] + eye_hc)).astype(BF16) for c in chunks]
    tinv = _tri_inv(amat, n_sub)
    sol = [_mm(tinv[c], jnp.concatenate([bst[c] * vst[c], (bst[c] * eg[c]) * kst[c]], axis=1))
           for c in chunks]
    heads = [(h * ck, (h + 1) * ck) for h in range(DN_HEADS)]
    wq = [[jnp.concatenate([sol[c][lo:hi, DN_DK:], eg[c][lo:hi] * qst[c][lo:hi]], axis=0).astype(BF16)
           for lo, hi in heads] for c in chunks]
    g_last = [[gst[c][hi - 1:hi] for lo, hi in heads] for c in chunks]
    kd = [[(kst[c][lo:hi] * jnp.exp(g_last[c][h] - gst[c][lo:hi])).astype(BF16)
           for h, (lo, hi) in enumerate(heads)] for c in chunks]

    env = {}

    def dn_post_a(c):
        ws_qs = [_mm(wq[c][h], sdn[h]) for h in range(DN_HEADS)]
        env["u"] = [sol[c][lo:hi, :DN_DK] - ws_qs[h][:ck] for h, (lo, hi) in enumerate(heads)]
        env["qs"] = jnp.concatenate([w[ck:] for w in ws_qs], axis=0)

    def dn_post_b(c):
        ost = env["qs"] + _mm(pmat[c], jnp.concatenate(env["u"], axis=0))
        for h, (lo, hi) in enumerate(heads):
            odn_s[rows[c], h * DN_DK:(h + 1) * DN_DK] = ost[lo:hi]
            sdn[h] = jnp.exp(g_last[c][h]) * sdn[h] + _mm_tn(kd[c][h], env["u"][h])

    def lru_fill():
        lru = proj(512, 1024)
        lrubuf[8:8 + tc, :] = lru[:, :LRU_W]
        lw = lcw_ref[...]
        xl = lrubuf[pl.ds(5, tc), :] * lw[0:1, :]
        for j in range(1, CONV_W):
            xl = xl + lrubuf[pl.ds(5 + j, tc), :] * lw[j:j + 1, :]
        lrubuf[0:8, :] = lrubuf[tc:tc + 8, :]
        env["xl"] = xl + lcb_ref[...]
        env["lru_y"] = lru[:, LRU_W:]

    def lru_scan_fill():
        a, b = _lru_coeffs(env["xl"], wgate_ref[...], bgate_ref[...], lam_ref[...])
        row = _iota((tc, LRU_W), 0)
        b = b + jnp.where(row == 0, a * hst[...], 0.0)
        s = 1
        while s < tc:
            keep = row >= s
            b = a * jnp.where(keep, pltpu.roll(b, s, 0), 0.0) + b
            a = a * jnp.where(keep, pltpu.roll(a, s, 0), 1.0)
            s *= 2
        hst[...] = b[tc - 1:tc, :]
        omix_s[:, 512:768] = (b * jax.nn.gelu(env["lru_y"])).astype(BF16)
        env["ret_q"] = proj(1024, 1280)

    def ret_kv_fill():
        kv = proj(1280, 1792)
        cos = cos_ref[...]
        sin = sin_ref[...]
        env["rq"] = _rope(env["ret_q"], cos, sin)
        env["rk"] = _rope(kv[:, :RET_W], cos, sin) * (RET_DK ** -0.5)
        env["rv"] = kv[:, RET_W:]

    def ret_scores_fill():
        qc = [env["rq"][rs] for rs in rows]
        kc = [env["rk"][rs] for rs in rows]
        qp = [jnp.where(head_sel, jnp.concatenate([q] * RET_HEADS, axis=0), 0.0) for q in qc]
        env["qc"] = qc
        env["sc"] = [(_mm_nt(qp[c], kc[c]) * ret_dec).astype(BF16) for c in chunks]

    def ret_intra_fill():
        vc = [env["rv"][rs].astype(BF16) for rs in rows]
        o_intra = []
        for c in chunks:
            op = jnp.where(head_sel, jnp.dot(env["sc"][c], vc[c], preferred_element_type=F32), 0.0)
            acc = op[0:ck]
            for h in range(1, RET_HEADS):
                acc = acc + op[h * ck:(h + 1) * ck]
            o_intra.append(acc)
        env["o_intra"] = o_intra
        env["s_upd"] = [jnp.where(bd, _mm_tn(env["rk"][rows[c]] * ret_kdec, vc[c]), 0.0) for c in chunks]

    def z_fill():
        env["dn_z"] = proj(0, 512)

    def g_fill():
        env["ret_g"] = proj(1792, 2048)

    def ret_post():
        for c in chunks:
            s0 = sret[...]
            oret_s[rows[c], :] = ret_eg * _mm(env["qc"][c], s0) + env["o_intra"][c]
            sret[...] = ret_gc * s0 + env["s_upd"][c]

    fillers = [lru_fill, lru_scan_fill, ret_kv_fill, ret_scores_fill, ret_intra_fill, z_fill, g_fill, ret_post]
    for c in chunks:
        dn_post_a(c)
        if 2 * c < len(fillers):
            fillers[2 * c]()
        dn_post_b(c)
        if 2 * c + 1 < len(fillers):
            fillers[2 * c + 1]()
    for f in fillers[2 * nc:]:
        f()

    nw = dnnw_ref[...]
    dn_z = env["dn_z"]
    for h in range(DN_HEADS):
        oh = odn_s[:, h * DN_DK:(h + 1) * DN_DK]
        oh = oh * lax.rsqrt(jnp.mean(oh * oh, axis=-1, keepdims=True) + EPS) * nw
        omix_s[:, h * DN_DK:(h + 1) * DN_DK] = (oh * _silu(dn_z[:, h * DN_DK:(h + 1) * DN_DK])).astype(BF16)
    omix_s[:, 768:1024] = _ret_norm_gate(oret_s[...], env["ret_g"], _head_avg_matrix()).astype(BF16)
    y_ref[...] = x + jnp.dot(omix_s[...], wout_ref[...], preferred_element_type=F32)

    @pl.when(t == pl.num_programs(1) - 1)
    def _():
        sdn_o_ref[0] = sdn[...]
        dnt_o_ref[0] = qkvbuf[0:8, :]
        h_o_ref[0] = hst[...]
        lrt_o_ref[0] = lrubuf[0:8, :]
        s_bd = sret[...]
        sretbd_o_ref[0] = s_bd
        for h in range(RET_HEADS):
            sret_o_ref[0, h] = s_bd[h * RET_DK:(h + 1) * RET_DK, h * RET_DK:(h + 1) * RET_DK]


def _mix_seq(x, lw, cos, sin, init, *, batch, seq, tc, ck):
    nt = seq // tc
    nc = tc // ck
    hc = DN_HEADS * ck
    const = lambda shape: pl.BlockSpec(shape, lambda b, t: (0,) * len(shape))
    in_specs = [
        pl.BlockSpec((tc, D_MODEL), lambda b, t: (b * nt + t, 0)),
        const((1, D_MODEL)), const((D_MODEL, 3 * DN_W)), const((D_MODEL, 256)), const((8, D_MODEL)),
        const((D_MODEL, 2048)), const((CONV_W, 3 * DN_W)),
        const((1, 128)), const((1, 128)), const((8, 1)), const((8, 1)), const((1, DN_DK)),
        const((CONV_W, LRU_W)), const((1, LRU_W)), const((LRU_W, 2 * LRU_W)), const((1, 2 * LRU_W)),
        const((1, LRU_W)),
        pl.BlockSpec((tc, RET_W), lambda b, t: (t, 0)), pl.BlockSpec((tc, RET_W), lambda b, t: (t, 0)),
        const((D_MODEL, D_MODEL)),
        const((DN_HEADS, DN_DK, DN_DK)), const((8, 3 * DN_W)), const((1, LRU_W)), const((8, LRU_W)),
        const((RET_W, RET_W)),
    ]
    out_shape = [
        jax.ShapeDtypeStruct((batch * seq, D_MODEL), F32),
        jax.ShapeDtypeStruct((batch, DN_HEADS, DN_DK, DN_DK), F32),
        jax.ShapeDtypeStruct((batch, 8, 3 * DN_W), F32),
        jax.ShapeDtypeStruct((batch, 1, LRU_W), F32),
        jax.ShapeDtypeStruct((batch, 8, LRU_W), F32),
        jax.ShapeDtypeStruct((batch, RET_W, RET_W), F32),
        jax.ShapeDtypeStruct((batch, RET_HEADS, RET_DK, RET_DK), F32),
    ]
    out_specs = [
        pl.BlockSpec((tc, D_MODEL), lambda b, t: (b * nt + t, 0)),
        pl.BlockSpec((1, DN_HEADS, DN_DK, DN_DK), lambda b, t: (b, 0, 0, 0)),
        pl.BlockSpec((1, 8, 3 * DN_W), lambda b, t: (b, 0, 0)),
        pl.BlockSpec((1, 1, LRU_W), lambda b, t: (b, 0, 0)),
        pl.BlockSpec((1, 8, LRU_W), lambda b, t: (b, 0, 0)),
        pl.BlockSpec((1, RET_W, RET_W), lambda b, t: (b, 0, 0)),
        pl.BlockSpec((1, RET_HEADS, RET_DK, RET_DK), lambda b, t: (b, 0, 0, 0)),
    ]
    scratch = [
        pltpu.VMEM((DN_HEADS, DN_DK, DN_DK), F32),
        pltpu.VMEM((RET_W, RET_W), F32),
        pltpu.VMEM((1, LRU_W), F32),
        pltpu.VMEM((tc + 8, 3 * DN_W), F32),
        pltpu.VMEM((tc + 8, LRU_W), F32),
        pltpu.VMEM((DN_HEADS, tc, DN_DK), F32),
        pltpu.VMEM((DN_HEADS, tc, DN_DK), F32),
        pltpu.VMEM((DN_HEADS, tc, DN_DK), F32),
        pltpu.VMEM((tc, 128), F32),
        pltpu.VMEM((tc, 128), F32),
        pltpu.VMEM((nc, 1, hc), F32),
        pltpu.VMEM((tc, DN_W), F32),
        pltpu.VMEM((tc, RET_W), F32),
        pltpu.VMEM((tc, D_MODEL), BF16),
    ]
    args = [x, lw["norm_mix"], lw["wqkv"], lw["wab"], lw["wabt"], lw["wrest"], lw["dn_conv_w"],
            lw["alog_r"], lw["dtb_r"], lw["alog_c"], lw["dtb_c"], lw["dn_norm_w"],
            lw["lru_conv_w"], lw["lru_conv_b"], lw["wgate"], lw["bgate"], lw["lam"], cos, sin,
            lw["w_out"], *init]
    return pl.pallas_call(
        functools.partial(_mix_seq_body, tc=tc, ck=ck),
        grid=(batch, nt),
        in_specs=in_specs,
        out_specs=out_specs,
        out_shape=out_shape,
        scratch_shapes=scratch,
        compiler_params=pltpu.CompilerParams(
            dimension_semantics=("arbitrary", "arbitrary"), vmem_limit_bytes=VMEM_LIMIT_BYTES),
        name=f"mix_seq_c{ck}",
    )(*args)


def _mix_step_body(x_ref, nw_ref, wqkv_ref, wab_ref, wrest_ref, dncw_ref, alog_r_ref, dtb_r_ref,
                   dnnw_ref, lcw_ref, lcb_ref, wgate_ref, bgate_ref, lam_ref, cos_ref, sin_ref,
                   wout_ref, sdn_ref, dnc_ref, h_ref, lrc_ref, sret_ref, sdn_all_ref, sret_all_ref,
                   y_ref, sdn_o_ref, dnc_o_ref, h_o_ref, lrc_o_ref, sret_o_ref,
                   q_s, k_s, v_s, eg_s, beta_s, odn_s, rq_s, rk_s, rv_s, oret_s, *, bb):
    x = x_ref[...]
    u = _rmsnorm(x, nw_ref[...]).astype(BF16)

    qkv_pre = jnp.dot(u, wqkv_ref[...], preferred_element_type=F32)
    cw = dncw_ref[...]
    conv = (dnc_ref[0] * cw[0:1, :] + dnc_ref[1] * cw[1:2, :] + dnc_ref[2] * cw[2:3, :]
            + qkv_pre * cw[3:4, :])
    dnc_o_ref[0] = dnc_ref[1]
    dnc_o_ref[1] = dnc_ref[2]
    dnc_o_ref[2] = qkv_pre
    qkv = _silu(conv)
    for h in range(DN_HEADS):
        sl = slice(h * DN_DK, (h + 1) * DN_DK)
        qh = qkv[:, sl]
        kh = qkv[:, DN_W + h * DN_DK:DN_W + (h + 1) * DN_DK]
        q_s[:, sl] = qh * lax.rsqrt(jnp.sum(qh * qh, axis=-1, keepdims=True) + EPS) * (DN_DK ** -0.5)
        k_s[:, sl] = kh * lax.rsqrt(jnp.sum(kh * kh, axis=-1, keepdims=True) + EPS)
    v_s[...] = qkv[:, 2 * DN_W:]
    ab = jnp.dot(u, wab_ref[...], preferred_element_type=F32)
    eg_s[...] = jnp.exp(-jnp.exp(alog_r_ref[...]) * _softplus(ab[:, :128] + dtb_r_ref[...]))
    beta_s[...] = jax.nn.sigmoid(ab[:, 128:])

    rest = jnp.dot(u, wrest_ref[...], preferred_element_type=F32)
    dn_z = rest[:, 0:512]
    lru_x = rest[:, 512:768]
    lru_y = rest[:, 768:1024]
    ret_g = rest[:, 1792:2048]

    lw = lcw_ref[...]
    xl = (lrc_ref[0] * lw[0:1, :] + lrc_ref[1] * lw[1:2, :] + lrc_ref[2] * lw[2:3, :]
          + lru_x * lw[3:4, :] + lcb_ref[...])
    lrc_o_ref[0] = lrc_ref[1]
    lrc_o_ref[1] = lrc_ref[2]
    lrc_o_ref[2] = lru_x
    a, b = _lru_coeffs(xl, wgate_ref[...], bgate_ref[...], lam_ref[...])
    h_new = a * h_ref[...] + b
    h_o_ref[...] = h_new
    o_lru = h_new * jax.nn.gelu(lru_y)

    cos = cos_ref[...]
    sin = sin_ref[...]
    rq_s[...] = _rope(rest[:, 1024:1280], cos, sin)
    rk_s[...] = _rope(rest[:, 1280:1536], cos, sin) * (RET_DK ** -0.5)
    rv_s[...] = rest[:, 1536:1792]

    def per_tile(i8, carry):
        r0 = pl.multiple_of(i8 * 8, 8)
        eg_t = eg_s[pl.ds(r0, 8), :]
        beta_t = beta_s[pl.ds(r0, 8), :]
        for h in range(DN_HEADS):
            sl = slice(h * DN_DK, (h + 1) * DN_DK)
            kt = k_s[pl.ds(r0, 8), sl].T
            qt = q_s[pl.ds(r0, 8), sl].T
            v_t = v_s[pl.ds(r0, 8), sl]
            rows = []
            for j in range(8):
                kcol = kt[:, j:j + 1]
                qcol = qt[:, j:j + 1]
                s0 = sdn_ref[r0 + j, h]
                eg = eg_t[j:j + 1, h:h + 1]
                ks = jnp.sum(s0 * kcol, axis=0, keepdims=True)
                qs = jnp.sum(s0 * qcol, axis=0, keepdims=True)
                qk = jnp.sum(qcol * kcol, axis=0, keepdims=True)
                uu = beta_t[j:j + 1, h:h + 1] * (v_t[j:j + 1, :] - eg * ks)
                rows.append(eg * qs + qk * uu)
                sdn_o_ref[r0 + j, h] = eg * s0 + kcol * uu
            odn_s[pl.ds(r0, 8), sl] = jnp.concatenate(rows, axis=0)
        kt = rk_s[pl.ds(r0, 8), :].T
        qt = rq_s[pl.ds(r0, 8), :].T
        v_t = rv_s[pl.ds(r0, 8), :]
        rows = []
        for j in range(8):
            pieces = []
            for h in range(RET_HEADS):
                sl = slice(h * RET_DK, (h + 1) * RET_DK)
                kcol = kt[sl, j:j + 1]
                qcol = qt[sl, j:j + 1]
                s0 = sret_ref[r0 + j, h]
                gamma = math.exp(LOG_GAMMA[h])
                qs = jnp.sum(s0 * qcol, axis=0, keepdims=True)
                qk = jnp.sum(qcol * kcol, axis=0, keepdims=True)
                pieces.append(gamma * qs + qk * v_t[j:j + 1, sl])
                sret_o_ref[r0 + j, h] = gamma * s0 + kcol * v_t[j:j + 1, sl]
            rows.append(jnp.concatenate(pieces, axis=1))
        oret_s[pl.ds(r0, 8), :] = jnp.concatenate(rows, axis=0)
        return carry

    lax.fori_loop(0, bb // 8, per_tile, 0)

    nw = dnnw_ref[...]
    parts = []
    for h in range(DN_HEADS):
        sl = slice(h * DN_DK, (h + 1) * DN_DK)
        oh = odn_s[:, sl]
        oh = oh * lax.rsqrt(jnp.mean(oh * oh, axis=-1, keepdims=True) + EPS) * nw
        parts.append((oh * _silu(dn_z[:, sl])).astype(BF16))
    parts.append(o_lru.astype(BF16))
    parts.append(_ret_norm_gate(oret_s[...], ret_g, _head_avg_matrix()).astype(BF16))
    y_ref[...] = x + jnp.dot(jnp.concatenate(parts, axis=1), wout_ref[...], preferred_element_type=F32)


def _mix_step(x, lw, cos, sin, layer, sdn, dnc, hl, lrc, sret, sdn_all, sret_all, *, bb):
    n = x.shape[0]
    depth = sdn.shape[0]
    const = lambda shape: pl.BlockSpec(shape, lambda i: (0,) * len(shape))
    in_specs = [
        pl.BlockSpec((bb, D_MODEL), lambda i: (i, 0)),
        const((1, D_MODEL)), const((D_MODEL, 3 * DN_W)), const((D_MODEL, 256)), const((D_MODEL, 2048)),
        const((CONV_W, 3 * DN_W)), const((1, 128)), const((1, 128)), const((1, DN_DK)),
        const((CONV_W, LRU_W)), const((1, LRU_W)), const((LRU_W, 2 * LRU_W)), const((1, 2 * LRU_W)),
        const((1, LRU_W)), const((1, RET_W)), const((1, RET_W)), const((D_MODEL, D_MODEL)),
        pl.BlockSpec((None, bb, DN_HEADS, DN_DK, DN_DK), lambda i: (layer, i, 0, 0, 0)),
        pl.BlockSpec((CONV_W - 1, bb, 3 * DN_W), lambda i: (0, i, 0)),
        pl.BlockSpec((None, bb, LRU_W), lambda i: (layer, i, 0)),
        pl.BlockSpec((CONV_W - 1, bb, LRU_W), lambda i: (0, i, 0)),
        pl.BlockSpec((None, bb, RET_HEADS, RET_DK, RET_DK), lambda i: (layer, i, 0, 0, 0)),
        pl.BlockSpec(memory_space=pl.ANY),
        pl.BlockSpec(memory_space=pl.ANY),
    ]
    out_shape = [
        jax.ShapeDtypeStruct((n, D_MODEL), F32),
        jax.ShapeDtypeStruct((depth, n, DN_HEADS, DN_DK, DN_DK), F32),
        jax.ShapeDtypeStruct((CONV_W - 1, n, 3 * DN_W), F32),
        jax.ShapeDtypeStruct((n, LRU_W), F32),
        jax.ShapeDtypeStruct((CONV_W - 1, n, LRU_W), F32),
        jax.ShapeDtypeStruct((depth, n, RET_HEADS, RET_DK, RET_DK), F32),
    ]
    out_specs = [
        pl.BlockSpec((bb, D_MODEL), lambda i: (i, 0)),
        pl.BlockSpec((None, bb, DN_HEADS, DN_DK, DN_DK), lambda i: (layer, i, 0, 0, 0)),
        pl.BlockSpec((CONV_W - 1, bb, 3 * DN_W), lambda i: (0, i, 0)),
        pl.BlockSpec((bb, LRU_W), lambda i: (i, 0)),
        pl.BlockSpec((CONV_W - 1, bb, LRU_W), lambda i: (0, i, 0)),
        pl.BlockSpec((None, bb, RET_HEADS, RET_DK, RET_DK), lambda i: (layer, i, 0, 0, 0)),
    ]
    scratch = [
        pltpu.VMEM((bb, DN_W), F32), pltpu.VMEM((bb, DN_W), F32), pltpu.VMEM((bb, DN_W), F32),
        pltpu.VMEM((bb, 128), F32), pltpu.VMEM((bb, 128), F32), pltpu.VMEM((bb, DN_W), F32),
        pltpu.VMEM((bb, RET_W), F32), pltpu.VMEM((bb, RET_W), F32), pltpu.VMEM((bb, RET_W), F32),
        pltpu.VMEM((bb, RET_W), F32),
    ]
    args = [x, lw["norm_mix"], lw["wqkv"], lw["wab"], lw["wrest"], lw["dn_conv_w"], lw["alog_r"],
            lw["dtb_r"], lw["dn_norm_w"], lw["lru_conv_w"], lw["lru_conv_b"], lw["wgate"], lw["bgate"],
            lw["lam"], cos, sin, lw["w_out"], sdn, dnc, hl, lrc, sret, sdn_all, sret_all]
    return pl.pallas_call(
        functools.partial(_mix_step_body, bb=bb),
        grid=(n // bb,),
        in_specs=in_specs,
        out_specs=out_specs,
        out_shape=out_shape,
        scratch_shapes=scratch,
        input_output_aliases={len(args) - 2: 1, len(args) - 1: 5},
        compiler_params=pltpu.CompilerParams(
            dimension_semantics=("arbitrary",), vmem_limit_bytes=VMEM_LIMIT_BYTES),
        name="mix_step",
    )(*args)


def _rope_tables(pos):
    half = RET_DK // 2
    inv = ROPE_BASE ** (-jnp.arange(half, dtype=F32) / half)
    ang = pos.astype(F32)[:, None] * inv[None, :]
    cos = jnp.cos(ang)
    sin = jnp.sin(ang)
    cos_full = jnp.tile(jnp.concatenate([cos, cos], axis=-1), (1, RET_HEADS))
    sin_signed = jnp.tile(jnp.concatenate([-sin, sin], axis=-1), (1, RET_HEADS))
    return cos_full, sin_signed


def _block_diag(w):
    n, d, e = w.shape
    eye = jnp.eye(n, dtype=w.dtype)
    return (eye[:, None, :, None] * w[:, :, None, :]).reshape(n * d, n * e)


def _layer_weights(l, norm_mix, w_in, dn_conv_w, dn_a_log, dn_dt_bias, dn_norm_w, lru_conv_w, lru_conv_b,
                   lru_wa, lru_ba, lru_wx, lru_bx, lru_lambda, w_out):
    wl = w_in[l]
    o_a = 3 * DN_W
    wa = wl[:, o_a:o_a + DN_HEADS]
    wb = wl[:, o_a + DN_HEADS:o_a + 2 * DN_HEADS]
    pad_cols = lambda w: jnp.pad(w, ((0, 0), (0, 128 - DN_HEADS)))
    pad_lane = lambda v: jnp.pad(v, (0, 128 - DN_HEADS)).reshape(1, 128)
    pad_sub = lambda v: jnp.pad(v, (0, 8 - DN_HEADS)).reshape(8, 1)
    return {
        "norm_mix": norm_mix[l].reshape(1, D_MODEL),
        "wqkv": wl[:, :o_a].astype(BF16),
        "wab": jnp.concatenate([pad_cols(wa), pad_cols(wb)], axis=1).astype(BF16),
        "wabt": jnp.pad(wa.T, ((0, 8 - DN_HEADS), (0, 0))).astype(BF16),
        "wrest": wl[:, o_a + 2 * DN_HEADS:].astype(BF16),
        "dn_conv_w": dn_conv_w[l],
        "alog_r": pad_lane(dn_a_log[l]), "dtb_r": pad_lane(dn_dt_bias[l]),
        "alog_c": pad_sub(dn_a_log[l]), "dtb_c": pad_sub(dn_dt_bias[l]),
        "dn_norm_w": dn_norm_w[l].reshape(1, DN_DK),
        "lru_conv_w": lru_conv_w[l], "lru_conv_b": lru_conv_b[l].reshape(1, LRU_W),
        "wgate": jnp.concatenate([_block_diag(lru_wa[l]), _block_diag(lru_wx[l])], axis=1).astype(BF16),
        "bgate": jnp.concatenate([lru_ba[l], lru_bx[l]]).reshape(1, 2 * LRU_W),
        "lam": lru_lambda[l].reshape(1, LRU_W),
        "w_out": w_out[l].astype(BF16),
    }


def kernel(x_prompt, x_sample, state_dn, state_dn_conv, state_lru, state_lru_conv, state_ret, meta_tokens, norm_ffn1, w_ffn1_in, w_ffn1_out, norm_mix, w_in, dn_conv_w, dn_a_log, dn_dt_bias, dn_norm_w, lru_conv_w, lru_conv_b, lru_wa, lru_ba, lru_wx, lru_bx, lru_lambda, w_out, norm_ffn2, w_ffn2_in, w_ffn2_out, norm_final):
    batch, seq, _ = x_prompt.shape
    n_dec = x_sample.shape[0]
    depth = w_in.shape[0]
    f1_in, f1_out = w_ffn1_in.astype(BF16), w_ffn1_out.astype(BF16)
    f2_in, f2_out = w_ffn2_in.astype(BF16), w_ffn2_out.astype(BF16)
    lws = [_layer_weights(l, norm_mix, w_in, dn_conv_w, dn_a_log, dn_dt_bias, dn_norm_w, lru_conv_w,
                          lru_conv_b, lru_wa, lru_ba, lru_wx, lru_bx, lru_lambda, w_out)
           for l in range(depth)]
    cos_m, sin_m = _rope_tables(jnp.arange(N_META))
    cos_p, sin_p = _rope_tables(N_META + jnp.arange(seq))
    cos_s, sin_s = _rope_tables(PAST_LEN + jnp.arange(1))

    n_small = n_dec + N_META
    xs = jnp.concatenate([x_sample[:, 0, :], meta_tokens.astype(F32)], axis=0)
    zero_init = (jnp.zeros((DN_HEADS, DN_DK, DN_DK), F32), jnp.zeros((8, 3 * DN_W), F32),
                 jnp.zeros((1, LRU_W), F32), jnp.zeros((8, LRU_W), F32), jnp.zeros((RET_W, RET_W), F32))
    new_s, meta_state = [], []
    sdn_all = jnp.zeros(state_dn.shape, F32)
    sret_all = jnp.zeros(state_ret.shape, F32)
    for l in range(depth):
        last = l == depth - 1
        xs = _ffn(xs, norm_ffn1[l], f1_in, f1_out, l, tm=n_small, nf=2)
        ys, sdn_all, dnc, hl, lrc, sret_all = _mix_step(
            xs[:n_dec], lws[l], cos_s, sin_s, l, state_dn, jnp.swapaxes(state_dn_conv[l], 0, 1),
            state_lru, jnp.swapaxes(state_lru_conv[l], 0, 1), state_ret, sdn_all, sret_all, bb=16)
        ym, m_sdn, m_dnt, m_h, m_lrt, m_sretbd, _ = _mix_seq(
            xs[n_dec:], lws[l], cos_m, sin_m, zero_init, batch=1, seq=N_META, tc=N_META, ck=N_META)
        new_s.append((jnp.swapaxes(dnc, 0, 1), hl, jnp.swapaxes(lrc, 0, 1)))
        meta_state.append((m_sdn[0], m_dnt[0], m_h[0], m_lrt[0], m_sretbd[0]))
        xs = jnp.concatenate([ys, ym], axis=0)
        xs = _ffn(xs, norm_ffn2[l], f2_in, f2_out, l, norm_final if last else None, tm=n_small, nf=2)
    y_sample = xs[:n_dec].reshape(n_dec, 1, D_MODEL)

    xp = x_prompt.reshape(batch * seq, D_MODEL)
    new_p = []
    for l in range(depth):
        last = l == depth - 1
        xp = _ffn(xp, norm_ffn1[l], f1_in, f1_out, l, tm=512, nf=2)
        xp, sdn, dnt, hl, lrt, _, sret = _mix_seq(
            xp, lws[l], cos_p, sin_p, meta_state[l], batch=batch, seq=seq, tc=256, ck=CHUNK)
        new_p.append((sdn, dnt[:, 5:8], hl[:, 0], lrt[:, 5:8], sret))
        xp = _ffn(xp, norm_ffn2[l], f2_in, f2_out, l, norm_final if last else None, tm=512, nf=2)
    y_prompt = xp.reshape(batch, seq, D_MODEL)

    outs_p = [jnp.stack([s[j] for s in new_p]) for j in range(5)]
    dnc_s, lru_s, lrc_s = [jnp.stack([s[j] for s in new_s]) for j in range(3)]
    return (y_prompt, y_sample, *outs_p, sdn_all, dnc_s, lru_s, lrc_s, sret_all)
```

```python
import functools
import math

import jax
import jax.numpy as jnp
from jax import lax
from jax.experimental import pallas as pl
from jax.experimental.pallas import tpu as pltpu

F32 = jnp.float32
BF16 = jnp.bfloat16

D_MODEL = 1024
N_META = 16
PAST_LEN = 16384
DN_HEADS = 4
DN_DK = 128
DN_W = 512
LRU_W = 256
LRU_BLOCKS = 4
LRU_C = 8.0
RET_HEADS = 4
RET_DK = 64
RET_W = 256
CONV_W = 4
CHUNK = 64
D_FF = 2816
ROPE_BASE = 10000.0
EPS = 1e-6
SUB = 16
LOG_GAMMA = tuple(math.log1p(-2.0 ** (-5.0 - h)) for h in range(RET_HEADS))

VMEM_LIMIT_BYTES = 56 * 1024 * 1024


def _mm(a, b):
    return jnp.dot(a.astype(BF16), b.astype(BF16), preferred_element_type=F32)


def _mm_nt(a, b):
    return lax.dot_general(a.astype(BF16), b.astype(BF16), (((1,), (1,)), ((), ())),
                           preferred_element_type=F32)


def _mm_tn(a, b):
    return lax.dot_general(a.astype(BF16), b.astype(BF16), (((0,), (0,)), ((), ())),
                           preferred_element_type=F32)


def _split3(x):
    hi = x.astype(BF16)
    r = x - hi.astype(F32)
    mid = r.astype(BF16)
    lo = (r - mid.astype(F32)).astype(BF16)
    return hi, mid, lo


def _mm_01_left(m01, x):
    m = m01.astype(BF16)
    return sum(jnp.dot(m, p, preferred_element_type=F32) for p in _split3(x))


def _mm_01_right(x, m01):
    m = m01.astype(BF16)
    return sum(jnp.dot(p, m, preferred_element_type=F32) for p in _split3(x))


def _mm_split(x, m_bf16):
    hi = x.astype(BF16)
    lo = (x - hi.astype(F32)).astype(BF16)
    return (jnp.dot(hi, m_bf16, preferred_element_type=F32)
            + jnp.dot(lo, m_bf16, preferred_element_type=F32))


def _rmsnorm(x, w):
    return x * lax.rsqrt(jnp.mean(x * x, axis=-1, keepdims=True) + EPS) * w


def _silu(x):
    return x * jax.nn.sigmoid(x)


def _softplus(x):
    return jnp.maximum(x, 0.0) + jnp.log1p(jnp.exp(-jnp.abs(x)))


def _iota(shape, dim):
    return lax.broadcasted_iota(jnp.int32, shape, dim)


def _lane_log_gamma(shape, dim, width):
    head = _iota(shape, dim) // width
    out = jnp.full(shape, LOG_GAMMA[0], F32)
    for h in range(1, RET_HEADS):
        out = jnp.where(head == h, LOG_GAMMA[h], out)
    return out


def _rope(x, cos, sin_signed):
    n = x.shape[1]
    half = RET_DK // 2
    first = (_iota(x.shape, 1) % RET_DK) < half
    swapped = jnp.where(first, pltpu.roll(x, n - half, 1), pltpu.roll(x, half, 1))
    return x * cos + swapped * sin_signed


def _ret_norm_gate(o, gate, avg_bf16):
    mu = _mm_split(o, avg_bf16)
    d = o - mu
    var = _mm_split(d * d, avg_bf16)
    return d * lax.rsqrt(var + EPS) * _silu(gate)


def _head_avg_matrix():
    r = _iota((RET_W, RET_W), 0) // RET_DK
    c = _iota((RET_W, RET_W), 1) // RET_DK
    return jnp.where(r == c, 1.0 / RET_DK, 0.0).astype(BF16)


def _lru_coeffs(xl, wgate, bgate, lam):
    gates = _mm(xl, wgate) + bgate
    r = jax.nn.sigmoid(gates[:, :LRU_W])
    i = jax.nn.sigmoid(gates[:, LRU_W:])
    log_a = -LRU_C * r * _softplus(-lam)
    a = jnp.exp(log_a)
    b = jnp.sqrt(jnp.maximum(-jnp.tanh(log_a) * (a * a + 1.0), 0.0)) * (i * xl)
    return a, b


def _ffn_body(*refs, final, nf):
    if final:
        x_ref, nw_ref, wi_ref, wo_ref, fw_ref, o_ref = refs
    else:
        x_ref, nw_ref, wi_ref, wo_ref, o_ref = refs
    tf = D_FF // nf
    x = x_ref[...]
    u = _rmsnorm(x, nw_ref[...]).astype(BF16)
    acc = None
    for j in range(nf):
        gate = jnp.dot(u, wi_ref[:, j * tf:(j + 1) * tf], preferred_element_type=F32)
        up = jnp.dot(u, wi_ref[:, D_FF + j * tf:D_FF + (j + 1) * tf], preferred_element_type=F32)
        h = (_silu(gate) * up).astype(BF16)
        part = jnp.dot(h, wo_ref[j * tf:(j + 1) * tf, :], preferred_element_type=F32)
        acc = part if acc is None else acc + part
    y = x + 0.5 * acc
    if final:
        y = _rmsnorm(y, fw_ref[...])
    o_ref[...] = y


def _ffn(x, norm_w, w_in, w_out, layer, final_w=None, *, tm, nf):
    n = x.shape[0]
    final = final_w is not None
    resident = pl.Buffered(1)
    in_specs = [
        pl.BlockSpec((tm, D_MODEL), lambda i: (i, 0)),
        pl.BlockSpec((1, D_MODEL), lambda i: (0, 0)),
        pl.BlockSpec((None, D_MODEL, 2 * D_FF), lambda i: (layer, 0, 0), pipeline_mode=resident),
        pl.BlockSpec((None, D_FF, D_MODEL), lambda i: (layer, 0, 0), pipeline_mode=resident),
    ]
    args = [x, norm_w.reshape(1, D_MODEL), w_in, w_out]
    if final:
        in_specs.append(pl.BlockSpec((1, D_MODEL), lambda i: (0, 0)))
        args.append(final_w.reshape(1, D_MODEL))
    return pl.pallas_call(
        functools.partial(_ffn_body, final=final, nf=nf),
        grid=(n // tm,),
        in_specs=in_specs,
        out_specs=pl.BlockSpec((tm, D_MODEL), lambda i: (i, 0)),
        out_shape=jax.ShapeDtypeStruct((n, D_MODEL), F32),
        compiler_params=pltpu.CompilerParams(
            dimension_semantics=("arbitrary",), vmem_limit_bytes=VMEM_LIMIT_BYTES),
        name="ffn_final" if final else "ffn",
    )(*args)


def _tri_inv(a, n_sub):
    m = a[0].shape[0]
    r = _iota((m, m), 0)
    c = _iota((m, m), 1)
    eye = jnp.where(r == c, 1.0, 0.0).astype(F32)
    diag = (r // SUB) == (c // SUB)
    x = [jnp.where(diag, ai, 0.0) for ai in a]
    off = [ai - xi for ai, xi in zip(a, x)]
    p = [eye - xi for xi in x]
    for _ in range(3):
        x = [_mm(xi, xi) for xi in x]
        p = [pi + _mm(pi, xi) for pi, xi in zip(p, x)]
    if n_sub == 1:
        return p
    n = [_mm(pi, oi) for pi, oi in zip(p, off)]
    n2 = [_mm(ni, ni) for ni in n]
    rr = [eye - ni + n2i - _mm(ni, n2i) for ni, n2i in zip(n, n2)]
    return [_mm(ri, pi) for ri, pi in zip(rr, p)]


def _mix_seq_body(x_ref, nw_ref, wqkv_ref, wabt_ref, wrest_ref, dncw_ref,
                  alog_c_ref, dtb_c_ref, dnnw_ref,
                  lcw_ref, lcb_ref, wgate_ref, bgate_ref, lam_ref, cos_ref, sin_ref, wout_ref,
                  sdn0_ref, dnt0_ref, h0_ref, lrt0_ref, sret0_ref,
                  y_ref, sdn_o_ref, dnt_o_ref, h_o_ref, lrt_o_ref, sretbd_o_ref, sret_o_ref,
                  sdn, sret, hst, qkvbuf, lrubuf, q_s, k_s, v_s, gb_s, grow_s,
                  odn_s, oret_s, omix_s, *, tc, ck):
    t = pl.program_id(1)
    nc = tc // ck
    hc = DN_HEADS * ck
    n_sub = ck // SUB

    @pl.when(t == 0)
    def _():
        sdn[...] = sdn0_ref[...]
        sret[...] = sret0_ref[...]
        hst[...] = h0_ref[...]
        qkvbuf[0:8, :] = dnt0_ref[...]
        lrubuf[0:8, :] = lrt0_ref[...]

    x = x_ref[...]
    u = _rmsnorm(x, nw_ref[...]).astype(BF16)

    qkvbuf[8:8 + tc, :] = jnp.dot(u, wqkv_ref[...], preferred_element_type=F32)
    cw = dncw_ref[...]
    conv = qkvbuf[pl.ds(5, tc), :] * cw[0:1, :]
    for j in range(1, CONV_W):
        conv = conv + qkvbuf[pl.ds(5 + j, tc), :] * cw[j:j + 1, :]
    qkvbuf[0:8, :] = qkvbuf[tc:tc + 8, :]
    qkv = _silu(conv)
    for h in range(DN_HEADS):
        qh = qkv[:, h * DN_DK:(h + 1) * DN_DK]
        kh = qkv[:, DN_W + h * DN_DK:DN_W + (h + 1) * DN_DK]
        q_s[h] = qh * lax.rsqrt(jnp.sum(qh * qh, axis=-1, keepdims=True) + EPS) * (DN_DK ** -0.5)
        k_s[h] = kh * lax.rsqrt(jnp.sum(kh * kh, axis=-1, keepdims=True) + EPS)
        v_s[h] = qkv[:, 2 * DN_W + h * DN_DK:2 * DN_W + (h + 1) * DN_DK]

    abt = lax.dot_general(wabt_ref[...], u, (((1,), (1,)), ((), ())), preferred_element_type=F32)
    g_row = -jnp.exp(alog_c_ref[...]) * _softplus(abt + dtb_c_ref[...])
    rr = _iota((tc, tc), 0)
    cc = _iota((tc, tc), 1)
    upper = jnp.where(((rr // ck) == (cc // ck)) & (rr <= cc), 1.0, 0.0)
    big_g_row = _mm_01_right(g_row, upper)
    is_g = _iota((8, tc), 0) < DN_HEADS
    gb_s[...] = jnp.where(is_g, big_g_row, jax.nn.sigmoid(abt)).T
    for c in range(nc):
        grow_s[c] = jnp.concatenate(
            [big_g_row[h:h + 1, c * ck:(c + 1) * ck] for h in range(DN_HEADS)], axis=1)

    def proj(lo, hi):
        return jnp.dot(u, wrest_ref[:, lo:hi], preferred_element_type=F32)

    sr = _iota((hc, hc), 0)
    sc = _iota((hc, hc), 1)
    same_head = (sr // ck) == (sc // ck)
    strict = same_head & (sr > sc)
    eye_hc = jnp.where(sr == sc, 1.0, 0.0).astype(F32)
    lg_rows = _lane_log_gamma((hc, ck), 0, ck)
    tpos = _iota((hc, ck), 0) % ck
    ipos = _iota((hc, ck), 1)
    causal = tpos >= ipos
    ret_dec = jnp.where(causal, jnp.exp(jnp.where(causal, (tpos - ipos).astype(F32) * lg_rows, 0.0)), 0.0)
    head_sel = (_iota((hc, RET_W), 0) // ck) == (_iota((hc, RET_W), 1) // RET_DK)
    lg_lane = _lane_log_gamma((ck, RET_W), 1, RET_DK)
    tl = _iota((ck, RET_W), 0).astype(F32)
    ret_eg = jnp.exp((tl + 1.0) * lg_lane)
    ret_kdec = jnp.exp((ck - 1.0 - tl) * lg_lane)
    ret_gc = jnp.exp(ck * _lane_log_gamma((1, RET_W), 1, RET_DK))
    bd = (_iota((RET_W, RET_W), 0) // RET_DK) == (_iota((RET_W, RET_W), 1) // RET_DK)

    chunks = range(nc)
    rows = [slice(c * ck, (c + 1) * ck) for c in chunks]

    kst = [jnp.concatenate([k_s[h, rs, :] for h in range(DN_HEADS)], axis=0) for rs in rows]
    qst = [jnp.concatenate([q_s[h, rs, :] for h in range(DN_HEADS)], axis=0) for rs in rows]
    vst = [jnp.concatenate([v_s[h, rs, :] for h in range(DN_HEADS)], axis=0) for rs in rows]
    gst = [jnp.concatenate([gb_s[rs, h:h + 1] for h in range(DN_HEADS)], axis=0) for rs in rows]
    bst = [jnp.concatenate([gb_s[rs, DN_HEADS + h:DN_HEADS + h + 1] for h in range(DN_HEADS)], axis=0)
           for rs in rows]
    dstrict = []
    for c in chunks:
        diff = gst[c] - grow_s[c]
        dstrict.append(jnp.where(strict, jnp.exp(jnp.where(strict, diff, 0.0)), 0.0))
    eg = [jnp.exp(g) for g in gst]
    amat = [bst[c] * _mm_nt(kst[c], kst[c]) * dstrict[c] for c in chunks]
    pmat = [(_mm_nt(qst[c], kst[c]) * (dstrict[c] + eye_hc)).astype(BF16) for c in chunks]
    tinv = _tri_inv(amat, n_sub)
    sol = [_mm(tinv[c], jnp.concatenate([bst[c] * vst[c], (bst[c] * eg[c]) * kst[c]], axis=1))
           for c in chunks]
    heads = [(h * ck, (h + 1) * ck) for h in range(DN_HEADS)]
    wq = [[jnp.concatenate([sol[c][lo:hi, DN_DK:], eg[c][lo:hi] * qst[c][lo:hi]], axis=0).astype(BF16)
           for lo, hi in heads] for c in chunks]
    g_last = [[gst[c][hi - 1:hi] for lo, hi in heads] for c in chunks]
    kd = [[(kst[c][lo:hi] * jnp.exp(g_last[c][h] - gst[c][lo:hi])).astype(BF16)
           for h, (lo, hi) in enumerate(heads)] for c in chunks]

    env = {}

    def dn_post_a(c):
        ws_qs = [_mm(wq[c][h], sdn[h]) for h in range(DN_HEADS)]
        env["u"] = [sol[c][lo:hi, :DN_DK] - ws_qs[h][:ck] for h, (lo, hi) in enumerate(heads)]
        env["qs"] = jnp.concatenate([w[ck:] for w in ws_qs], axis=0)

    def dn_post_b(c):
        ost = env["qs"] + _mm(pmat[c], jnp.concatenate(env["u"], axis=0))
        for h, (lo, hi) in enumerate(heads):
            odn_s[rows[c], h * DN_DK:(h + 1) * DN_DK] = ost[lo:hi]
            sdn[h] = jnp.exp(g_last[c][h]) * sdn[h] + _mm_tn(kd[c][h], env["u"][h])

    def lru_fill():
        lru = proj(512, 1024)
        lrubuf[8:8 + tc, :] = lru[:, :LRU_W]
        lw = lcw_ref[...]
        xl = lrubuf[pl.ds(5, tc), :] * lw[0:1, :]
        for j in range(1, CONV_W):
            xl = xl + lrubuf[pl.ds(5 + j, tc), :] * lw[j:j + 1, :]
        lrubuf[0:8, :] = lrubuf[tc:tc + 8, :]
        env["xl"] = xl + lcb_ref[...]
        env["lru_y"] = lru[:, LRU_W:]

    def lru_scan_fill():
        a, b = _lru_coeffs(env["xl"], wgate_ref[...], bgate_ref[...], lam_ref[...])
        row = _iota((tc, LRU_W), 0)
        b = b + jnp.where(row == 0, a * hst[...], 0.0)
        s = 1
        while s < tc:
            keep = row >= s
            b = a * jnp.where(keep, pltpu.roll(b, s, 0), 0.0) + b
            a = a * jnp.where(keep, pltpu.roll(a, s, 0), 1.0)
            s *= 2
        hst[...] = b[tc - 1:tc, :]
        omix_s[:, 512:768] = (b * jax.nn.gelu(env["lru_y"])).astype(BF16)
        env["ret_q"] = proj(1024, 1280)

    def ret_kv_fill():
        kv = proj(1280, 1792)
        cos = cos_ref[...]
        sin = sin_ref[...]
        env["rq"] = _rope(env["ret_q"], cos, sin)
        env["rk"] = _rope(kv[:, :RET_W], cos, sin) * (RET_DK ** -0.5)
        env["rv"] = kv[:, RET_W:]

    def ret_scores_fill():
        qc = [env["rq"][rs] for rs in rows]
        kc = [env["rk"][rs] for rs in rows]
        qp = [jnp.where(head_sel, jnp.concatenate([q] * RET_HEADS, axis=0), 0.0) for q in qc]
        env["qc"] = qc
        env["sc"] = [(_mm_nt(qp[c], kc[c]) * ret_dec).astype(BF16) for c in chunks]

    def ret_intra_fill():
        vc = [env["rv"][rs].astype(BF16) for rs in rows]
        o_intra = []
        for c in chunks:
            op = jnp.where(head_sel, jnp.dot(env["sc"][c], vc[c], preferred_element_type=F32), 0.0)
            acc = op[0:ck]
            for h in range(1, RET_HEADS):
                acc = acc + op[h * ck:(h + 1) * ck]
            o_intra.append(acc)
        env["o_intra"] = o_intra
        env["s_upd"] = [jnp.where(bd, _mm_tn(env["rk"][rows[c]] * ret_kdec, vc[c]), 0.0) for c in chunks]

    def z_fill():
        env["dn_z"] = proj(0, 512)

    def g_fill():
        env["ret_g"] = proj(1792, 2048)

    def ret_post():
        for c in chunks:
            s0 = sret[...]
            oret_s[rows[c], :] = ret_eg * _mm(env["qc"][c], s0) + env["o_intra"][c]
            sret[...] = ret_gc * s0 + env["s_upd"][c]

    fillers = [lru_fill, lru_scan_fill, ret_kv_fill, ret_scores_fill, ret_intra_fill, z_fill, g_fill, ret_post]
    for c in chunks:
        dn_post_a(c)
        if 2 * c < len(fillers):
            fillers[2 * c]()
        dn_post_b(c)
        if 2 * c + 1 < len(fillers):
            fillers[2 * c + 1]()
    for f in fillers[2 * nc:]:
        f()

    nw = dnnw_ref[...]
    dn_z = env["dn_z"]
    for h in range(DN_HEADS):
        oh = odn_s[:, h * DN_DK:(h + 1) * DN_DK]
        oh = oh * lax.rsqrt(jnp.mean(oh * oh, axis=-1, keepdims=True) + EPS) * nw
        omix_s[:, h * DN_DK:(h + 1) * DN_DK] = (oh * _silu(dn_z[:, h * DN_DK:(h + 1) * DN_DK])).astype(BF16)
    omix_s[:, 768:1024] = _ret_norm_gate(oret_s[...], env["ret_g"], _head_avg_matrix()).astype(BF16)
    y_ref[...] = x + jnp.dot(omix_s[...], wout_ref[...], preferred_element_type=F32)

    @pl.when(t == pl.num_programs(1) - 1)
    def _():
        sdn_o_ref[0] = sdn[...]
        dnt_o_ref[0] = qkvbuf[0:8, :]
        h_o_ref[0] = hst[...]
        lrt_o_ref[0] = lrubuf[0:8, :]
        s_bd = sret[...]
        sretbd_o_ref[0] = s_bd
        for h in range(RET_HEADS):
            sret_o_ref[0, h] = s_bd[h * RET_DK:(h + 1) * RET_DK, h * RET_DK:(h + 1) * RET_DK]


def _mix_seq(x, lw, cos, sin, init, *, batch, seq, tc, ck):
    nt = seq // tc
    nc = tc // ck
    hc = DN_HEADS * ck
    const = lambda shape: pl.BlockSpec(shape, lambda b, t: (0,) * len(shape))
    in_specs = [
        pl.BlockSpec((tc, D_MODEL), lambda b, t: (b * nt + t, 0)),
        const((1, D_MODEL)), const((D_MODEL, 3 * DN_W)), const((8, D_MODEL)),
        const((D_MODEL, 2048)), const((CONV_W, 3 * DN_W)),
        const((8, 1)), const((8, 1)), const((1, DN_DK)),
        const((CONV_W, LRU_W)), const((1, LRU_W)), const((LRU_W, 2 * LRU_W)), const((1, 2 * LRU_W)),
        const((1, LRU_W)),
        pl.BlockSpec((tc, RET_W), lambda b, t: (t, 0)), pl.BlockSpec((tc, RET_W), lambda b, t: (t, 0)),
        const((D_MODEL, D_MODEL)),
        const((DN_HEADS, DN_DK, DN_DK)), const((8, 3 * DN_W)), const((1, LRU_W)), const((8, LRU_W)),
        const((RET_W, RET_W)),
    ]
    out_shape = [
        jax.ShapeDtypeStruct((batch * seq, D_MODEL), F32),
        jax.ShapeDtypeStruct((batch, DN_HEADS, DN_DK, DN_DK), F32),
        jax.ShapeDtypeStruct((batch, 8, 3 * DN_W), F32),
        jax.ShapeDtypeStruct((batch, 1, LRU_W), F32),
        jax.ShapeDtypeStruct((batch, 8, LRU_W), F32),
        jax.ShapeDtypeStruct((batch, RET_W, RET_W), F32),
        jax.ShapeDtypeStruct((batch, RET_HEADS, RET_DK, RET_DK), F32),
    ]
    out_specs = [
        pl.BlockSpec((tc, D_MODEL), lambda b, t: (b * nt + t, 0)),
        pl.BlockSpec((1, DN_HEADS, DN_DK, DN_DK), lambda b, t: (b, 0, 0, 0)),
        pl.BlockSpec((1, 8, 3 * DN_W), lambda b, t: (b, 0, 0)),
        pl.BlockSpec((1, 1, LRU_W), lambda b, t: (b, 0, 0)),
        pl.BlockSpec((1, 8, LRU_W), lambda b, t: (b, 0, 0)),
        pl.BlockSpec((1, RET_W, RET_W), lambda b, t: (b, 0, 0)),
        pl.BlockSpec((1, RET_HEADS, RET_DK, RET_DK), lambda b, t: (b, 0, 0, 0)),
    ]
    scratch = [
        pltpu.VMEM((DN_HEADS, DN_DK, DN_DK), F32),
        pltpu.VMEM((RET_W, RET_W), F32),
        pltpu.VMEM((1, LRU_W), F32),
        pltpu.VMEM((tc + 8, 3 * DN_W), F32),
        pltpu.VMEM((tc + 8, LRU_W), F32),
        pltpu.VMEM((DN_HEADS, tc, DN_DK), F32),
        pltpu.VMEM((DN_HEADS, tc, DN_DK), F32),
        pltpu.VMEM((DN_HEADS, tc, DN_DK), F32),
        pltpu.VMEM((tc, 8), F32),
        pltpu.VMEM((nc, 1, hc), F32),
        pltpu.VMEM((tc, DN_W), F32),
        pltpu.VMEM((tc, RET_W), F32),
        pltpu.VMEM((tc, D_MODEL), BF16),
    ]
    args = [x, lw["norm_mix"], lw["wqkv"], lw["wabt"], lw["wrest"], lw["dn_conv_w"],
            lw["alog_c"], lw["dtb_c"], lw["dn_norm_w"],
            lw["lru_conv_w"], lw["lru_conv_b"], lw["wgate"], lw["bgate"], lw["lam"], cos, sin,
            lw["w_out"], *init]
    return pl.pallas_call(
        functools.partial(_mix_seq_body, tc=tc, ck=ck),
        grid=(batch, nt),
        in_specs=in_specs,
        out_specs=out_specs,
        out_shape=out_shape,
        scratch_shapes=scratch,
        compiler_params=pltpu.CompilerParams(
            dimension_semantics=("arbitrary", "arbitrary"), vmem_limit_bytes=VMEM_LIMIT_BYTES),
        name=f"mix_seq_c{ck}",
    )(*args)


def _mix_step_body(*refs, bb, layer):
    (x_ref, nw_ref, wqkv_ref, wab_ref, wrest_ref, dncw_ref, alog_r_ref, dtb_r_ref,
     dnnw_ref, lcw_ref, lcb_ref, wgate_ref, bgate_ref, lam_ref, cos_ref, sin_ref,
     wout_ref, sdn_ref, dnc_ref, h_ref, lrc_ref, sret_ref) = refs[:22]
    refs = refs[22:]
    if layer > 0:
        prev_sdn_ref, prev_sret_ref = refs[:2]
        refs = refs[2:]
    (y_ref, sdn_o_ref, dnc_o_ref, h_o_ref, lrc_o_ref, sret_o_ref,
     q_s, k_s, v_s, eg_s, beta_s, odn_s, rq_s, rk_s, rv_s, oret_s) = refs
    if layer > 0:
        sdn_o_ref[0:layer] = prev_sdn_ref[...]
        sret_o_ref[0:layer] = prev_sret_ref[...]
    x = x_ref[...]
    u = _rmsnorm(x, nw_ref[...]).astype(BF16)

    qkv_pre = jnp.dot(u, wqkv_ref[...], preferred_element_type=F32)
    cw = dncw_ref[...]
    conv = (dnc_ref[0] * cw[0:1, :] + dnc_ref[1] * cw[1:2, :] + dnc_ref[2] * cw[2:3, :]
            + qkv_pre * cw[3:4, :])
    dnc_o_ref[0] = dnc_ref[1]
    dnc_o_ref[1] = dnc_ref[2]
    dnc_o_ref[2] = qkv_pre
    qkv = _silu(conv)
    for h in range(DN_HEADS):
        sl = slice(h * DN_DK, (h + 1) * DN_DK)
        qh = qkv[:, sl]
        kh = qkv[:, DN_W + h * DN_DK:DN_W + (h + 1) * DN_DK]
        q_s[:, sl] = qh * lax.rsqrt(jnp.sum(qh * qh, axis=-1, keepdims=True) + EPS) * (DN_DK ** -0.5)
        k_s[:, sl] = kh * lax.rsqrt(jnp.sum(kh * kh, axis=-1, keepdims=True) + EPS)
    v_s[...] = qkv[:, 2 * DN_W:]
    ab = jnp.dot(u, wab_ref[...], preferred_element_type=F32)
    eg_s[...] = jnp.exp(-jnp.exp(alog_r_ref[...]) * _softplus(ab[:, :128] + dtb_r_ref[...]))
    beta_s[...] = jax.nn.sigmoid(ab[:, 128:])

    rest = jnp.dot(u, wrest_ref[...], preferred_element_type=F32)
    dn_z = rest[:, 0:512]
    lru_x = rest[:, 512:768]
    lru_y = rest[:, 768:1024]
    ret_g = rest[:, 1792:2048]

    lw = lcw_ref[...]
    xl = (lrc_ref[0] * lw[0:1, :] + lrc_ref[1] * lw[1:2, :] + lrc_ref[2] * lw[2:3, :]
          + lru_x * lw[3:4, :] + lcb_ref[...])
    lrc_o_ref[0] = lrc_ref[1]
    lrc_o_ref[1] = lrc_ref[2]
    lrc_o_ref[2] = lru_x
    a, b = _lru_coeffs(xl, wgate_ref[...], bgate_ref[...], lam_ref[...])
    h_new = a * h_ref[...] + b
    h_o_ref[...] = h_new
    o_lru = h_new * jax.nn.gelu(lru_y)

    cos = cos_ref[...]
    sin = sin_ref[...]
    rq_s[...] = _rope(rest[:, 1024:1280], cos, sin)
    rk_s[...] = _rope(rest[:, 1280:1536], cos, sin) * (RET_DK ** -0.5)
    rv_s[...] = rest[:, 1536:1792]

    def per_tile(i8, carry):
        r0 = pl.multiple_of(i8 * 8, 8)
        eg_t = eg_s[pl.ds(r0, 8), :]
        beta_t = beta_s[pl.ds(r0, 8), :]
        for h in range(DN_HEADS):
            sl = slice(h * DN_DK, (h + 1) * DN_DK)
            kt = k_s[pl.ds(r0, 8), sl].T
            qt = q_s[pl.ds(r0, 8), sl].T
            v_t = v_s[pl.ds(r0, 8), sl]
            rows = []
            for j in range(8):
                kcol = kt[:, j:j + 1]
                qcol = qt[:, j:j + 1]
                s0 = sdn_ref[r0 + j, h]
                eg = eg_t[j:j + 1, h:h + 1]
                ks = jnp.sum(s0 * kcol, axis=0, keepdims=True)
                qs = jnp.sum(s0 * qcol, axis=0, keepdims=True)
                qk = jnp.sum(qcol * kcol, axis=0, keepdims=True)
                uu = beta_t[j:j + 1, h:h + 1] * (v_t[j:j + 1, :] - eg * ks)
                rows.append(eg * qs + qk * uu)
                sdn_o_ref[layer, r0 + j, h] = eg * s0 + kcol * uu
            odn_s[pl.ds(r0, 8), sl] = jnp.concatenate(rows, axis=0)
        kt = rk_s[pl.ds(r0, 8), :].T
        qt = rq_s[pl.ds(r0, 8), :].T
        v_t = rv_s[pl.ds(r0, 8), :]
        rows = []
        for j in range(8):
            pieces = []
            for h in range(RET_HEADS):
                sl = slice(h * RET_DK, (h + 1) * RET_DK)
                kcol = kt[sl, j:j + 1]
                qcol = qt[sl, j:j + 1]
                s0 = sret_ref[r0 + j, h]
                gamma = math.exp(LOG_GAMMA[h])
                qs = jnp.sum(s0 * qcol, axis=0, keepdims=True)
                qk = jnp.sum(qcol * kcol, axis=0, keepdims=True)
                pieces.append(gamma * qs + qk * v_t[j:j + 1, sl])
                sret_o_ref[layer, r0 + j, h] = gamma * s0 + kcol * v_t[j:j + 1, sl]
            rows.append(jnp.concatenate(pieces, axis=1))
        oret_s[pl.ds(r0, 8), :] = jnp.concatenate(rows, axis=0)
        return carry

    lax.fori_loop(0, bb // 8, per_tile, 0)

    nw = dnnw_ref[...]
    parts = []
    for h in range(DN_HEADS):
        sl = slice(h * DN_DK, (h + 1) * DN_DK)
        oh = odn_s[:, sl]
        oh = oh * lax.rsqrt(jnp.mean(oh * oh, axis=-1, keepdims=True) + EPS) * nw
        parts.append((oh * _silu(dn_z[:, sl])).astype(BF16))
    parts.append(o_lru.astype(BF16))
    parts.append(_ret_norm_gate(oret_s[...], ret_g, _head_avg_matrix()).astype(BF16))
    y_ref[...] = x + jnp.dot(jnp.concatenate(parts, axis=1), wout_ref[...], preferred_element_type=F32)


def _mix_step(x, lw, cos, sin, layer, sdn, dnc, hl, lrc, sret, prev_sdn, prev_sret, *, bb):
    n = x.shape[0]
    const = lambda shape: pl.BlockSpec(shape, lambda i: (0,) * len(shape))
    in_specs = [
        pl.BlockSpec((bb, D_MODEL), lambda i: (i, 0)),
        const((1, D_MODEL)), const((D_MODEL, 3 * DN_W)), const((D_MODEL, 256)), const((D_MODEL, 2048)),
        const((CONV_W, 3 * DN_W)), const((1, 128)), const((1, 128)), const((1, DN_DK)),
        const((CONV_W, LRU_W)), const((1, LRU_W)), const((LRU_W, 2 * LRU_W)), const((1, 2 * LRU_W)),
        const((1, LRU_W)), const((1, RET_W)), const((1, RET_W)), const((D_MODEL, D_MODEL)),
        pl.BlockSpec((None, bb, DN_HEADS, DN_DK, DN_DK), lambda i: (layer, i, 0, 0, 0)),
        pl.BlockSpec((CONV_W - 1, bb, 3 * DN_W), lambda i: (0, i, 0)),
        pl.BlockSpec((None, bb, LRU_W), lambda i: (layer, i, 0)),
        pl.BlockSpec((CONV_W - 1, bb, LRU_W), lambda i: (0, i, 0)),
        pl.BlockSpec((None, bb, RET_HEADS, RET_DK, RET_DK), lambda i: (layer, i, 0, 0, 0)),
    ]
    if layer > 0:
        in_specs += [
            pl.BlockSpec((layer, bb, DN_HEADS, DN_DK, DN_DK), lambda i: (0, i, 0, 0, 0)),
            pl.BlockSpec((layer, bb, RET_HEADS, RET_DK, RET_DK), lambda i: (0, i, 0, 0, 0)),
        ]
    out_shape = [
        jax.ShapeDtypeStruct((n, D_MODEL), F32),
        jax.ShapeDtypeStruct((layer + 1, n, DN_HEADS, DN_DK, DN_DK), F32),
        jax.ShapeDtypeStruct((CONV_W - 1, n, 3 * DN_W), F32),
        jax.ShapeDtypeStruct((n, LRU_W), F32),
        jax.ShapeDtypeStruct((CONV_W - 1, n, LRU_W), F32),
        jax.ShapeDtypeStruct((layer + 1, n, RET_HEADS, RET_DK, RET_DK), F32),
    ]
    out_specs = [
        pl.BlockSpec((bb, D_MODEL), lambda i: (i, 0)),
        pl.BlockSpec((layer + 1, bb, DN_HEADS, DN_DK, DN_DK), lambda i: (0, i, 0, 0, 0)),
        pl.BlockSpec((CONV_W - 1, bb, 3 * DN_W), lambda i: (0, i, 0)),
        pl.BlockSpec((bb, LRU_W), lambda i: (i, 0)),
        pl.BlockSpec((CONV_W - 1, bb, LRU_W), lambda i: (0, i, 0)),
        pl.BlockSpec((layer + 1, bb, RET_HEADS, RET_DK, RET_DK), lambda i: (0, i, 0, 0, 0)),
    ]
    scratch = [
        pltpu.VMEM((bb, DN_W), F32), pltpu.VMEM((bb, DN_W), F32), pltpu.VMEM((bb, DN_W), F32),
        pltpu.VMEM((bb, 128), F32), pltpu.VMEM((bb, 128), F32), pltpu.VMEM((bb, DN_W), F32),
        pltpu.VMEM((bb, RET_W), F32), pltpu.VMEM((bb, RET_W), F32), pltpu.VMEM((bb, RET_W), F32),
        pltpu.VMEM((bb, RET_W), F32),
    ]
    args = [x, lw["norm_mix"], lw["wqkv"], lw["wab"], lw["wrest"], lw["dn_conv_w"], lw["alog_r"],
            lw["dtb_r"], lw["dn_norm_w"], lw["lru_conv_w"], lw["lru_conv_b"], lw["wgate"], lw["bgate"],
            lw["lam"], cos, sin, lw["w_out"], sdn, dnc, hl, lrc, sret]
    if layer > 0:
        args += [prev_sdn, prev_sret]
    return pl.pallas_call(
        functools.partial(_mix_step_body, bb=bb, layer=layer),
        grid=(n // bb,),
        in_specs=in_specs,
        out_specs=out_specs,
        out_shape=out_shape,
        scratch_shapes=scratch,
        compiler_params=pltpu.CompilerParams(
            dimension_semantics=("arbitrary",), vmem_limit_bytes=VMEM_LIMIT_BYTES),
        name="mix_step",
    )(*args)


def _rope_tables(pos):
    half = RET_DK // 2
    inv = ROPE_BASE ** (-jnp.arange(half, dtype=F32) / half)
    ang = pos.astype(F32)[:, None] * inv[None, :]
    cos = jnp.cos(ang)
    sin = jnp.sin(ang)
    cos_full = jnp.tile(jnp.concatenate([cos, cos], axis=-1), (1, RET_HEADS))
    sin_signed = jnp.tile(jnp.concatenate([-sin, sin], axis=-1), (1, RET_HEADS))
    return cos_full, sin_signed


def _block_diag(w):
    n, d, e = w.shape
    eye = jnp.eye(n, dtype=w.dtype)
    return (eye[:, None, :, None] * w[:, :, None, :]).reshape(n * d, n * e)


def _layer_weights(l, norm_mix, w_in, dn_conv_w, dn_a_log, dn_dt_bias, dn_norm_w, lru_conv_w, lru_conv_b,
                   lru_wa, lru_ba, lru_wx, lru_bx, lru_lambda, w_out):
    wl = w_in[l]
    o_a = 3 * DN_W
    wa = wl[:, o_a:o_a + DN_HEADS]
    wb = wl[:, o_a + DN_HEADS:o_a + 2 * DN_HEADS]
    pad_cols = lambda w: jnp.pad(w, ((0, 0), (0, 128 - DN_HEADS)))
    pad_lane = lambda v: jnp.pad(v, (0, 128 - DN_HEADS)).reshape(1, 128)
    pad_sub = lambda v: jnp.pad(v, (0, 8 - DN_HEADS)).reshape(8, 1)
    return {
        "norm_mix": norm_mix[l].reshape(1, D_MODEL),
        "wqkv": wl[:, :o_a].astype(BF16),
        "wab": jnp.concatenate([pad_cols(wa), pad_cols(wb)], axis=1).astype(BF16),
        "wabt": jnp.concatenate([wa, wb], axis=1).T.astype(BF16),
        "wrest": wl[:, o_a + 2 * DN_HEADS:].astype(BF16),
        "dn_conv_w": dn_conv_w[l],
        "alog_r": pad_lane(dn_a_log[l]), "dtb_r": pad_lane(dn_dt_bias[l]),
        "alog_c": pad_sub(dn_a_log[l]), "dtb_c": pad_sub(dn_dt_bias[l]),
        "dn_norm_w": dn_norm_w[l].reshape(1, DN_DK),
        "lru_conv_w": lru_conv_w[l], "lru_conv_b": lru_conv_b[l].reshape(1, LRU_W),
        "wgate": jnp.concatenate([_block_diag(lru_wa[l]), _block_diag(lru_wx[l])], axis=1).astype(BF16),
        "bgate": jnp.concatenate([lru_ba[l], lru_bx[l]]).reshape(1, 2 * LRU_W),
        "lam": lru_lambda[l].reshape(1, LRU_W),
        "w_out": w_out[l].astype(BF16),
    }


def kernel(x_prompt, x_sample, state_dn, state_dn_conv, state_lru, state_lru_conv, state_ret, meta_tokens, norm_ffn1, w_ffn1_in, w_ffn1_out, norm_mix, w_in, dn_conv_w, dn_a_log, dn_dt_bias, dn_norm_w, lru_conv_w, lru_conv_b, lru_wa, lru_ba, lru_wx, lru_bx, lru_lambda, w_out, norm_ffn2, w_ffn2_in, w_ffn2_out, norm_final):
    batch, seq, _ = x_prompt.shape
    n_dec = x_sample.shape[0]
    depth = w_in.shape[0]
    f1_in, f1_out = w_ffn1_in.astype(BF16), w_ffn1_out.astype(BF16)
    f2_in, f2_out = w_ffn2_in.astype(BF16), w_ffn2_out.astype(BF16)
    lws = [_layer_weights(l, norm_mix, w_in, dn_conv_w, dn_a_log, dn_dt_bias, dn_norm_w, lru_conv_w,
                          lru_conv_b, lru_wa, lru_ba, lru_wx, lru_bx, lru_lambda, w_out)
           for l in range(depth)]
    cos_m, sin_m = _rope_tables(jnp.arange(N_META))
    cos_p, sin_p = _rope_tables(N_META + jnp.arange(seq))
    cos_s, sin_s = _rope_tables(PAST_LEN + jnp.arange(1))

    n_small = n_dec + N_META
    xs = jnp.concatenate([x_sample[:, 0, :], meta_tokens.astype(F32)], axis=0)
    zero_init = (jnp.zeros((DN_HEADS, DN_DK, DN_DK), F32), jnp.zeros((8, 3 * DN_W), F32),
                 jnp.zeros((1, LRU_W), F32), jnp.zeros((8, LRU_W), F32), jnp.zeros((RET_W, RET_W), F32))
    new_s, meta_state = [], []
    sdn_all = sret_all = None
    for l in range(depth):
        last = l == depth - 1
        xs = _ffn(xs, norm_ffn1[l], f1_in, f1_out, l, tm=n_small, nf=1)
        ys, sdn_all, dnc, hl, lrc, sret_all = _mix_step(
            xs[:n_dec], lws[l], cos_s, sin_s, l, state_dn, jnp.swapaxes(state_dn_conv[l], 0, 1),
            state_lru, jnp.swapaxes(state_lru_conv[l], 0, 1), state_ret, sdn_all, sret_all, bb=8)
        ym, m_sdn, m_dnt, m_h, m_lrt, m_sretbd, _ = _mix_seq(
            xs[n_dec:], lws[l], cos_m, sin_m, zero_init, batch=1, seq=N_META, tc=N_META, ck=N_META)
        new_s.append((jnp.swapaxes(dnc, 0, 1), hl, jnp.swapaxes(lrc, 0, 1)))
        meta_state.append((m_sdn[0], m_dnt[0], m_h[0], m_lrt[0], m_sretbd[0]))
        xs = jnp.concatenate([ys, ym], axis=0)
        xs = _ffn(xs, norm_ffn2[l], f2_in, f2_out, l, norm_final if last else None, tm=n_small, nf=1)
    y_sample = xs[:n_dec].reshape(n_dec, 1, D_MODEL)

    xp = x_prompt.reshape(batch * seq, D_MODEL)
    new_p = []
    for l in range(depth):
        last = l == depth - 1
        xp = _ffn(xp, norm_ffn1[l], f1_in, f1_out, l, tm=512, nf=1)
        xp, sdn, dnt, hl, lrt, _, sret = _mix_seq(
            xp, lws[l], cos_p, sin_p, meta_state[l], batch=batch, seq=seq, tc=256, ck=CHUNK)
        new_p.append((sdn, dnt[:, 5:8], hl[:, 0], lrt[:, 5:8], sret))
        xp = _ffn(xp, norm_ffn2[l], f2_in, f2_out, l, norm_final if last else None, tm=512, nf=1)
    y_prompt = xp.reshape(batch, seq, D_MODEL)

    outs_p = [jnp.stack([s[j] for s in new_p]) for j in range(5)]
    dnc_s, lru_s, lrc_s = [jnp.stack([s[j] for s in new_s]) for j in range(3)]
    return (y_prompt, y_sample, *outs_p, sdn_all, dnc_s, lru_s, lrc_s, sret_all)
```

```python
import functools
import math

import jax
import jax.numpy as jnp
from jax import lax
from jax.experimental import pallas as pl
from jax.experimental.pallas import tpu as pltpu

F32 = jnp.float32
BF16 = jnp.bfloat16

D_MODEL = 1024
N_META = 16
PAST_LEN = 16384
DN_HEADS = 4
DN_DK = 128
DN_W = 512
LRU_W = 256
LRU_BLOCKS = 4
LRU_C = 8.0
RET_HEADS = 4
RET_DK = 64
RET_W = 256
CONV_W = 4
CHUNK = 64
D_FF = 2816
ROPE_BASE = 10000.0
EPS = 1e-6
SUB = 16
LOG_GAMMA = tuple(math.log1p(-2.0 ** (-5.0 - h)) for h in range(RET_HEADS))

VMEM_LIMIT_BYTES = 56 * 1024 * 1024


def _mm(a, b):
    return jnp.dot(a.astype(BF16), b.astype(BF16), preferred_element_type=F32)


def _mm_nt(a, b):
    return lax.dot_general(a.astype(BF16), b.astype(BF16), (((1,), (1,)), ((), ())),
                           preferred_element_type=F32)


def _mm_tn(a, b):
    return lax.dot_general(a.astype(BF16), b.astype(BF16), (((0,), (0,)), ((), ())),
                           preferred_element_type=F32)


def _split3(x):
    hi = x.astype(BF16)
    r = x - hi.astype(F32)
    mid = r.astype(BF16)
    lo = (r - mid.astype(F32)).astype(BF16)
    return hi, mid, lo


def _mm_01_left(m01, x):
    m = m01.astype(BF16)
    return sum(jnp.dot(m, p, preferred_element_type=F32) for p in _split3(x))


def _mm_01_right(x, m01):
    m = m01.astype(BF16)
    return sum(jnp.dot(p, m, preferred_element_type=F32) for p in _split3(x))


def _mm_split(x, m_bf16):
    hi = x.astype(BF16)
    lo = (x - hi.astype(F32)).astype(BF16)
    return (jnp.dot(hi, m_bf16, preferred_element_type=F32)
            + jnp.dot(lo, m_bf16, preferred_element_type=F32))


def _rmsnorm(x, w):
    return x * lax.rsqrt(jnp.mean(x * x, axis=-1, keepdims=True) + EPS) * w


def _silu(x):
    return x * jax.nn.sigmoid(x)


def _softplus(x):
    return jnp.maximum(x, 0.0) + jnp.log1p(jnp.exp(-jnp.abs(x)))


def _iota(shape, dim):
    return lax.broadcasted_iota(jnp.int32, shape, dim)


def _lane_log_gamma(shape, dim, width):
    head = _iota(shape, dim) // width
    out = jnp.full(shape, LOG_GAMMA[0], F32)
    for h in range(1, RET_HEADS):
        out = jnp.where(head == h, LOG_GAMMA[h], out)
    return out


def _rope(x, cos, sin_signed):
    n = x.shape[1]
    half = RET_DK // 2
    first = (_iota(x.shape, 1) % RET_DK) < half
    swapped = jnp.where(first, pltpu.roll(x, n - half, 1), pltpu.roll(x, half, 1))
    return x * cos + swapped * sin_signed


def _ret_norm_gate(o, gate, avg_bf16):
    mu = _mm_split(o, avg_bf16)
    d = o - mu
    var = _mm_split(d * d, avg_bf16)
    return d * lax.rsqrt(var + EPS) * _silu(gate)


def _head_avg_matrix():
    r = _iota((RET_W, RET_W), 0) // RET_DK
    c = _iota((RET_W, RET_W), 1) // RET_DK
    return jnp.where(r == c, 1.0 / RET_DK, 0.0).astype(BF16)


def _lru_coeffs(xl, wgate, bgate, lam):
    gates = _mm(xl, wgate) + bgate
    r = jax.nn.sigmoid(gates[:, :LRU_W])
    i = jax.nn.sigmoid(gates[:, LRU_W:])
    log_a = -LRU_C * r * _softplus(-lam)
    a = jnp.exp(log_a)
    b = jnp.sqrt(jnp.maximum(-jnp.tanh(log_a) * (a * a + 1.0), 0.0)) * (i * xl)
    return a, b


def _ffn_body(*refs, final, nf):
    if final:
        x_ref, nw_ref, wi_ref, wo_ref, fw_ref, o_ref = refs
    else:
        x_ref, nw_ref, wi_ref, wo_ref, o_ref = refs
    tf = D_FF // nf
    x = x_ref[...]
    u = _rmsnorm(x, nw_ref[...]).astype(BF16)
    acc = None
    for j in range(nf):
        gate = jnp.dot(u, wi_ref[:, j * tf:(j + 1) * tf], preferred_element_type=F32)
        up = jnp.dot(u, wi_ref[:, D_FF + j * tf:D_FF + (j + 1) * tf], preferred_element_type=F32)
        h = (_silu(gate) * up).astype(BF16)
        part = jnp.dot(h, wo_ref[j * tf:(j + 1) * tf, :], preferred_element_type=F32)
        acc = part if acc is None else acc + part
    y = x + 0.5 * acc
    if final:
        y = _rmsnorm(y, fw_ref[...])
    o_ref[...] = y


def _ffn(x, norm_w, w_in, w_out, layer, final_w=None, *, tm, nf):
    n = x.shape[0]
    final = final_w is not None
    resident = pl.Buffered(1)
    in_specs = [
        pl.BlockSpec((tm, D_MODEL), lambda i: (i, 0)),
        pl.BlockSpec((1, D_MODEL), lambda i: (0, 0)),
        pl.BlockSpec((None, D_MODEL, 2 * D_FF), lambda i: (layer, 0, 0), pipeline_mode=resident),
        pl.BlockSpec((None, D_FF, D_MODEL), lambda i: (layer, 0, 0), pipeline_mode=resident),
    ]
    args = [x, norm_w.reshape(1, D_MODEL), w_in, w_out]
    if final:
        in_specs.append(pl.BlockSpec((1, D_MODEL), lambda i: (0, 0)))
        args.append(final_w.reshape(1, D_MODEL))
    return pl.pallas_call(
        functools.partial(_ffn_body, final=final, nf=nf),
        grid=(n // tm,),
        in_specs=in_specs,
        out_specs=pl.BlockSpec((tm, D_MODEL), lambda i: (i, 0)),
        out_shape=jax.ShapeDtypeStruct((n, D_MODEL), F32),
        compiler_params=pltpu.CompilerParams(
            dimension_semantics=("arbitrary",), vmem_limit_bytes=VMEM_LIMIT_BYTES),
        name="ffn_final" if final else "ffn",
    )(*args)


def _tri_inv(a, n_sub):
    m = a[0].shape[0]
    r = _iota((m, m), 0)
    c = _iota((m, m), 1)
    eye = jnp.where(r == c, 1.0, 0.0).astype(F32)
    diag = (r // SUB) == (c // SUB)
    x = [jnp.where(diag, ai, 0.0) for ai in a]
    off = [ai - xi for ai, xi in zip(a, x)]
    p = [eye - xi for xi in x]
    for _ in range(3):
        x = [_mm(xi, xi) for xi in x]
        p = [pi + _mm(pi, xi) for pi, xi in zip(p, x)]
    if n_sub == 1:
        return p
    n = [_mm(pi, oi) for pi, oi in zip(p, off)]
    n2 = [_mm(ni, ni) for ni in n]
    rr = [eye - ni + n2i - _mm(ni, n2i) for ni, n2i in zip(n, n2)]
    return [_mm(ri, pi) for ri, pi in zip(rr, p)]


def _mix_seq_body(x_ref, nw_ref, wqkv_ref, wabt_ref, wrest_ref, dncw_ref,
                  alog_c_ref, dtb_c_ref, dnnw_ref,
                  lcw_ref, lcb_ref, wgate_ref, bgate_ref, lam_ref, cos_ref, sin_ref, wout_ref,
                  sdn0_ref, dnt0_ref, h0_ref, lrt0_ref, sret0_ref,
                  y_ref, sdn_o_ref, dnt_o_ref, h_o_ref, lrt_o_ref, sretbd_o_ref, sret_o_ref,
                  sdn, sret, hst, qkvbuf, lrubuf, q_s, k_s, v_s, gb_s, grow_s,
                  odn_s, oret_s, omix_s, *, tc, ck):
    t = pl.program_id(1)
    nc = tc // ck
    hc = DN_HEADS * ck
    n_sub = ck // SUB

    @pl.when(t == 0)
    def _():
        sdn[...] = sdn0_ref[...]
        sret[...] = sret0_ref[...]
        hst[...] = h0_ref[...]
        qkvbuf[0:8, :] = dnt0_ref[...]
        lrubuf[0:8, :] = lrt0_ref[...]

    x = x_ref[...]
    u = _rmsnorm(x, nw_ref[...]).astype(BF16)

    qkvbuf[8:8 + tc, :] = jnp.dot(u, wqkv_ref[...], preferred_element_type=F32)
    cw = dncw_ref[...]
    conv = qkvbuf[pl.ds(5, tc), :] * cw[0:1, :]
    for j in range(1, CONV_W):
        conv = conv + qkvbuf[pl.ds(5 + j, tc), :] * cw[j:j + 1, :]
    qkvbuf[0:8, :] = qkvbuf[tc:tc + 8, :]
    qkv = _silu(conv)
    for h in range(DN_HEADS):
        qh = qkv[:, h * DN_DK:(h + 1) * DN_DK]
        kh = qkv[:, DN_W + h * DN_DK:DN_W + (h + 1) * DN_DK]
        q_s[h] = qh * lax.rsqrt(jnp.sum(qh * qh, axis=-1, keepdims=True) + EPS) * (DN_DK ** -0.5)
        k_s[h] = kh * lax.rsqrt(jnp.sum(kh * kh, axis=-1, keepdims=True) + EPS)
        v_s[h] = qkv[:, 2 * DN_W + h * DN_DK:2 * DN_W + (h + 1) * DN_DK]

    abt = lax.dot_general(wabt_ref[...], u, (((1,), (1,)), ((), ())), preferred_element_type=F32)
    g_row = -jnp.exp(alog_c_ref[...]) * _softplus(abt + dtb_c_ref[...])
    rr = _iota((tc, tc), 0)
    cc = _iota((tc, tc), 1)
    upper = jnp.where(((rr // ck) == (cc // ck)) & (rr <= cc), 1.0, 0.0)
    big_g_row = _mm_01_right(g_row, upper)
    is_g = _iota((8, tc), 0) < DN_HEADS
    gb_s[...] = jnp.where(is_g, big_g_row, jax.nn.sigmoid(abt)).T
    for c in range(nc):
        grow_s[c] = jnp.concatenate(
            [big_g_row[h:h + 1, c * ck:(c + 1) * ck] for h in range(DN_HEADS)], axis=1)

    def proj(lo, hi):
        return jnp.dot(u, wrest_ref[:, lo:hi], preferred_element_type=F32)

    sr = _iota((hc, hc), 0)
    sc = _iota((hc, hc), 1)
    same_head = (sr // ck) == (sc // ck)
    strict = same_head & (sr > sc)
    eye_hc = jnp.where(sr == sc, 1.0, 0.0).astype(F32)
    lg_rows = _lane_log_gamma((hc, ck), 0, ck)
    tpos = _iota((hc, ck), 0) % ck
    ipos = _iota((hc, ck), 1)
    causal = tpos >= ipos
    ret_dec = jnp.where(causal, jnp.exp(jnp.where(causal, (tpos - ipos).astype(F32) * lg_rows, 0.0)), 0.0)
    head_sel = (_iota((hc, RET_W), 0) // ck) == (_iota((hc, RET_W), 1) // RET_DK)
    lg_lane = _lane_log_gamma((ck, RET_W), 1, RET_DK)
    tl = _iota((ck, RET_W), 0).astype(F32)
    ret_eg = jnp.exp((tl + 1.0) * lg_lane)
    ret_kdec = jnp.exp((ck - 1.0 - tl) * lg_lane)
    ret_gc = jnp.exp(ck * _lane_log_gamma((1, RET_W), 1, RET_DK))
    bd = (_iota((RET_W, RET_W), 0) // RET_DK) == (_iota((RET_W, RET_W), 1) // RET_DK)

    chunks = range(nc)
    rows = [slice(c * ck, (c + 1) * ck) for c in chunks]

    kst = [jnp.concatenate([k_s[h, rs, :] for h in range(DN_HEADS)], axis=0) for rs in rows]
    qst = [jnp.concatenate([q_s[h, rs, :] for h in range(DN_HEADS)], axis=0) for rs in rows]
    vst = [jnp.concatenate([v_s[h, rs, :] for h in range(DN_HEADS)], axis=0) for rs in rows]
    gst = [jnp.concatenate([gb_s[rs, h:h + 1] for h in range(DN_HEADS)], axis=0) for rs in rows]
    bst = [jnp.concatenate([gb_s[rs, DN_HEADS + h:DN_HEADS + h + 1] for h in range(DN_HEADS)], axis=0)
           for rs in rows]
    dstrict = []
    for c in chunks:
        diff = gst[c] - grow_s[c]
        dstrict.append(jnp.where(strict, jnp.exp(jnp.where(strict, diff, 0.0)), 0.0))
    eg = [jnp.exp(g) for g in gst]
    amat = [bst[c] * _mm_nt(kst[c], kst[c]) * dstrict[c] for c in chunks]
    pmat = [(_mm_nt(qst[c], kst[c]) * (dstrict[c] + eye_hc)).astype(BF16) for c in chunks]
    tinv = _tri_inv(amat, n_sub)
    sol = [_mm(tinv[c], jnp.concatenate([bst[c] * vst[c], (bst[c] * eg[c]) * kst[c]], axis=1))
           for c in chunks]
    heads = [(h * ck, (h + 1) * ck) for h in range(DN_HEADS)]
    wq = [[jnp.concatenate([sol[c][lo:hi, DN_DK:], eg[c][lo:hi] * qst[c][lo:hi]], axis=0).astype(BF16)
           for lo, hi in heads] for c in chunks]
    g_last = [[gst[c][hi - 1:hi] for lo, hi in heads] for c in chunks]
    kd = [[(kst[c][lo:hi] * jnp.exp(g_last[c][h] - gst[c][lo:hi])).astype(BF16)
           for h, (lo, hi) in enumerate(heads)] for c in chunks]

    env = {}

    def dn_post_a(c):
        ws_qs = [_mm(wq[c][h], sdn[h]) for h in range(DN_HEADS)]
        env["u"] = [sol[c][lo:hi, :DN_DK] - ws_qs[h][:ck] for h, (lo, hi) in enumerate(heads)]
        env["qs"] = jnp.concatenate([w[ck:] for w in ws_qs], axis=0)

    def dn_post_b(c):
        ost = env["qs"] + _mm(pmat[c], jnp.concatenate(env["u"], axis=0))
        for h, (lo, hi) in enumerate(heads):
            odn_s[rows[c], h * DN_DK:(h + 1) * DN_DK] = ost[lo:hi]
            sdn[h] = jnp.exp(g_last[c][h]) * sdn[h] + _mm_tn(kd[c][h], env["u"][h])

    def lru_fill():
        lru = proj(512, 1024)
        lrubuf[8:8 + tc, :] = lru[:, :LRU_W]
        lw = lcw_ref[...]
        xl = lrubuf[pl.ds(5, tc), :] * lw[0:1, :]
        for j in range(1, CONV_W):
            xl = xl + lrubuf[pl.ds(5 + j, tc), :] * lw[j:j + 1, :]
        lrubuf[0:8, :] = lrubuf[tc:tc + 8, :]
        env["xl"] = xl + lcb_ref[...]
        env["lru_y"] = lru[:, LRU_W:]

    def lru_scan_fill():
        a, b = _lru_coeffs(env["xl"], wgate_ref[...], bgate_ref[...], lam_ref[...])
        row = _iota((tc, LRU_W), 0)
        b = b + jnp.where(row == 0, a * hst[...], 0.0)
        s = 1
        while s < tc:
            keep = row >= s
            b = a * jnp.where(keep, pltpu.roll(b, s, 0), 0.0) + b
            a = a * jnp.where(keep, pltpu.roll(a, s, 0), 1.0)
            s *= 2
        hst[...] = b[tc - 1:tc, :]
        omix_s[:, 512:768] = (b * jax.nn.gelu(env["lru_y"])).astype(BF16)
        env["ret_q"] = proj(1024, 1280)

    def ret_kv_fill():
        kv = proj(1280, 1792)
        cos = cos_ref[...]
        sin = sin_ref[...]
        env["rq"] = _rope(env["ret_q"], cos, sin)
        env["rk"] = _rope(kv[:, :RET_W], cos, sin) * (RET_DK ** -0.5)
        env["rv"] = kv[:, RET_W:]

    def ret_scores_fill():
        qc = [env["rq"][rs] for rs in rows]
        kc = [env["rk"][rs] for rs in rows]
        qp = [jnp.where(head_sel, jnp.concatenate([q] * RET_HEADS, axis=0), 0.0) for q in qc]
        env["qc"] = qc
        env["sc"] = [(_mm_nt(qp[c], kc[c]) * ret_dec).astype(BF16) for c in chunks]

    def ret_intra_fill():
        vc = [env["rv"][rs].astype(BF16) for rs in rows]
        o_intra = []
        for c in chunks:
            op = jnp.where(head_sel, jnp.dot(env["sc"][c], vc[c], preferred_element_type=F32), 0.0)
            acc = op[0:ck]
            for h in range(1, RET_HEADS):
                acc = acc + op[h * ck:(h + 1) * ck]
            o_intra.append(acc)
        env["o_intra"] = o_intra
        env["s_upd"] = [jnp.where(bd, _mm_tn(env["rk"][rows[c]] * ret_kdec, vc[c]), 0.0) for c in chunks]

    def z_fill():
        env["dn_z"] = proj(0, 512)

    def g_fill():
        env["ret_g"] = proj(1792, 2048)

    def ret_post():
        for c in chunks:
            s0 = sret[...]
            oret_s[rows[c], :] = ret_eg * _mm(env["qc"][c], s0) + env["o_intra"][c]
            sret[...] = ret_gc * s0 + env["s_upd"][c]

    fillers = [lru_fill, lru_scan_fill, ret_kv_fill, ret_scores_fill, ret_intra_fill, z_fill, g_fill, ret_post]
    for c in chunks:
        dn_post_a(c)
        if 2 * c < len(fillers):
            fillers[2 * c]()
        dn_post_b(c)
        if 2 * c + 1 < len(fillers):
            fillers[2 * c + 1]()
    for f in fillers[2 * nc:]:
        f()

    nw = dnnw_ref[...]
    dn_z = env["dn_z"]
    for h in range(DN_HEADS):
        oh = odn_s[:, h * DN_DK:(h + 1) * DN_DK]
        oh = oh * lax.rsqrt(jnp.mean(oh * oh, axis=-1, keepdims=True) + EPS) * nw
        omix_s[:, h * DN_DK:(h + 1) * DN_DK] = (oh * _silu(dn_z[:, h * DN_DK:(h + 1) * DN_DK])).astype(BF16)
    omix_s[:, 768:1024] = _ret_norm_gate(oret_s[...], env["ret_g"], _head_avg_matrix()).astype(BF16)
    y_ref[...] = x + jnp.dot(omix_s[...], wout_ref[...], preferred_element_type=F32)

    @pl.when(t == pl.num_programs(1) - 1)
    def _():
        sdn_o_ref[0] = sdn[...]
        dnt_o_ref[0] = qkvbuf[0:8, :]
        h_o_ref[0] = hst[...]
        lrt_o_ref[0] = lrubuf[0:8, :]
        s_bd = sret[...]
        sretbd_o_ref[0] = s_bd
        for h in range(RET_HEADS):
            sret_o_ref[0, h] = s_bd[h * RET_DK:(h + 1) * RET_DK, h * RET_DK:(h + 1) * RET_DK]


def _mix_seq(x, lw, cos, sin, init, *, batch, seq, tc, ck):
    nt = seq // tc
    nc = tc // ck
    hc = DN_HEADS * ck
    const = lambda shape: pl.BlockSpec(shape, lambda b, t: (0,) * len(shape))
    in_specs = [
        pl.BlockSpec((tc, D_MODEL), lambda b, t: (b * nt + t, 0)),
        const((1, D_MODEL)), const((D_MODEL, 3 * DN_W)), const((8, D_MODEL)),
        const((D_MODEL, 2048)), const((CONV_W, 3 * DN_W)),
        const((8, 1)), const((8, 1)), const((1, DN_DK)),
        const((CONV_W, LRU_W)), const((1, LRU_W)), const((LRU_W, 2 * LRU_W)), const((1, 2 * LRU_W)),
        const((1, LRU_W)),
        pl.BlockSpec((tc, RET_W), lambda b, t: (t, 0)), pl.BlockSpec((tc, RET_W), lambda b, t: (t, 0)),
        const((D_MODEL, D_MODEL)),
        const((DN_HEADS, DN_DK, DN_DK)), const((8, 3 * DN_W)), const((1, LRU_W)), const((8, LRU_W)),
        const((RET_W, RET_W)),
    ]
    out_shape = [
        jax.ShapeDtypeStruct((batch * seq, D_MODEL), F32),
        jax.ShapeDtypeStruct((batch, DN_HEADS, DN_DK, DN_DK), F32),
        jax.ShapeDtypeStruct((batch, 8, 3 * DN_W), F32),
        jax.ShapeDtypeStruct((batch, 1, LRU_W), F32),
        jax.ShapeDtypeStruct((batch, 8, LRU_W), F32),
        jax.ShapeDtypeStruct((batch, RET_W, RET_W), F32),
        jax.ShapeDtypeStruct((batch, RET_HEADS, RET_DK, RET_DK), F32),
    ]
    out_specs = [
        pl.BlockSpec((tc, D_MODEL), lambda b, t: (b * nt + t, 0)),
        pl.BlockSpec((1, DN_HEADS, DN_DK, DN_DK), lambda b, t: (b, 0, 0, 0)),
        pl.BlockSpec((1, 8, 3 * DN_W), lambda b, t: (b, 0, 0)),
        pl.BlockSpec((1, 1, LRU_W), lambda b, t: (b, 0, 0)),
        pl.BlockSpec((1, 8, LRU_W), lambda b, t: (b, 0, 0)),
        pl.BlockSpec((1, RET_W, RET_W), lambda b, t: (b, 0, 0)),
        pl.BlockSpec((1, RET_HEADS, RET_DK, RET_DK), lambda b, t: (b, 0, 0, 0)),
    ]
    scratch = [
        pltpu.VMEM((DN_HEADS, DN_DK, DN_DK), F32),
        pltpu.VMEM((RET_W, RET_W), F32),
        pltpu.VMEM((1, LRU_W), F32),
        pltpu.VMEM((tc + 8, 3 * DN_W), F32),
        pltpu.VMEM((tc + 8, LRU_W), F32),
        pltpu.VMEM((DN_HEADS, tc, DN_DK), F32),
        pltpu.VMEM((DN_HEADS, tc, DN_DK), F32),
        pltpu.VMEM((DN_HEADS, tc, DN_DK), F32),
        pltpu.VMEM((tc, 8), F32),
        pltpu.VMEM((nc, 1, hc), F32),
        pltpu.VMEM((tc, DN_W), F32),
        pltpu.VMEM((tc, RET_W), F32),
        pltpu.VMEM((tc, D_MODEL), BF16),
    ]
    args = [x, lw["norm_mix"], lw["wqkv"], lw["wabt"], lw["wrest"], lw["dn_conv_w"],
            lw["alog_c"], lw["dtb_c"], lw["dn_norm_w"],
            lw["lru_conv_w"], lw["lru_conv_b"], lw["wgate"], lw["bgate"], lw["lam"], cos, sin,
            lw["w_out"], *init]
    return pl.pallas_call(
        functools.partial(_mix_seq_body, tc=tc, ck=ck),
        grid=(batch, nt),
        in_specs=in_specs,
        out_specs=out_specs,
        out_shape=out_shape,
        scratch_shapes=scratch,
        compiler_params=pltpu.CompilerParams(
            dimension_semantics=("arbitrary", "arbitrary"), vmem_limit_bytes=VMEM_LIMIT_BYTES),
        name=f"mix_seq_c{ck}",
    )(*args)


def _mix_step_body(*refs, bb, layer):
    (x_ref, nw_ref, wqkv_ref, wab_ref, wrest_ref, dncw_ref, alog_r_ref, dtb_r_ref,
     dnnw_ref, lcw_ref, lcb_ref, wgate_ref, bgate_ref, lam_ref, cos_ref, sin_ref,
     sdn_ref, dnc_ref, h_ref, lrc_ref) = refs[:20]
    refs = refs[20:]
    if layer > 0:
        prev_sdn_ref = refs[0]
        refs = refs[1:]
    (omix_o_ref, retin_o_ref, sdn_o_ref, dnc_o_ref, h_o_ref, lrc_o_ref,
     q_s, k_s, v_s, eg_s, beta_s, odn_s) = refs
    if layer > 0:
        sdn_o_ref[0:layer] = prev_sdn_ref[...]
    x = x_ref[...]
    u = _rmsnorm(x, nw_ref[...]).astype(BF16)

    qkv_pre = jnp.dot(u, wqkv_ref[...], preferred_element_type=F32)
    cw = dncw_ref[...]
    conv = (dnc_ref[0] * cw[0:1, :] + dnc_ref[1] * cw[1:2, :] + dnc_ref[2] * cw[2:3, :]
            + qkv_pre * cw[3:4, :])
    dnc_o_ref[0] = dnc_ref[1]
    dnc_o_ref[1] = dnc_ref[2]
    dnc_o_ref[2] = qkv_pre
    qkv = _silu(conv)
    for h in range(DN_HEADS):
        sl = slice(h * DN_DK, (h + 1) * DN_DK)
        qh = qkv[:, sl]
        kh = qkv[:, DN_W + h * DN_DK:DN_W + (h + 1) * DN_DK]
        q_s[:, sl] = qh * lax.rsqrt(jnp.sum(qh * qh, axis=-1, keepdims=True) + EPS) * (DN_DK ** -0.5)
        k_s[:, sl] = kh * lax.rsqrt(jnp.sum(kh * kh, axis=-1, keepdims=True) + EPS)
    v_s[...] = qkv[:, 2 * DN_W:]
    ab = jnp.dot(u, wab_ref[...], preferred_element_type=F32)
    eg_s[...] = jnp.exp(-jnp.exp(alog_r_ref[...]) * _softplus(ab[:, :128] + dtb_r_ref[...]))
    beta_s[...] = jax.nn.sigmoid(ab[:, 128:])

    rest = jnp.dot(u, wrest_ref[...], preferred_element_type=F32)
    dn_z = rest[:, 0:512]
    lru_x = rest[:, 512:768]
    lru_y = rest[:, 768:1024]
    ret_g = rest[:, 1792:2048]

    lw = lcw_ref[...]
    xl = (lrc_ref[0] * lw[0:1, :] + lrc_ref[1] * lw[1:2, :] + lrc_ref[2] * lw[2:3, :]
          + lru_x * lw[3:4, :] + lcb_ref[...])
    lrc_o_ref[0] = lrc_ref[1]
    lrc_o_ref[1] = lrc_ref[2]
    lrc_o_ref[2] = lru_x
    a, b = _lru_coeffs(xl, wgate_ref[...], bgate_ref[...], lam_ref[...])
    h_new = a * h_ref[...] + b
    h_o_ref[...] = h_new
    o_lru = h_new * jax.nn.gelu(lru_y)

    cos = cos_ref[...]
    sin = sin_ref[...]
    retin_o_ref[:, 0:256] = _rope(rest[:, 1024:1280], cos, sin)
    retin_o_ref[:, 256:512] = _rope(rest[:, 1280:1536], cos, sin) * (RET_DK ** -0.5)
    retin_o_ref[:, 512:768] = rest[:, 1536:1792]
    retin_o_ref[:, 768:1024] = ret_g

    def per_tile(i8, carry):
        r0 = pl.multiple_of(i8 * 8, 8)
        eg_t = eg_s[pl.ds(r0, 8), :]
        beta_t = beta_s[pl.ds(r0, 8), :]
        for h in range(DN_HEADS):
            sl = slice(h * DN_DK, (h + 1) * DN_DK)
            kt = k_s[pl.ds(r0, 8), sl].T
            qt = q_s[pl.ds(r0, 8), sl].T
            v_t = v_s[pl.ds(r0, 8), sl]
            rows = []
            for j in range(8):
                kcol = kt[:, j:j + 1]
                qcol = qt[:, j:j + 1]
                s0 = sdn_ref[r0 + j, h]
                eg = eg_t[j:j + 1, h:h + 1]
                ks = jnp.sum(s0 * kcol, axis=0, keepdims=True)
                qs = jnp.sum(s0 * qcol, axis=0, keepdims=True)
                qk = jnp.sum(qcol * kcol, axis=0, keepdims=True)
                uu = beta_t[j:j + 1, h:h + 1] * (v_t[j:j + 1, :] - eg * ks)
                rows.append(eg * qs + qk * uu)
                sdn_o_ref[layer, r0 + j, h] = eg * s0 + kcol * uu
            odn_s[pl.ds(r0, 8), sl] = jnp.concatenate(rows, axis=0)
        return carry

    lax.fori_loop(0, bb // 8, per_tile, 0)

    nw = dnnw_ref[...]
    for h in range(DN_HEADS):
        sl = slice(h * DN_DK, (h + 1) * DN_DK)
        oh = odn_s[:, sl]
        oh = oh * lax.rsqrt(jnp.mean(oh * oh, axis=-1, keepdims=True) + EPS) * nw
        omix_o_ref[:, sl] = (oh * _silu(dn_z[:, sl])).astype(BF16)
    omix_o_ref[:, DN_W:DN_W + LRU_W] = o_lru.astype(BF16)


def _mix_step(x, lw, cos, sin, layer, sdn, dnc, hl, lrc, prev_sdn, *, bb):
    n = x.shape[0]
    const = lambda shape: pl.BlockSpec(shape, lambda i: (0,) * len(shape))
    in_specs = [
        pl.BlockSpec((bb, D_MODEL), lambda i: (i, 0)),
        const((1, D_MODEL)), const((D_MODEL, 3 * DN_W)), const((D_MODEL, 256)), const((D_MODEL, 2048)),
        const((CONV_W, 3 * DN_W)), const((1, 128)), const((1, 128)), const((1, DN_DK)),
        const((CONV_W, LRU_W)), const((1, LRU_W)), const((LRU_W, 2 * LRU_W)), const((1, 2 * LRU_W)),
        const((1, LRU_W)), const((1, RET_W)), const((1, RET_W)),
        pl.BlockSpec((None, bb, DN_HEADS, DN_DK, DN_DK), lambda i: (layer, i, 0, 0, 0)),
        pl.BlockSpec((CONV_W - 1, bb, 3 * DN_W), lambda i: (0, i, 0)),
        pl.BlockSpec((None, bb, LRU_W), lambda i: (layer, i, 0)),
        pl.BlockSpec((CONV_W - 1, bb, LRU_W), lambda i: (0, i, 0)),
    ]
    if layer > 0:
        in_specs.append(pl.BlockSpec((layer, bb, DN_HEADS, DN_DK, DN_DK), lambda i: (0, i, 0, 0, 0)))
    out_shape = [
        jax.ShapeDtypeStruct((n, DN_W + LRU_W), BF16),
        jax.ShapeDtypeStruct((n, 4 * RET_W), F32),
        jax.ShapeDtypeStruct((layer + 1, n, DN_HEADS, DN_DK, DN_DK), F32),
        jax.ShapeDtypeStruct((CONV_W - 1, n, 3 * DN_W), F32),
        jax.ShapeDtypeStruct((n, LRU_W), F32),
        jax.ShapeDtypeStruct((CONV_W - 1, n, LRU_W), F32),
    ]
    out_specs = [
        pl.BlockSpec((bb, DN_W + LRU_W), lambda i: (i, 0)),
        pl.BlockSpec((bb, 4 * RET_W), lambda i: (i, 0)),
        pl.BlockSpec((layer + 1, bb, DN_HEADS, DN_DK, DN_DK), lambda i: (0, i, 0, 0, 0)),
        pl.BlockSpec((CONV_W - 1, bb, 3 * DN_W), lambda i: (0, i, 0)),
        pl.BlockSpec((bb, LRU_W), lambda i: (i, 0)),
        pl.BlockSpec((CONV_W - 1, bb, LRU_W), lambda i: (0, i, 0)),
    ]
    scratch = [
        pltpu.VMEM((bb, DN_W), F32), pltpu.VMEM((bb, DN_W), F32), pltpu.VMEM((bb, DN_W), F32),
        pltpu.VMEM((bb, 128), F32), pltpu.VMEM((bb, 128), F32), pltpu.VMEM((bb, DN_W), F32),
    ]
    args = [x, lw["norm_mix"], lw["wqkv"], lw["wab"], lw["wrest"], lw["dn_conv_w"], lw["alog_r"],
            lw["dtb_r"], lw["dn_norm_w"], lw["lru_conv_w"], lw["lru_conv_b"], lw["wgate"], lw["bgate"],
            lw["lam"], cos, sin, sdn, dnc, hl, lrc]
    if layer > 0:
        args.append(prev_sdn)
    return pl.pallas_call(
        functools.partial(_mix_step_body, bb=bb, layer=layer),
        grid=(n // bb,),
        in_specs=in_specs,
        out_specs=out_specs,
        out_shape=out_shape,
        scratch_shapes=scratch,
        compiler_params=pltpu.CompilerParams(
            dimension_semantics=("arbitrary",), vmem_limit_bytes=VMEM_LIMIT_BYTES),
        name="mix_step",
    )(*args)


def _ret_step_body(*refs, layer):
    x_ref, omix_ref, retin_ref, wout_ref, s_ref = refs[:5]
    refs = refs[5:]
    if layer > 0:
        prev_ref = refs[0]
        refs = refs[1:]
    y_ref, s_o_ref, qt_s, kt_s, vt_s, ot_s = refs
    h = pl.program_id(0)
    if layer > 0:
        s_o_ref[0:layer] = prev_ref[...]

    @pl.when(h == 0)
    def _():
        qt_s[...] = retin_ref[:, 0:RET_W].T
        kt_s[...] = retin_ref[:, RET_W:2 * RET_W].T
        vt_s[...] = retin_ref[:, 2 * RET_W:3 * RET_W].T

    rows = pl.ds(pl.multiple_of(h * RET_DK, RET_DK), RET_DK)
    q = qt_s[rows, :]
    k = kt_s[rows, :]
    v = vt_s[rows, :]
    gamma = jnp.float32(math.exp(LOG_GAMMA[0]))
    for i in range(1, RET_HEADS):
        gamma = jnp.where(h == i, jnp.float32(math.exp(LOG_GAMMA[i])), gamma)
    acc = jnp.zeros_like(v)
    for d in range(RET_DK):
        s_d = s_ref[d]
        acc = acc + s_d * q[d:d + 1, :]
        s_o_ref[layer, 0, d] = gamma * s_d + k[d:d + 1, :] * v
    ot_s[rows, :] = gamma * acc + jnp.sum(q * k, axis=0, keepdims=True) * v

    @pl.when(h == pl.num_programs(0) - 1)
    def _():
        o_ret = _ret_norm_gate(ot_s[...].T, retin_ref[:, 3 * RET_W:], _head_avg_matrix())
        omix = jnp.concatenate([omix_ref[...], o_ret.astype(BF16)], axis=1)
        y_ref[...] = x_ref[...] + jnp.dot(omix, wout_ref[...], preferred_element_type=F32)


def _ret_step(x, omix_part, retin, w_out, sret_t, layer, prev, *, n):
    const = lambda shape: pl.BlockSpec(shape, lambda h: (0,) * len(shape))
    in_specs = [
        const((n, D_MODEL)), const((n, DN_W + LRU_W)), const((n, 4 * RET_W)), const((D_MODEL, D_MODEL)),
        pl.BlockSpec((None, None, RET_DK, RET_DK, n), lambda h: (layer, h, 0, 0, 0)),
    ]
    args = [x, omix_part, retin, w_out, sret_t]
    if layer > 0:
        in_specs.append(pl.BlockSpec((layer, 1, RET_DK, RET_DK, n), lambda h: (0, h, 0, 0, 0)))
        args.append(prev)
    return pl.pallas_call(
        functools.partial(_ret_step_body, layer=layer),
        grid=(RET_HEADS,),
        in_specs=in_specs,
        out_specs=[const((n, D_MODEL)),
                   pl.BlockSpec((layer + 1, 1, RET_DK, RET_DK, n), lambda h: (0, h, 0, 0, 0))],
        out_shape=[jax.ShapeDtypeStruct((n, D_MODEL), F32),
                   jax.ShapeDtypeStruct((layer + 1, RET_HEADS, RET_DK, RET_DK, n), F32)],
        scratch_shapes=[pltpu.VMEM((RET_W, n), F32)] * 4,
        compiler_params=pltpu.CompilerParams(
            dimension_semantics=("arbitrary",), vmem_limit_bytes=VMEM_LIMIT_BYTES),
        name="ret_step",
    )(*args)


def _rope_tables(pos):
    half = RET_DK // 2
    inv = ROPE_BASE ** (-jnp.arange(half, dtype=F32) / half)
    ang = pos.astype(F32)[:, None] * inv[None, :]
    cos = jnp.cos(ang)
    sin = jnp.sin(ang)
    cos_full = jnp.tile(jnp.concatenate([cos, cos], axis=-1), (1, RET_HEADS))
    sin_signed = jnp.tile(jnp.concatenate([-sin, sin], axis=-1), (1, RET_HEADS))
    return cos_full, sin_signed


def _block_diag(w):
    n, d, e = w.shape
    eye = jnp.eye(n, dtype=w.dtype)
    return (eye[:, None, :, None] * w[:, :, None, :]).reshape(n * d, n * e)


def _layer_weights(l, norm_mix, w_in, dn_conv_w, dn_a_log, dn_dt_bias, dn_norm_w, lru_conv_w, lru_conv_b,
                   lru_wa, lru_ba, lru_wx, lru_bx, lru_lambda, w_out):
    wl = w_in[l]
    o_a = 3 * DN_W
    wa = wl[:, o_a:o_a + DN_HEADS]
    wb = wl[:, o_a + DN_HEADS:o_a + 2 * DN_HEADS]
    pad_cols = lambda w: jnp.pad(w, ((0, 0), (0, 128 - DN_HEADS)))
    pad_lane = lambda v: jnp.pad(v, (0, 128 - DN_HEADS)).reshape(1, 128)
    pad_sub = lambda v: jnp.pad(v, (0, 8 - DN_HEADS)).reshape(8, 1)
    return {
        "norm_mix": norm_mix[l].reshape(1, D_MODEL),
        "wqkv": wl[:, :o_a].astype(BF16),
        "wab": jnp.concatenate([pad_cols(wa), pad_cols(wb)], axis=1).astype(BF16),
        "wabt": jnp.concatenate([wa, wb], axis=1).T.astype(BF16),
        "wrest": wl[:, o_a + 2 * DN_HEADS:].astype(BF16),
        "dn_conv_w": dn_conv_w[l],
        "alog_r": pad_lane(dn_a_log[l]), "dtb_r": pad_lane(dn_dt_bias[l]),
        "alog_c": pad_sub(dn_a_log[l]), "dtb_c": pad_sub(dn_dt_bias[l]),
        "dn_norm_w": dn_norm_w[l].reshape(1, DN_DK),
        "lru_conv_w": lru_conv_w[l], "lru_conv_b": lru_conv_b[l].reshape(1, LRU_W),
        "wgate": jnp.concatenate([_block_diag(lru_wa[l]), _block_diag(lru_wx[l])], axis=1).astype(BF16),
        "bgate": jnp.concatenate([lru_ba[l], lru_bx[l]]).reshape(1, 2 * LRU_W),
        "lam": lru_lambda[l].reshape(1, LRU_W),
        "w_out": w_out[l].astype(BF16),
    }


def kernel(x_prompt, x_sample, state_dn, state_dn_conv, state_lru, state_lru_conv, state_ret, meta_tokens, norm_ffn1, w_ffn1_in, w_ffn1_out, norm_mix, w_in, dn_conv_w, dn_a_log, dn_dt_bias, dn_norm_w, lru_conv_w, lru_conv_b, lru_wa, lru_ba, lru_wx, lru_bx, lru_lambda, w_out, norm_ffn2, w_ffn2_in, w_ffn2_out, norm_final):
    batch, seq, _ = x_prompt.shape
    n_dec = x_sample.shape[0]
    depth = w_in.shape[0]
    f1_in, f1_out = w_ffn1_in.astype(BF16), w_ffn1_out.astype(BF16)
    f2_in, f2_out = w_ffn2_in.astype(BF16), w_ffn2_out.astype(BF16)
    lws = [_layer_weights(l, norm_mix, w_in, dn_conv_w, dn_a_log, dn_dt_bias, dn_norm_w, lru_conv_w,
                          lru_conv_b, lru_wa, lru_ba, lru_wx, lru_bx, lru_lambda, w_out)
           for l in range(depth)]
    cos_m, sin_m = _rope_tables(jnp.arange(N_META))
    cos_p, sin_p = _rope_tables(N_META + jnp.arange(seq))
    cos_s, sin_s = _rope_tables(PAST_LEN + jnp.arange(1))

    n_small = n_dec + N_META
    xs = jnp.concatenate([x_sample[:, 0, :], meta_tokens.astype(F32)], axis=0)
    zero_init = (jnp.zeros((DN_HEADS, DN_DK, DN_DK), F32), jnp.zeros((8, 3 * DN_W), F32),
                 jnp.zeros((1, LRU_W), F32), jnp.zeros((8, LRU_W), F32), jnp.zeros((RET_W, RET_W), F32))
    new_s, meta_state = [], []
    sdn_all = sret_all = None
    sret_t = jnp.transpose(state_ret, (0, 2, 3, 4, 1))
    for l in range(depth):
        last = l == depth - 1
        xs = _ffn(xs, norm_ffn1[l], f1_in, f1_out, l, tm=n_small, nf=1)
        omix_part, retin, sdn_all, dnc, hl, lrc = _mix_step(
            xs[:n_dec], lws[l], cos_s, sin_s, l, state_dn, jnp.swapaxes(state_dn_conv[l], 0, 1),
            state_lru, jnp.swapaxes(state_lru_conv[l], 0, 1), sdn_all, bb=8)
        ys, sret_all = _ret_step(xs[:n_dec], omix_part, retin, lws[l]["w_out"], sret_t, l, sret_all,
                                 n=n_dec)
        ym, m_sdn, m_dnt, m_h, m_lrt, m_sretbd, _ = _mix_seq(
            xs[n_dec:], lws[l], cos_m, sin_m, zero_init, batch=1, seq=N_META, tc=N_META, ck=N_META)
        new_s.append((jnp.swapaxes(dnc, 0, 1), hl, jnp.swapaxes(lrc, 0, 1)))
        meta_state.append((m_sdn[0], m_dnt[0], m_h[0], m_lrt[0], m_sretbd[0]))
        xs = jnp.concatenate([ys, ym], axis=0)
        xs = _ffn(xs, norm_ffn2[l], f2_in, f2_out, l, norm_final if last else None, tm=n_small, nf=1)
    y_sample = xs[:n_dec].reshape(n_dec, 1, D_MODEL)

    xp = x_prompt.reshape(batch * seq, D_MODEL)
    new_p = []
    for l in range(depth):
        last = l == depth - 1
        xp = _ffn(xp, norm_ffn1[l], f1_in, f1_out, l, tm=512, nf=1)
        xp, sdn, dnt, hl, lrt, _, sret = _mix_seq(
            xp, lws[l], cos_p, sin_p, meta_state[l], batch=batch, seq=seq, tc=256, ck=CHUNK)
        new_p.append((sdn, dnt[:, 5:8], hl[:, 0], lrt[:, 5:8], sret))
        xp = _ffn(xp, norm_ffn2[l], f2_in, f2_out, l, norm_final if last else None, tm=512, nf=1)
    y_prompt = xp.reshape(batch, seq, D_MODEL)

    outs_p = [jnp.stack([s[j] for s in new_p]) for j in range(5)]
    dnc_s, lru_s, lrc_s = [jnp.stack([s[j] for s in new_s]) for j in range(3)]
    sret_s = jnp.transpose(sret_all, (0, 4, 1, 2, 3))
    return (y_prompt, y_sample, *outs_p, sdn_all, dnc_s, lru_s, lrc_s, sret_s)
```

```python
import functools
import math

import jax
import jax.numpy as jnp
from jax import lax
from jax.experimental import pallas as pl
from jax.experimental.pallas import tpu as pltpu

F32 = jnp.float32
BF16 = jnp.bfloat16

D_MODEL = 1024
N_META = 16
PAST_LEN = 16384
DN_HEADS = 4
DN_DK = 128
DN_W = 512
LRU_W = 256
LRU_BLOCKS = 4
LRU_C = 8.0
RET_HEADS = 4
RET_DK = 64
RET_W = 256
CONV_W = 4
CHUNK = 64
D_FF = 2816
ROPE_BASE = 10000.0
EPS = 1e-6
SUB = 16
LOG_GAMMA = tuple(math.log1p(-2.0 ** (-5.0 - h)) for h in range(RET_HEADS))

VMEM_LIMIT_BYTES = 56 * 1024 * 1024


def _mm(a, b):
    return jnp.dot(a.astype(BF16), b.astype(BF16), preferred_element_type=F32)


def _mm_nt(a, b):
    return lax.dot_general(a.astype(BF16), b.astype(BF16), (((1,), (1,)), ((), ())),
                           preferred_element_type=F32)


def _mm_tn(a, b):
    return lax.dot_general(a.astype(BF16), b.astype(BF16), (((0,), (0,)), ((), ())),
                           preferred_element_type=F32)


def _split3(x):
    hi = x.astype(BF16)
    r = x - hi.astype(F32)
    mid = r.astype(BF16)
    lo = (r - mid.astype(F32)).astype(BF16)
    return hi, mid, lo


def _mm_01_left(m01, x):
    m = m01.astype(BF16)
    return sum(jnp.dot(m, p, preferred_element_type=F32) for p in _split3(x))


def _mm_01_right(x, m01):
    m = m01.astype(BF16)
    return sum(jnp.dot(p, m, preferred_element_type=F32) for p in _split3(x))


def _mm_split(x, m_bf16):
    hi = x.astype(BF16)
    lo = (x - hi.astype(F32)).astype(BF16)
    return (jnp.dot(hi, m_bf16, preferred_element_type=F32)
            + jnp.dot(lo, m_bf16, preferred_element_type=F32))


def _rmsnorm(x, w):
    return x * lax.rsqrt(jnp.mean(x * x, axis=-1, keepdims=True) + EPS) * w


def _silu(x):
    return x * jax.nn.sigmoid(x)


def _softplus(x):
    return jnp.maximum(x, 0.0) + jnp.log1p(jnp.exp(-jnp.abs(x)))


def _iota(shape, dim):
    return lax.broadcasted_iota(jnp.int32, shape, dim)


def _lane_log_gamma(shape, dim, width):
    head = _iota(shape, dim) // width
    out = jnp.full(shape, LOG_GAMMA[0], F32)
    for h in range(1, RET_HEADS):
        out = jnp.where(head == h, LOG_GAMMA[h], out)
    return out


def _rope(x, cos, sin_signed):
    n = x.shape[1]
    half = RET_DK // 2
    first = (_iota(x.shape, 1) % RET_DK) < half
    swapped = jnp.where(first, pltpu.roll(x, n - half, 1), pltpu.roll(x, half, 1))
    return x * cos + swapped * sin_signed


def _ret_norm_gate(o, gate, avg_bf16):
    mu = _mm_split(o, avg_bf16)
    d = o - mu
    var = _mm_split(d * d, avg_bf16)
    return d * lax.rsqrt(var + EPS) * _silu(gate)


def _head_avg_matrix():
    r = _iota((RET_W, RET_W), 0) // RET_DK
    c = _iota((RET_W, RET_W), 1) // RET_DK
    return jnp.where(r == c, 1.0 / RET_DK, 0.0).astype(BF16)


def _lru_coeffs(xl, wgate, bgate, lam):
    gates = _mm(xl, wgate) + bgate
    r = jax.nn.sigmoid(gates[:, :LRU_W])
    i = jax.nn.sigmoid(gates[:, LRU_W:])
    log_a = -LRU_C * r * _softplus(-lam)
    a = jnp.exp(log_a)
    b = jnp.sqrt(jnp.maximum(-jnp.tanh(log_a) * (a * a + 1.0), 0.0)) * (i * xl)
    return a, b


def _ffn_body(*refs, final, nf):
    if final:
        x_ref, nw_ref, wi_ref, wo_ref, fw_ref, o_ref = refs
    else:
        x_ref, nw_ref, wi_ref, wo_ref, o_ref = refs
    tf = D_FF // nf
    x = x_ref[...]
    u = _rmsnorm(x, nw_ref[...]).astype(BF16)
    acc = None
    for j in range(nf):
        gate = jnp.dot(u, wi_ref[:, j * tf:(j + 1) * tf], preferred_element_type=F32)
        up = jnp.dot(u, wi_ref[:, D_FF + j * tf:D_FF + (j + 1) * tf], preferred_element_type=F32)
        h = (_silu(gate) * up).astype(BF16)
        part = jnp.dot(h, wo_ref[j * tf:(j + 1) * tf, :], preferred_element_type=F32)
        acc = part if acc is None else acc + part
    y = x + 0.5 * acc
    if final:
        y = _rmsnorm(y, fw_ref[...])
    o_ref[...] = y


def _ffn(x, norm_w, w_in, w_out, layer, final_w=None, *, tm, nf):
    n = x.shape[0]
    final = final_w is not None
    resident = pl.Buffered(1)
    in_specs = [
        pl.BlockSpec((tm, D_MODEL), lambda i: (i, 0)),
        pl.BlockSpec((1, D_MODEL), lambda i: (0, 0)),
        pl.BlockSpec((None, D_MODEL, 2 * D_FF), lambda i: (layer, 0, 0), pipeline_mode=resident),
        pl.BlockSpec((None, D_FF, D_MODEL), lambda i: (layer, 0, 0), pipeline_mode=resident),
    ]
    args = [x, norm_w.reshape(1, D_MODEL), w_in, w_out]
    if final:
        in_specs.append(pl.BlockSpec((1, D_MODEL), lambda i: (0, 0)))
        args.append(final_w.reshape(1, D_MODEL))
    return pl.pallas_call(
        functools.partial(_ffn_body, final=final, nf=nf),
        grid=(n // tm,),
        in_specs=in_specs,
        out_specs=pl.BlockSpec((tm, D_MODEL), lambda i: (i, 0)),
        out_shape=jax.ShapeDtypeStruct((n, D_MODEL), F32),
        compiler_params=pltpu.CompilerParams(
            dimension_semantics=("arbitrary",), vmem_limit_bytes=VMEM_LIMIT_BYTES),
        name="ffn_final" if final else "ffn",
    )(*args)


def _heads_per_group(ck):
    return max(1, min(DN_HEADS, 128 // ck))


def _tri_inv_steps(a, n_sub, out):
    m = a[0].shape[0]
    r = _iota((m, m), 0)
    c = _iota((m, m), 1)
    eye = jnp.where(r == c, 1.0, 0.0).astype(F32)
    diag = (r // SUB) == (c // SUB)
    x = [jnp.where(diag, ai, 0.0) for ai in a]
    off = [ai - xi for ai, xi in zip(a, x)]
    p = [eye - xi for xi in x]
    for _ in range(3):
        x = [_mm(xi, xi) for xi in x]
        yield
        p = [pi + _mm(pi, xi) for pi, xi in zip(p, x)]
        yield
    if n_sub == 1:
        out.extend(p)
        return
    n = [_mm(pi, oi) for pi, oi in zip(p, off)]
    yield
    n2 = [_mm(ni, ni) for ni in n]
    yield
    rr = [eye - ni + n2i - _mm(ni, n2i) for ni, n2i in zip(n, n2)]
    yield
    out.extend(_mm(ri, pi) for ri, pi in zip(rr, p))
    yield


def _mix_pipe_body(x_ref, nw_ref, wqkv_ref, wabt_ref, wrest_ref, dncw_ref,
                   alog_c_ref, dtb_c_ref, dnnw_ref,
                   lcw_ref, lcb_ref, wgate_ref, bgate_ref, lam_ref, cos_ref, sin_ref, wout_ref,
                   sdn0_ref, dnt0_ref, h0_ref, lrt0_ref, sret0_ref,
                   y_ref, sdn_o_ref, dnt_o_ref, h_o_ref, lrt_o_ref, sretbd_o_ref, sret_o_ref,
                   sdn, sret, hst, qkvbuf, lrubuf, q_s, k_s, v_s, gb_s, grow_s, odn_s, oret_s, omix_s,
                   h_uv, h_wq, h_pm, h_kd, h_egl, h_rq, h_oi, h_su, h_z, h_rg, h_x, h_lru,
                   *, tc, ck, nt, n_blocks):
    i = pl.program_id(0)
    t_a = i % nt
    t_b = (i + nt - 1) % nt
    nc = tc // ck
    hc = DN_HEADS * ck
    n_sub = ck // SUB
    chunks = range(nc)
    rows = [slice(c * ck, (c + 1) * ck) for c in chunks]
    heads = [(h * ck, (h + 1) * ck) for h in range(DN_HEADS)]
    hg = _heads_per_group(ck)
    ng = DN_HEADS // hg
    gc = hg * ck
    probs = [(c, g) for c in chunks for g in range(ng)]

    @pl.when(i == 0)
    def _():
        for ref in (h_uv, h_wq, h_pm, h_kd, h_egl, h_rq, h_oi, h_su, h_z, h_rg, h_x, h_lru):
            ref[...] = jnp.zeros_like(ref)

    @pl.when(t_a == 0)
    def _():
        hst[...] = h0_ref[...]
        qkvbuf[0:8, :] = dnt0_ref[...]
        lrubuf[0:8, :] = lrt0_ref[...]

    @pl.when((t_b == 0) | (i == 0))
    def _():
        sdn[...] = sdn0_ref[...]
        sret[...] = sret0_ref[...]

    sr = _iota((gc, gc), 0)
    sc = _iota((gc, gc), 1)
    strict = ((sr // ck) == (sc // ck)) & (sr > sc)
    eye_g = jnp.where(sr == sc, 1.0, 0.0).astype(F32)
    lg_rows = _lane_log_gamma((hc, ck), 0, ck)
    tpos = _iota((hc, ck), 0) % ck
    ipos = _iota((hc, ck), 1)
    causal = tpos >= ipos
    ret_dec = jnp.where(causal, jnp.exp(jnp.where(causal, (tpos - ipos).astype(F32) * lg_rows, 0.0)), 0.0)
    head_sel = (_iota((hc, RET_W), 0) // ck) == (_iota((hc, RET_W), 1) // RET_DK)
    lg_lane = _lane_log_gamma((ck, RET_W), 1, RET_DK)
    tl = _iota((ck, RET_W), 0).astype(F32)
    ret_eg = jnp.exp((tl + 1.0) * lg_lane)
    ret_kdec = jnp.exp((ck - 1.0 - tl) * lg_lane)
    ret_gc = jnp.exp(ck * _lane_log_gamma((1, RET_W), 1, RET_DK))
    bd = (_iota((RET_W, RET_W), 0) // RET_DK) == (_iota((RET_W, RET_W), 1) // RET_DK)

    a = {}
    b = {}

    def a_norm():
        a["x"] = x_ref[...]
        a["u"] = _rmsnorm(a["x"], nw_ref[...]).astype(BF16)

    def a_qkv_piece(p):
        cols = slice(2 * DN_DK * p, 2 * DN_DK * (p + 1))
        qkvbuf[8:8 + tc, cols] = jnp.dot(a["u"], wqkv_ref[:, cols], preferred_element_type=F32)

    def a_conv_piece(p):
        cols = slice(2 * DN_DK * p, 2 * DN_DK * (p + 1))
        cw = dncw_ref[:, cols]
        conv = qkvbuf[pl.ds(5, tc), cols] * cw[0:1, :]
        for j in range(1, CONV_W):
            conv = conv + qkvbuf[pl.ds(5 + j, tc), cols] * cw[j:j + 1, :]
        qkvbuf[0:8, cols] = qkvbuf[tc:tc + 8, cols]
        act = _silu(conv)
        kind, first_head = divmod(2 * p, DN_HEADS)
        for j in range(2):
            v = act[:, j * DN_DK:(j + 1) * DN_DK]
            if kind == 0:
                q_s[first_head + j] = (v * lax.rsqrt(jnp.sum(v * v, axis=-1, keepdims=True) + EPS)
                                       * (DN_DK ** -0.5))
            elif kind == 1:
                k_s[first_head + j] = v * lax.rsqrt(jnp.sum(v * v, axis=-1, keepdims=True) + EPS)
            else:
                v_s[first_head + j] = v

    def proj(lo, hi):
        return jnp.dot(a["u"], wrest_ref[:, lo:hi], preferred_element_type=F32)

    def a_lru_in():
        lru = proj(512, 1024)
        lrubuf[8:8 + tc, :] = lru[:, :LRU_W]
        lw = lcw_ref[...]
        xl = lrubuf[pl.ds(5, tc), :] * lw[0:1, :]
        for j in range(1, CONV_W):
            xl = xl + lrubuf[pl.ds(5 + j, tc), :] * lw[j:j + 1, :]
        lrubuf[0:8, :] = lrubuf[tc:tc + 8, :]
        a["xl"] = xl + lcb_ref[...]
        a["lru_y"] = lru[:, LRU_W:]

    def a_ret_q():
        a["ret_q"] = proj(1024, 1280)

    def a_ret_kv():
        kv = proj(1280, 1792)
        cos = cos_ref[...]
        sin = sin_ref[...]
        a["rq"] = _rope(a["ret_q"], cos, sin)
        a["rk"] = _rope(kv[:, :RET_W], cos, sin) * (RET_DK ** -0.5)
        a["rv"] = kv[:, RET_W:]

    def a_z():
        a["dn_z"] = proj(0, 512)

    def a_g():
        a["ret_g"] = proj(1792, 2048)

    def a_gates():
        abt = lax.dot_general(wabt_ref[...], a["u"], (((1,), (1,)), ((), ())), preferred_element_type=F32)
        g_row = -jnp.exp(alog_c_ref[...]) * _softplus(abt + dtb_c_ref[...])
        rr = _iota((tc, tc), 0)
        cc = _iota((tc, tc), 1)
        upper = jnp.where(((rr // ck) == (cc // ck)) & (rr <= cc), 1.0, 0.0)
        big_g_row = _mm_01_right(g_row, upper)
        is_g = _iota((8, tc), 0) < DN_HEADS
        gb_s[...] = jnp.where(is_g, big_g_row, jax.nn.sigmoid(abt)).T
        for c in chunks:
            grow_s[c] = jnp.concatenate(
                [big_g_row[h:h + 1, c * ck:(c + 1) * ck] for h in range(DN_HEADS)], axis=1)

    def a_lru_scan_steps():
        ga, gb = _lru_coeffs(a["xl"], wgate_ref[...], bgate_ref[...], lam_ref[...])
        row = _iota((tc, LRU_W), 0)
        gb = gb + jnp.where(row == 0, ga * hst[...], 0.0)
        yield
        s = 1
        while s < tc:
            keep = row >= s
            gb = ga * jnp.where(keep, pltpu.roll(gb, s, 0), 0.0) + gb
            ga = ga * jnp.where(keep, pltpu.roll(ga, s, 0), 1.0)
            s *= 2
            yield
        hst[...] = gb[tc - 1:tc, :]
        a["o_lru"] = (gb * jax.nn.gelu(a["lru_y"])).astype(BF16)
        yield

    def a_kkqk():
        def stack(ref, c, g):
            return jnp.concatenate([ref[h, rows[c], :] for h in range(g * hg, (g + 1) * hg)], axis=0)

        def stack_col(c, g, first):
            return jnp.concatenate([gb_s[rows[c], first + h:first + h + 1]
                                    for h in range(g * hg, (g + 1) * hg)], axis=0)

        a["kst"] = [stack(k_s, c, g) for c, g in probs]
        a["qst"] = [stack(q_s, c, g) for c, g in probs]
        a["vst"] = [stack(v_s, c, g) for c, g in probs]
        gst = [stack_col(c, g, 0) for c, g in probs]
        a["gst"] = gst
        a["bst"] = [stack_col(c, g, DN_HEADS) for c, g in probs]
        dstrict = []
        for k, (c, g) in enumerate(probs):
            diff = gst[k] - grow_s[c][:, g * gc:(g + 1) * gc]
            dstrict.append(jnp.where(strict, jnp.exp(jnp.where(strict, diff, 0.0)), 0.0))
        a["eg"] = [jnp.exp(g) for g in gst]
        a["amat"] = [a["bst"][k] * _mm_nt(a["kst"][k], a["kst"][k]) * dstrict[k] for k in range(len(probs))]
        a["pmat"] = [(_mm_nt(a["qst"][k], a["kst"][k]) * (dstrict[k] + eye_g)).astype(BF16)
                     for k in range(len(probs))]

    tinv = []

    def a_sol():
        kst, qst, vst, gst, bst, eg = (a[k] for k in ("kst", "qst", "vst", "gst", "bst", "eg"))
        sol = [_mm(tinv[k], jnp.concatenate([bst[k] * vst[k], (bst[k] * eg[k]) * kst[k]], axis=1))
               for k in range(len(probs))]
        a["uv"] = [s[:, :DN_DK] for s in sol]
        wq, egl, kd = {}, {}, {}
        for k, (c, g) in enumerate(probs):
            for j in range(hg):
                lo, hi = j * ck, (j + 1) * ck
                key = (c, g * hg + j)
                wq[key] = jnp.concatenate([sol[k][lo:hi, DN_DK:], eg[k][lo:hi] * qst[k][lo:hi]],
                                          axis=0).astype(BF16)
                g_last = gst[k][hi - 1:hi]
                egl[key] = jnp.exp(g_last)
                kd[key] = (kst[k][lo:hi] * jnp.exp(g_last - gst[k][lo:hi])).astype(BF16)
        a["wq"], a["egl"], a["kd"] = wq, egl, kd

    def a_ret_scores():
        qc = [a["rq"][rs] for rs in rows]
        kc = [a["rk"][rs] for rs in rows]
        qp = [jnp.where(head_sel, jnp.concatenate([q] * RET_HEADS, axis=0), 0.0) for q in qc]
        a["qc"] = qc
        a["sc"] = [(_mm_nt(qp[c], kc[c]) * ret_dec).astype(BF16) for c in chunks]

    def a_ret_intra():
        vc = [a["rv"][rs].astype(BF16) for rs in rows]
        o_intra = []
        for c in chunks:
            op = jnp.where(head_sel, jnp.dot(a["sc"][c], vc[c], preferred_element_type=F32), 0.0)
            acc = op[0:ck]
            for h in range(1, RET_HEADS):
                acc = acc + op[h * ck:(h + 1) * ck]
            o_intra.append(acc)
        a["o_intra"] = o_intra
        a["s_upd"] = [jnp.where(bd, _mm_tn(a["rk"][rows[c]] * ret_kdec, vc[c]), 0.0) for c in chunks]

    def a_handoff():
        for k, (c, g) in enumerate(probs):
            h_uv[c, g * gc:(g + 1) * gc, :] = a["uv"][k]
            h_pm[c, g] = a["pmat"][k]
        for c in chunks:
            h_rq[c] = a["qc"][c].astype(BF16)
            h_oi[c] = a["o_intra"][c]
            h_su[c] = a["s_upd"][c]
            for h in range(DN_HEADS):
                h_wq[c, h] = a["wq"][c, h]
                h_kd[c, h] = a["kd"][c, h]
                h_egl[c, h:h + 1, :] = jnp.broadcast_to(a["egl"][c, h], (1, DN_DK))
        h_z[...] = a["dn_z"]
        h_rg[...] = a["ret_g"]
        h_x[...] = a["x"]
        h_lru[...] = a["o_lru"]

    def b_post_a(c):
        ws_qs = [_mm(h_wq[c, h], sdn[h]) for h in range(DN_HEADS)]
        b["u"] = [h_uv[c, lo:hi, :] - ws_qs[h][:ck] for h, (lo, hi) in enumerate(heads)]
        b["qs"] = [w[ck:] for w in ws_qs]

    def b_post_b(c):
        for g in range(ng):
            group = range(g * hg, (g + 1) * hg)
            ost = (jnp.concatenate([b["qs"][h] for h in group], axis=0)
                   + _mm(h_pm[c, g], jnp.concatenate([b["u"][h] for h in group], axis=0)))
            for j, h in enumerate(group):
                odn_s[rows[c], h * DN_DK:(h + 1) * DN_DK] = ost[j * ck:(j + 1) * ck]
        for h in range(DN_HEADS):
            sdn[h] = h_egl[c, h:h + 1, :] * sdn[h] + _mm_tn(h_kd[c, h], b["u"][h])

    def b_ret():
        for c in chunks:
            s0 = sret[...]
            oret_s[rows[c], :] = ret_eg * _mm(h_rq[c], s0) + h_oi[c]
            sret[...] = ret_gc * s0 + h_su[c]

    def b_out():
        nw = dnnw_ref[...]
        for h in range(DN_HEADS):
            sl = slice(h * DN_DK, (h + 1) * DN_DK)
            oh = odn_s[:, sl]
            oh = oh * lax.rsqrt(jnp.mean(oh * oh, axis=-1, keepdims=True) + EPS) * nw
            omix_s[:, sl] = (oh * _silu(h_z[:, sl])).astype(BF16)
        omix_s[:, 512:768] = h_lru[...]
        omix_s[:, 768:1024] = _ret_norm_gate(oret_s[...], h_rg[...], _head_avg_matrix()).astype(BF16)
        y_ref[...] = h_x[...] + jnp.dot(omix_s[...], wout_ref[...], preferred_element_type=F32)

    n_piece = 3 * DN_W // (2 * DN_DK)
    a_seq = [a_norm, functools.partial(a_qkv_piece, 0)]
    for p in range(1, n_piece):
        a_seq.append(lambda p=p: (a_qkv_piece(p), a_conv_piece(p - 1)))
    a_seq += [lambda: (a_lru_in(), a_conv_piece(n_piece - 1)), lambda: (a_gates(), a_ret_q()),
              a_ret_kv, a_z, a_g, a_kkqk]
    b_seq = []
    for c in chunks:
        b_seq += [functools.partial(b_post_a, c), functools.partial(b_post_b, c)]
    b_seq += [b_ret, b_out]
    b_seq.pop(0)()
    for fa in a_seq:
        fa()
        if b_seq:
            b_seq.pop(0)()
    for fb in b_seq:
        fb()
    scan = a_lru_scan_steps()
    for _ in _tri_inv_steps(a["amat"], n_sub, tinv):
        next(scan, None)
    for _ in scan:
        pass
    a_sol()
    a_ret_scores()
    a_ret_intra()
    a_handoff()

    @pl.when((t_a == nt - 1) & (i < n_blocks))
    def _():
        dnt_o_ref[0] = qkvbuf[0:8, :]
        h_o_ref[0] = hst[...]
        lrt_o_ref[0] = lrubuf[0:8, :]

    @pl.when((t_b == nt - 1) & (i > 0))
    def _():
        sdn_o_ref[0] = sdn[...]
        s_bd = sret[...]
        sretbd_o_ref[0] = s_bd
        for h in range(RET_HEADS):
            sret_o_ref[0, h] = s_bd[h * RET_DK:(h + 1) * RET_DK, h * RET_DK:(h + 1) * RET_DK]


def _mix_pipe(x, lw, cos, sin, init, *, batch, seq, tc, ck):
    nt = seq // tc
    nc = tc // ck
    hc = DN_HEADS * ck
    hg = _heads_per_group(ck)
    ng, gc = DN_HEADS // hg, hg * ck
    n_blocks = batch * nt
    blk_a = lambda i: jnp.minimum(i, n_blocks - 1)
    blk_b = lambda i: jnp.maximum(i - 1, 0)
    const = lambda shape: pl.BlockSpec(shape, lambda i: (0,) * len(shape))
    in_specs = [
        pl.BlockSpec((tc, D_MODEL), lambda i: (blk_a(i), 0)),
        const((1, D_MODEL)), const((D_MODEL, 3 * DN_W)), const((8, D_MODEL)),
        const((D_MODEL, 2048)), const((CONV_W, 3 * DN_W)),
        const((8, 1)), const((8, 1)), const((1, DN_DK)),
        const((CONV_W, LRU_W)), const((1, LRU_W)), const((LRU_W, 2 * LRU_W)), const((1, 2 * LRU_W)),
        const((1, LRU_W)),
        pl.BlockSpec((tc, RET_W), lambda i: (blk_a(i) % nt, 0)),
        pl.BlockSpec((tc, RET_W), lambda i: (blk_a(i) % nt, 0)),
        const((D_MODEL, D_MODEL)),
        const((DN_HEADS, DN_DK, DN_DK)), const((8, 3 * DN_W)), const((1, LRU_W)), const((8, LRU_W)),
        const((RET_W, RET_W)),
    ]
    out_shape = [
        jax.ShapeDtypeStruct((batch * seq, D_MODEL), F32),
        jax.ShapeDtypeStruct((batch, DN_HEADS, DN_DK, DN_DK), F32),
        jax.ShapeDtypeStruct((batch, 8, 3 * DN_W), F32),
        jax.ShapeDtypeStruct((batch, 1, LRU_W), F32),
        jax.ShapeDtypeStruct((batch, 8, LRU_W), F32),
        jax.ShapeDtypeStruct((batch, RET_W, RET_W), F32),
        jax.ShapeDtypeStruct((batch, RET_HEADS, RET_DK, RET_DK), F32),
    ]
    seq_a = lambda i: blk_a(i) // nt
    seq_b = lambda i: blk_b(i) // nt
    out_specs = [
        pl.BlockSpec((tc, D_MODEL), lambda i: (blk_b(i), 0)),
        pl.BlockSpec((1, DN_HEADS, DN_DK, DN_DK), lambda i: (seq_b(i), 0, 0, 0)),
        pl.BlockSpec((1, 8, 3 * DN_W), lambda i: (seq_a(i), 0, 0)),
        pl.BlockSpec((1, 1, LRU_W), lambda i: (seq_a(i), 0, 0)),
        pl.BlockSpec((1, 8, LRU_W), lambda i: (seq_a(i), 0, 0)),
        pl.BlockSpec((1, RET_W, RET_W), lambda i: (seq_b(i), 0, 0)),
        pl.BlockSpec((1, RET_HEADS, RET_DK, RET_DK), lambda i: (seq_b(i), 0, 0, 0)),
    ]
    scratch = [
        pltpu.VMEM((DN_HEADS, DN_DK, DN_DK), F32),
        pltpu.VMEM((RET_W, RET_W), F32),
        pltpu.VMEM((1, LRU_W), F32),
        pltpu.VMEM((tc + 8, 3 * DN_W), F32),
        pltpu.VMEM((tc + 8, LRU_W), F32),
        pltpu.VMEM((DN_HEADS, tc, DN_DK), F32),
        pltpu.VMEM((DN_HEADS, tc, DN_DK), F32),
        pltpu.VMEM((DN_HEADS, tc, DN_DK), F32),
        pltpu.VMEM((tc, 8), F32),
        pltpu.VMEM((nc, 1, hc), F32),
        pltpu.VMEM((tc, DN_W), F32),
        pltpu.VMEM((tc, RET_W), F32),
        pltpu.VMEM((tc, D_MODEL), BF16),
        pltpu.VMEM((nc, hc, DN_DK), F32),
        pltpu.VMEM((nc, DN_HEADS, 2 * ck, DN_DK), BF16),
        pltpu.VMEM((nc, ng, gc, gc), BF16),
        pltpu.VMEM((nc, DN_HEADS, ck, DN_DK), BF16),
        pltpu.VMEM((nc, 8, DN_DK), F32),
        pltpu.VMEM((nc, ck, RET_W), BF16),
        pltpu.VMEM((nc, ck, RET_W), F32),
        pltpu.VMEM((nc, RET_W, RET_W), F32),
        pltpu.VMEM((tc, DN_W), F32),
        pltpu.VMEM((tc, RET_W), F32),
        pltpu.VMEM((tc, D_MODEL), F32),
        pltpu.VMEM((tc, LRU_W), BF16),
    ]
    args = [x, lw["norm_mix"], lw["wqkv"], lw["wabt"], lw["wrest"], lw["dn_conv_w"],
            lw["alog_c"], lw["dtb_c"], lw["dn_norm_w"],
            lw["lru_conv_w"], lw["lru_conv_b"], lw["wgate"], lw["bgate"], lw["lam"], cos, sin,
            lw["w_out"], *init]
    return pl.pallas_call(
        functools.partial(_mix_pipe_body, tc=tc, ck=ck, nt=nt, n_blocks=n_blocks),
        grid=(n_blocks + 1,),
        in_specs=in_specs,
        out_specs=out_specs,
        out_shape=out_shape,
        scratch_shapes=scratch,
        compiler_params=pltpu.CompilerParams(
            dimension_semantics=("arbitrary",), vmem_limit_bytes=VMEM_LIMIT_BYTES),
        name=f"mix_pipe_c{ck}",
    )(*args)


def _mix_step_body(*refs, bb, layer):
    (x_ref, nw_ref, wqkv_ref, wab_ref, wrest_ref, dncw_ref, alog_r_ref, dtb_r_ref,
     dnnw_ref, lcw_ref, lcb_ref, wgate_ref, bgate_ref, lam_ref, cos_ref, sin_ref,
     sdn_ref, dnc_ref, h_ref, lrc_ref) = refs[:20]
    refs = refs[20:]
    if layer > 0:
        prev_sdn_ref = refs[0]
        refs = refs[1:]
    (omix_o_ref, retin_o_ref, sdn_o_ref, dnc_o_ref, h_o_ref, lrc_o_ref,
     q_s, k_s, v_s, eg_s, beta_s, odn_s) = refs
    if layer > 0:
        sdn_o_ref[0:layer] = prev_sdn_ref[...]
    x = x_ref[...]
    u = _rmsnorm(x, nw_ref[...]).astype(BF16)

    qkv_pre = jnp.dot(u, wqkv_ref[...], preferred_element_type=F32)
    cw = dncw_ref[...]
    conv = (dnc_ref[0] * cw[0:1, :] + dnc_ref[1] * cw[1:2, :] + dnc_ref[2] * cw[2:3, :]
            + qkv_pre * cw[3:4, :])
    dnc_o_ref[0] = dnc_ref[1]
    dnc_o_ref[1] = dnc_ref[2]
    dnc_o_ref[2] = qkv_pre
    qkv = _silu(conv)
    for h in range(DN_HEADS):
        sl = slice(h * DN_DK, (h + 1) * DN_DK)
        qh = qkv[:, sl]
        kh = qkv[:, DN_W + h * DN_DK:DN_W + (h + 1) * DN_DK]
        q_s[:, sl] = qh * lax.rsqrt(jnp.sum(qh * qh, axis=-1, keepdims=True) + EPS) * (DN_DK ** -0.5)
        k_s[:, sl] = kh * lax.rsqrt(jnp.sum(kh * kh, axis=-1, keepdims=True) + EPS)
    v_s[...] = qkv[:, 2 * DN_W:]
    ab = jnp.dot(u, wab_ref[...], preferred_element_type=F32)
    eg_s[...] = jnp.exp(-jnp.exp(alog_r_ref[...]) * _softplus(ab[:, :128] + dtb_r_ref[...]))
    beta_s[...] = jax.nn.sigmoid(ab[:, 128:])

    rest = jnp.dot(u, wrest_ref[...], preferred_element_type=F32)
    dn_z = rest[:, 0:512]
    lru_x = rest[:, 512:768]
    lru_y = rest[:, 768:1024]
    ret_g = rest[:, 1792:2048]

    lw = lcw_ref[...]
    xl = (lrc_ref[0] * lw[0:1, :] + lrc_ref[1] * lw[1:2, :] + lrc_ref[2] * lw[2:3, :]
          + lru_x * lw[3:4, :] + lcb_ref[...])
    lrc_o_ref[0] = lrc_ref[1]
    lrc_o_ref[1] = lrc_ref[2]
    lrc_o_ref[2] = lru_x
    a, b = _lru_coeffs(xl, wgate_ref[...], bgate_ref[...], lam_ref[...])
    h_new = a * h_ref[...] + b
    h_o_ref[...] = h_new
    o_lru = h_new * jax.nn.gelu(lru_y)

    cos = cos_ref[...]
    sin = sin_ref[...]
    retin_o_ref[:, 0:256] = _rope(rest[:, 1024:1280], cos, sin)
    retin_o_ref[:, 256:512] = _rope(rest[:, 1280:1536], cos, sin) * (RET_DK ** -0.5)
    retin_o_ref[:, 512:768] = rest[:, 1536:1792]
    retin_o_ref[:, 768:1024] = ret_g

    def per_tile(i8, carry):
        r0 = pl.multiple_of(i8 * 8, 8)
        eg_t = eg_s[pl.ds(r0, 8), :]
        beta_t = beta_s[pl.ds(r0, 8), :]
        for h in range(DN_HEADS):
            sl = slice(h * DN_DK, (h + 1) * DN_DK)
            kt = k_s[pl.ds(r0, 8), sl].T
            qt = q_s[pl.ds(r0, 8), sl].T
            v_t = v_s[pl.ds(r0, 8), sl]
            rows = []
            for j in range(8):
                kcol = kt[:, j:j + 1]
                qcol = qt[:, j:j + 1]
                s0 = sdn_ref[r0 + j, h]
                eg = eg_t[j:j + 1, h:h + 1]
                ks = jnp.sum(s0 * kcol, axis=0, keepdims=True)
                qs = jnp.sum(s0 * qcol, axis=0, keepdims=True)
                qk = jnp.sum(qcol * kcol, axis=0, keepdims=True)
                uu = beta_t[j:j + 1, h:h + 1] * (v_t[j:j + 1, :] - eg * ks)
                rows.append(eg * qs + qk * uu)
                sdn_o_ref[layer, r0 + j, h] = eg * s0 + kcol * uu
            odn_s[pl.ds(r0, 8), sl] = jnp.concatenate(rows, axis=0)
        return carry

    lax.fori_loop(0, bb // 8, per_tile, 0)

    nw = dnnw_ref[...]
    for h in range(DN_HEADS):
        sl = slice(h * DN_DK, (h + 1) * DN_DK)
        oh = odn_s[:, sl]
        oh = oh * lax.rsqrt(jnp.mean(oh * oh, axis=-1, keepdims=True) + EPS) * nw
        omix_o_ref[:, sl] = (oh * _silu(dn_z[:, sl])).astype(BF16)
    omix_o_ref[:, DN_W:DN_W + LRU_W] = o_lru.astype(BF16)


def _mix_step(x, lw, cos, sin, layer, sdn, dnc, hl, lrc, prev_sdn, *, bb):
    n = x.shape[0]
    const = lambda shape: pl.BlockSpec(shape, lambda i: (0,) * len(shape))
    in_specs = [
        pl.BlockSpec((bb, D_MODEL), lambda i: (i, 0)),
        const((1, D_MODEL)), const((D_MODEL, 3 * DN_W)), const((D_MODEL, 256)), const((D_MODEL, 2048)),
        const((CONV_W, 3 * DN_W)), const((1, 128)), const((1, 128)), const((1, DN_DK)),
        const((CONV_W, LRU_W)), const((1, LRU_W)), const((LRU_W, 2 * LRU_W)), const((1, 2 * LRU_W)),
        const((1, LRU_W)), const((1, RET_W)), const((1, RET_W)),
        pl.BlockSpec((None, bb, DN_HEADS, DN_DK, DN_DK), lambda i: (layer, i, 0, 0, 0)),
        pl.BlockSpec((CONV_W - 1, bb, 3 * DN_W), lambda i: (0, i, 0)),
        pl.BlockSpec((None, bb, LRU_W), lambda i: (layer, i, 0)),
        pl.BlockSpec((CONV_W - 1, bb, LRU_W), lambda i: (0, i, 0)),
    ]
    if layer > 0:
        in_specs.append(pl.BlockSpec((layer, bb, DN_HEADS, DN_DK, DN_DK), lambda i: (0, i, 0, 0, 0)))
    out_shape = [
        jax.ShapeDtypeStruct((n, DN_W + LRU_W), BF16),
        jax.ShapeDtypeStruct((n, 4 * RET_W), F32),
        jax.ShapeDtypeStruct((layer + 1, n, DN_HEADS, DN_DK, DN_DK), F32),
        jax.ShapeDtypeStruct((CONV_W - 1, n, 3 * DN_W), F32),
        jax.ShapeDtypeStruct((n, LRU_W), F32),
        jax.ShapeDtypeStruct((CONV_W - 1, n, LRU_W), F32),
    ]
    out_specs = [
        pl.BlockSpec((bb, DN_W + LRU_W), lambda i: (i, 0)),
        pl.BlockSpec((bb, 4 * RET_W), lambda i: (i, 0)),
        pl.BlockSpec((layer + 1, bb, DN_HEADS, DN_DK, DN_DK), lambda i: (0, i, 0, 0, 0)),
        pl.BlockSpec((CONV_W - 1, bb, 3 * DN_W), lambda i: (0, i, 0)),
        pl.BlockSpec((bb, LRU_W), lambda i: (i, 0)),
        pl.BlockSpec((CONV_W - 1, bb, LRU_W), lambda i: (0, i, 0)),
    ]
    scratch = [
        pltpu.VMEM((bb, DN_W), F32), pltpu.VMEM((bb, DN_W), F32), pltpu.VMEM((bb, DN_W), F32),
        pltpu.VMEM((bb, 128), F32), pltpu.VMEM((bb, 128), F32), pltpu.VMEM((bb, DN_W), F32),
    ]
    args = [x, lw["norm_mix"], lw["wqkv"], lw["wab"], lw["wrest"], lw["dn_conv_w"], lw["alog_r"],
            lw["dtb_r"], lw["dn_norm_w"], lw["lru_conv_w"], lw["lru_conv_b"], lw["wgate"], lw["bgate"],
            lw["lam"], cos, sin, sdn, dnc, hl, lrc]
    if layer > 0:
        args.append(prev_sdn)
    return pl.pallas_call(
        functools.partial(_mix_step_body, bb=bb, layer=layer),
        grid=(n // bb,),
        in_specs=in_specs,
        out_specs=out_specs,
        out_shape=out_shape,
        scratch_shapes=scratch,
        compiler_params=pltpu.CompilerParams(
            dimension_semantics=("arbitrary",), vmem_limit_bytes=VMEM_LIMIT_BYTES),
        name="mix_step",
    )(*args)


def _ret_step_body(*refs, layer):
    x_ref, omix_ref, retin_ref, wout_ref, s_ref = refs[:5]
    refs = refs[5:]
    if layer > 0:
        prev_ref = refs[0]
        refs = refs[1:]
    y_ref, s_o_ref, qt_s, kt_s, vt_s, ot_s = refs
    h = pl.program_id(0)
    if layer > 0:
        s_o_ref[0:layer] = prev_ref[...]

    @pl.when(h == 0)
    def _():
        qt_s[...] = retin_ref[:, 0:RET_W].T
        kt_s[...] = retin_ref[:, RET_W:2 * RET_W].T
        vt_s[...] = retin_ref[:, 2 * RET_W:3 * RET_W].T

    rows = pl.ds(pl.multiple_of(h * RET_DK, RET_DK), RET_DK)
    q = qt_s[rows, :]
    k = kt_s[rows, :]
    v = vt_s[rows, :]
    gamma = jnp.float32(math.exp(LOG_GAMMA[0]))
    for i in range(1, RET_HEADS):
        gamma = jnp.where(h == i, jnp.float32(math.exp(LOG_GAMMA[i])), gamma)
    acc = jnp.zeros_like(v)
    for d in range(RET_DK):
        s_d = s_ref[d]
        acc = acc + s_d * q[d:d + 1, :]
        s_o_ref[layer, 0, d] = gamma * s_d + k[d:d + 1, :] * v
    ot_s[rows, :] = gamma * acc + jnp.sum(q * k, axis=0, keepdims=True) * v

    @pl.when(h == pl.num_programs(0) - 1)
    def _():
        o_ret = _ret_norm_gate(ot_s[...].T, retin_ref[:, 3 * RET_W:], _head_avg_matrix())
        omix = jnp.concatenate([omix_ref[...], o_ret.astype(BF16)], axis=1)
        y_ref[...] = x_ref[...] + jnp.dot(omix, wout_ref[...], preferred_element_type=F32)


def _ret_step(x, omix_part, retin, w_out, sret_t, layer, prev, *, n):
    const = lambda shape: pl.BlockSpec(shape, lambda h: (0,) * len(shape))
    in_specs = [
        const((n, D_MODEL)), const((n, DN_W + LRU_W)), const((n, 4 * RET_W)), const((D_MODEL, D_MODEL)),
        pl.BlockSpec((None, None, RET_DK, RET_DK, n), lambda h: (layer, h, 0, 0, 0)),
    ]
    args = [x, omix_part, retin, w_out, sret_t]
    if layer > 0:
        in_specs.append(pl.BlockSpec((layer, 1, RET_DK, RET_DK, n), lambda h: (0, h, 0, 0, 0)))
        args.append(prev)
    return pl.pallas_call(
        functools.partial(_ret_step_body, layer=layer),
        grid=(RET_HEADS,),
        in_specs=in_specs,
        out_specs=[const((n, D_MODEL)),
                   pl.BlockSpec((layer + 1, 1, RET_DK, RET_DK, n), lambda h: (0, h, 0, 0, 0))],
        out_shape=[jax.ShapeDtypeStruct((n, D_MODEL), F32),
                   jax.ShapeDtypeStruct((layer + 1, RET_HEADS, RET_DK, RET_DK, n), F32)],
        scratch_shapes=[pltpu.VMEM((RET_W, n), F32)] * 4,
        compiler_params=pltpu.CompilerParams(
            dimension_semantics=("arbitrary",), vmem_limit_bytes=VMEM_LIMIT_BYTES),
        name="ret_step",
    )(*args)


def _rope_tables(pos):
    half = RET_DK // 2
    inv = ROPE_BASE ** (-jnp.arange(half, dtype=F32) / half)
    ang = pos.astype(F32)[:, None] * inv[None, :]
    cos = jnp.cos(ang)
    sin = jnp.sin(ang)
    cos_full = jnp.tile(jnp.concatenate([cos, cos], axis=-1), (1, RET_HEADS))
    sin_signed = jnp.tile(jnp.concatenate([-sin, sin], axis=-1), (1, RET_HEADS))
    return cos_full, sin_signed


def _block_diag(w):
    n, d, e = w.shape
    eye = jnp.eye(n, dtype=w.dtype)
    return (eye[:, None, :, None] * w[:, :, None, :]).reshape(n * d, n * e)


def _layer_weights(l, norm_mix, w_in, dn_conv_w, dn_a_log, dn_dt_bias, dn_norm_w, lru_conv_w, lru_conv_b,
                   lru_wa, lru_ba, lru_wx, lru_bx, lru_lambda, w_out):
    wl = w_in[l]
    o_a = 3 * DN_W
    wa = wl[:, o_a:o_a + DN_HEADS]
    wb = wl[:, o_a + DN_HEADS:o_a + 2 * DN_HEADS]
    pad_cols = lambda w: jnp.pad(w, ((0, 0), (0, 128 - DN_HEADS)))
    pad_lane = lambda v: jnp.pad(v, (0, 128 - DN_HEADS)).reshape(1, 128)
    pad_sub = lambda v: jnp.pad(v, (0, 8 - DN_HEADS)).reshape(8, 1)
    return {
        "norm_mix": norm_mix[l].reshape(1, D_MODEL),
        "wqkv": wl[:, :o_a].astype(BF16),
        "wab": jnp.concatenate([pad_cols(wa), pad_cols(wb)], axis=1).astype(BF16),
        "wabt": jnp.concatenate([wa, wb], axis=1).T.astype(BF16),
        "wrest": wl[:, o_a + 2 * DN_HEADS:].astype(BF16),
        "dn_conv_w": dn_conv_w[l],
        "alog_r": pad_lane(dn_a_log[l]), "dtb_r": pad_lane(dn_dt_bias[l]),
        "alog_c": pad_sub(dn_a_log[l]), "dtb_c": pad_sub(dn_dt_bias[l]),
        "dn_norm_w": dn_norm_w[l].reshape(1, DN_DK),
        "lru_conv_w": lru_conv_w[l], "lru_conv_b": lru_conv_b[l].reshape(1, LRU_W),
        "wgate": jnp.concatenate([_block_diag(lru_wa[l]), _block_diag(lru_wx[l])], axis=1).astype(BF16),
        "bgate": jnp.concatenate([lru_ba[l], lru_bx[l]]).reshape(1, 2 * LRU_W),
        "lam": lru_lambda[l].reshape(1, LRU_W),
        "w_out": w_out[l].astype(BF16),
    }


def kernel(x_prompt, x_sample, state_dn, state_dn_conv, state_lru, state_lru_conv, state_ret, meta_tokens, norm_ffn1, w_ffn1_in, w_ffn1_out, norm_mix, w_in, dn_conv_w, dn_a_log, dn_dt_bias, dn_norm_w, lru_conv_w, lru_conv_b, lru_wa, lru_ba, lru_wx, lru_bx, lru_lambda, w_out, norm_ffn2, w_ffn2_in, w_ffn2_out, norm_final):
    batch, seq, _ = x_prompt.shape
    n_dec = x_sample.shape[0]
    depth = w_in.shape[0]
    f1_in, f1_out = w_ffn1_in.astype(BF16), w_ffn1_out.astype(BF16)
    f2_in, f2_out = w_ffn2_in.astype(BF16), w_ffn2_out.astype(BF16)
    lws = [_layer_weights(l, norm_mix, w_in, dn_conv_w, dn_a_log, dn_dt_bias, dn_norm_w, lru_conv_w,
                          lru_conv_b, lru_wa, lru_ba, lru_wx, lru_bx, lru_lambda, w_out)
           for l in range(depth)]
    cos_m, sin_m = _rope_tables(jnp.arange(N_META))
    cos_p, sin_p = _rope_tables(N_META + jnp.arange(seq))
    cos_s, sin_s = _rope_tables(PAST_LEN + jnp.arange(1))

    n_small = n_dec + N_META
    xs = jnp.concatenate([x_sample[:, 0, :], meta_tokens.astype(F32)], axis=0)
    zero_init = (jnp.zeros((DN_HEADS, DN_DK, DN_DK), F32), jnp.zeros((8, 3 * DN_W), F32),
                 jnp.zeros((1, LRU_W), F32), jnp.zeros((8, LRU_W), F32), jnp.zeros((RET_W, RET_W), F32))
    new_s, meta_state = [], []
    sdn_all = sret_all = None
    sret_t = jnp.transpose(state_ret, (0, 2, 3, 4, 1))
    for l in range(depth):
        last = l == depth - 1
        xs = _ffn(xs, norm_ffn1[l], f1_in, f1_out, l, tm=n_small, nf=1)
        omix_part, retin, sdn_all, dnc, hl, lrc = _mix_step(
            xs[:n_dec], lws[l], cos_s, sin_s, l, state_dn, jnp.swapaxes(state_dn_conv[l], 0, 1),
            state_lru, jnp.swapaxes(state_lru_conv[l], 0, 1), sdn_all, bb=8)
        ys, sret_all = _ret_step(xs[:n_dec], omix_part, retin, lws[l]["w_out"], sret_t, l, sret_all,
                                 n=n_dec)
        ym, m_sdn, m_dnt, m_h, m_lrt, m_sretbd, _ = _mix_pipe(
            xs[n_dec:], lws[l], cos_m, sin_m, zero_init, batch=1, seq=N_META, tc=N_META, ck=N_META)
        new_s.append((jnp.swapaxes(dnc, 0, 1), hl, jnp.swapaxes(lrc, 0, 1)))
        meta_state.append((m_sdn[0], m_dnt[0], m_h[0], m_lrt[0], m_sretbd[0]))
        xs = jnp.concatenate([ys, ym], axis=0)
        xs = _ffn(xs, norm_ffn2[l], f2_in, f2_out, l, norm_final if last else None, tm=n_small, nf=1)
    y_sample = xs[:n_dec].reshape(n_dec, 1, D_MODEL)

    xp = x_prompt.reshape(batch * seq, D_MODEL)
    new_p = []
    for l in range(depth):
        last = l == depth - 1
        xp = _ffn(xp, norm_ffn1[l], f1_in, f1_out, l, tm=512, nf=1)
        xp, sdn, dnt, hl, lrt, _, sret = _mix_pipe(
            xp, lws[l], cos_p, sin_p, meta_state[l], batch=batch, seq=seq, tc=256, ck=CHUNK)
        new_p.append((sdn, dnt[:, 5:8], hl[:, 0], lrt[:, 5:8], sret))
        xp = _ffn(xp, norm_ffn2[l], f2_in, f2_out, l, norm_final if last else None, tm=512, nf=1)
    y_prompt = xp.reshape(batch, seq, D_MODEL)

    outs_p = [jnp.stack([s[j] for s in new_p]) for j in range(5)]
    dnc_s, lru_s, lrc_s = [jnp.stack([s[j] for s in new_s]) for j in range(3)]
    sret_s = jnp.transpose(sret_all, (0, 4, 1, 2, 3))
    return (y_prompt, y_sample, *outs_p, sdn_all, dnc_s, lru_s, lrc_s, sret_s)
```

```python
import functools
import math

import jax
import jax.numpy as jnp
from jax import lax
from jax.experimental import pallas as pl
from jax.experimental.pallas import tpu as pltpu

F32 = jnp.float32
BF16 = jnp.bfloat16

D_MODEL = 1024
N_META = 16
PAST_LEN = 16384
DN_HEADS = 4
DN_DK = 128
DN_W = 512
LRU_W = 256
LRU_BLOCKS = 4
LRU_C = 8.0
RET_HEADS = 4
RET_DK = 64
RET_W = 256
CONV_W = 4
CHUNK = 64
D_FF = 2816
ROPE_BASE = 10000.0
EPS = 1e-6
SUB = 16
LOG_GAMMA = tuple(math.log1p(-2.0 ** (-5.0 - h)) for h in range(RET_HEADS))

VMEM_LIMIT_BYTES = 56 * 1024 * 1024


def _mm(a, b):
    return jnp.dot(a.astype(BF16), b.astype(BF16), preferred_element_type=F32)


def _mm_nt(a, b):
    return lax.dot_general(a.astype(BF16), b.astype(BF16), (((1,), (1,)), ((), ())),
                           preferred_element_type=F32)


def _mm_tn(a, b):
    return lax.dot_general(a.astype(BF16), b.astype(BF16), (((0,), (0,)), ((), ())),
                           preferred_element_type=F32)


def _split3(x):
    hi = x.astype(BF16)
    r = x - hi.astype(F32)
    mid = r.astype(BF16)
    lo = (r - mid.astype(F32)).astype(BF16)
    return hi, mid, lo


def _mm_01_left(m01, x):
    m = m01.astype(BF16)
    return sum(jnp.dot(m, p, preferred_element_type=F32) for p in _split3(x))


def _mm_01_right(x, m01):
    m = m01.astype(BF16)
    return sum(jnp.dot(p, m, preferred_element_type=F32) for p in _split3(x))


def _mm_split(x, m_bf16):
    hi = x.astype(BF16)
    lo = (x - hi.astype(F32)).astype(BF16)
    return (jnp.dot(hi, m_bf16, preferred_element_type=F32)
            + jnp.dot(lo, m_bf16, preferred_element_type=F32))


def _rmsnorm(x, w):
    return x * lax.rsqrt(jnp.mean(x * x, axis=-1, keepdims=True) + EPS) * w


def _sigmoid(x):
    return 0.5 + 0.5 * jnp.tanh(0.5 * x)


def _silu(x):
    return x * _sigmoid(x)


def _causal_conv(x, tail, w):
    tc, n = x.shape
    nt8 = tc // 8
    first = _iota((nt8, 8, n), 1) == 0
    xm1, xm2, xm3 = tail[7:8], tail[6:7], tail[5:6]

    def shift1(s, carry):
        r = pltpu.roll(s.reshape(nt8, 8, n), 1, 1)
        before = jnp.concatenate([jnp.broadcast_to(carry, (1, 8, n)), r[:-1]], axis=0)
        return jnp.where(first, before, r).reshape(tc, n)

    s = shift1(x * w[0:1], xm1 * w[0:1])
    s = shift1(s + x * w[1:2], xm2 * w[0:1] + xm1 * w[1:2])
    s = shift1(s + x * w[2:3], xm3 * w[0:1] + xm2 * w[1:2] + xm1 * w[2:3])
    return s + x * w[3:4]


def _softplus(x):
    return jnp.maximum(x, 0.0) + jnp.log1p(jnp.exp(-jnp.abs(x)))


def _iota(shape, dim):
    return lax.broadcasted_iota(jnp.int32, shape, dim)


def _lane_log_gamma(shape, dim, width):
    head = _iota(shape, dim) // width
    out = jnp.full(shape, LOG_GAMMA[0], F32)
    for h in range(1, RET_HEADS):
        out = jnp.where(head == h, LOG_GAMMA[h], out)
    return out


def _rope(x, cos, sin_signed):
    n = x.shape[1]
    half = RET_DK // 2
    first = (_iota(x.shape, 1) % RET_DK) < half
    swapped = jnp.where(first, pltpu.roll(x, n - half, 1), pltpu.roll(x, half, 1))
    return x * cos + swapped * sin_signed


def _ret_norm_gate(o, gate, avg_bf16):
    mu = _mm_split(o, avg_bf16)
    d = o - mu
    var = _mm_split(d * d, avg_bf16)
    return d * lax.rsqrt(var + EPS) * _silu(gate)


def _head_avg_matrix():
    r = _iota((RET_W, RET_W), 0) // RET_DK
    c = _iota((RET_W, RET_W), 1) // RET_DK
    return jnp.where(r == c, 1.0 / RET_DK, 0.0).astype(BF16)


def _lru_coeffs(xl, wgate, bgate, lam):
    gates = _mm(xl, wgate) + bgate
    r = _sigmoid(gates[:, :LRU_W])
    i = _sigmoid(gates[:, LRU_W:])
    log_a = -LRU_C * r * _softplus(-lam)
    a = jnp.exp(log_a)
    b = jnp.sqrt(jnp.maximum(-jnp.tanh(log_a) * (a * a + 1.0), 0.0)) * (i * xl)
    return a, b


def _ffn_body(*refs, final, nf):
    if final:
        x_ref, nw_ref, wg_ref, wu_ref, wo_ref, fw_ref, o_ref = refs
    else:
        x_ref, nw_ref, wg_ref, wu_ref, wo_ref, o_ref = refs
    tf = D_FF // nf
    x = x_ref[...]
    u = _rmsnorm(x, nw_ref[...]).astype(BF16)
    acc = None
    for j in range(nf):
        gate = jnp.dot(u, wg_ref[:, j * tf:(j + 1) * tf], preferred_element_type=F32)
        up = jnp.dot(u, wu_ref[:, j * tf:(j + 1) * tf], preferred_element_type=F32)
        h = (_silu(gate) * up).astype(BF16)
        part = jnp.dot(h, wo_ref[j * tf:(j + 1) * tf, :], preferred_element_type=F32)
        acc = part if acc is None else acc + part
    y = x + 0.5 * acc
    if final:
        y = _rmsnorm(y, fw_ref[...])
    o_ref[...] = y


def _ffn(x, norm_w, wg, wu, wo, final_w=None, *, tm, nf):
    n = x.shape[0]
    final = final_w is not None
    resident = pl.Buffered(1)
    in_specs = [
        pl.BlockSpec((tm, D_MODEL), lambda i: (i, 0)),
        pl.BlockSpec((1, D_MODEL), lambda i: (0, 0)),
        pl.BlockSpec((D_MODEL, D_FF), lambda i: (0, 0), pipeline_mode=resident),
        pl.BlockSpec((D_MODEL, D_FF), lambda i: (0, 0), pipeline_mode=resident),
        pl.BlockSpec((D_FF, D_MODEL), lambda i: (0, 0), pipeline_mode=resident),
    ]
    args = [x, norm_w.reshape(1, D_MODEL), wg, wu, wo]
    if final:
        in_specs.append(pl.BlockSpec((1, D_MODEL), lambda i: (0, 0)))
        args.append(final_w.reshape(1, D_MODEL))
    return pl.pallas_call(
        functools.partial(_ffn_body, final=final, nf=nf),
        grid=(n // tm,),
        in_specs=in_specs,
        out_specs=pl.BlockSpec((tm, D_MODEL), lambda i: (i, 0)),
        out_shape=jax.ShapeDtypeStruct((n, D_MODEL), F32),
        compiler_params=pltpu.CompilerParams(
            dimension_semantics=("arbitrary",), vmem_limit_bytes=VMEM_LIMIT_BYTES),
        name="ffn_final" if final else "ffn",
    )(*args)


def _ffn_cast_body(*refs, final):
    if final:
        x_ref, nw_ref, wg_ref, wu_ref, wo_ref, fw_ref, o_ref, wg_o, wu_o, wo_o, u_s, acc_s = refs
    else:
        x_ref, nw_ref, wg_ref, wu_ref, wo_ref, o_ref, wg_o, wu_o, wo_o, u_s, acc_s = refs
    j = pl.program_id(0)

    @pl.when(j == 0)
    def _():
        u_s[...] = _rmsnorm(x_ref[...], nw_ref[...]).astype(BF16)
        acc_s[...] = jnp.zeros_like(acc_s)

    wg = wg_ref[...].astype(BF16)
    wu = wu_ref[...].astype(BF16)
    wo = wo_ref[...].astype(BF16)
    wg_o[...] = wg
    wu_o[...] = wu
    wo_o[...] = wo
    u = u_s[...]
    gate = jnp.dot(u, wg, preferred_element_type=F32)
    up = jnp.dot(u, wu, preferred_element_type=F32)
    acc_s[...] += jnp.dot((_silu(gate) * up).astype(BF16), wo, preferred_element_type=F32)

    @pl.when(j == pl.num_programs(0) - 1)
    def _():
        y = x_ref[...] + 0.5 * acc_s[...]
        if final:
            y = _rmsnorm(y, fw_ref[...])
        o_ref[...] = y


def _ffn_cast(x, norm_w, w_in, w_out, layer, final_w=None, *, tf):
    n = x.shape[0]
    nf = D_FF // tf
    final = final_w is not None
    in_specs = [
        pl.BlockSpec((n, D_MODEL), lambda j: (0, 0)),
        pl.BlockSpec((1, D_MODEL), lambda j: (0, 0)),
        pl.BlockSpec((None, D_MODEL, tf), lambda j: (layer, 0, j)),
        pl.BlockSpec((None, D_MODEL, tf), lambda j: (layer, 0, j + nf)),
        pl.BlockSpec((None, tf, D_MODEL), lambda j: (layer, j, 0)),
    ]
    args = [x, norm_w.reshape(1, D_MODEL), w_in, w_in, w_out]
    if final:
        in_specs.append(pl.BlockSpec((1, D_MODEL), lambda j: (0, 0)))
        args.append(final_w.reshape(1, D_MODEL))
    return pl.pallas_call(
        functools.partial(_ffn_cast_body, final=final),
        grid=(nf,),
        in_specs=in_specs,
        out_specs=[pl.BlockSpec((n, D_MODEL), lambda j: (0, 0)),
                   pl.BlockSpec((D_MODEL, tf), lambda j: (0, j)),
                   pl.BlockSpec((D_MODEL, tf), lambda j: (0, j)),
                   pl.BlockSpec((tf, D_MODEL), lambda j: (j, 0))],
        out_shape=[jax.ShapeDtypeStruct((n, D_MODEL), F32),
                   jax.ShapeDtypeStruct((D_MODEL, D_FF), BF16),
                   jax.ShapeDtypeStruct((D_MODEL, D_FF), BF16),
                   jax.ShapeDtypeStruct((D_FF, D_MODEL), BF16)],
        scratch_shapes=[pltpu.VMEM((n, D_MODEL), BF16), pltpu.VMEM((n, D_MODEL), F32)],
        compiler_params=pltpu.CompilerParams(
            dimension_semantics=("arbitrary",), vmem_limit_bytes=VMEM_LIMIT_BYTES),
        name="ffn_cast_final" if final else "ffn_cast",
    )(*args)


def _heads_per_group(ck):
    return max(1, min(DN_HEADS, 128 // ck))


def _tri_inv_steps(a, n_sub, out):
    m = a[0].shape[0]
    r = _iota((m, m), 0)
    c = _iota((m, m), 1)
    eye = jnp.where(r == c, 1.0, 0.0).astype(F32)
    diag = (r // SUB) == (c // SUB)
    x = [jnp.where(diag, ai, 0.0) for ai in a]
    off = [ai - xi for ai, xi in zip(a, x)]
    p = [eye - xi for xi in x]
    for _ in range(3):
        x = [_mm(xi, xi) for xi in x]
        yield
        p = [pi + _mm(pi, xi) for pi, xi in zip(p, x)]
        yield
    if n_sub == 1:
        out.extend(p)
        return
    n = [_mm(pi, oi) for pi, oi in zip(p, off)]
    yield
    n2 = [_mm(ni, ni) for ni in n]
    yield
    rr = [eye - ni + n2i - _mm(ni, n2i) for ni, n2i in zip(n, n2)]
    yield
    out.extend(_mm(ri, pi) for ri, pi in zip(rr, p))
    yield


def _mix_pipe_body(x_ref, nw_ref, wqkv_ref, wabt_ref, wrest_ref, dncw_ref,
                   alog_c_ref, dtb_c_ref, dnnw_ref,
                   lcw_ref, lcb_ref, wgate_ref, bgate_ref, lam_ref, cos_ref, sin_ref, wout_ref,
                   sdn0_ref, dnt0_ref, h0_ref, lrt0_ref, sret0_ref,
                   y_ref, sdn_o_ref, dnt_o_ref, h_o_ref, lrt_o_ref, sretbd_o_ref, sret_o_ref,
                   sdn, sret, hst, qkvbuf, lrubuf, q_s, k_s, v_s, gb_s, grow_s, odn_s, oret_s, omix_s,
                   h_uv, h_wq, h_pm, h_kd, h_egl, h_rq, h_oi, h_su, h_z, h_rg, h_x, h_lru,
                   *, tc, ck, nt, n_blocks):
    i = pl.program_id(0)
    t_a = i % nt
    t_b = (i + nt - 1) % nt
    nc = tc // ck
    hc = DN_HEADS * ck
    n_sub = ck // SUB
    chunks = range(nc)
    rows = [slice(c * ck, (c + 1) * ck) for c in chunks]
    heads = [(h * ck, (h + 1) * ck) for h in range(DN_HEADS)]
    hg = _heads_per_group(ck)
    ng = DN_HEADS // hg
    gc = hg * ck
    probs = [(c, g) for c in chunks for g in range(ng)]

    @pl.when(i == 0)
    def _():
        for ref in (h_uv, h_wq, h_pm, h_kd, h_egl, h_rq, h_oi, h_su, h_z, h_rg, h_x, h_lru):
            ref[...] = jnp.zeros_like(ref)

    @pl.when(t_a == 0)
    def _():
        hst[...] = h0_ref[...]
        qkvbuf[0:8, :] = dnt0_ref[...]
        lrubuf[0:8, :] = lrt0_ref[...]

    @pl.when((t_b == 0) | (i == 0))
    def _():
        sdn[...] = sdn0_ref[...]
        sret[...] = sret0_ref[...]

    sr = _iota((gc, gc), 0)
    sc = _iota((gc, gc), 1)
    strict = ((sr // ck) == (sc // ck)) & (sr > sc)
    eye_g = jnp.where(sr == sc, 1.0, 0.0).astype(F32)
    lg_rows = _lane_log_gamma((hc, ck), 0, ck)
    tpos = _iota((hc, ck), 0) % ck
    ipos = _iota((hc, ck), 1)
    causal = tpos >= ipos
    ret_dec = jnp.where(causal, jnp.exp(jnp.where(causal, (tpos - ipos).astype(F32) * lg_rows, 0.0)), 0.0)
    head_sel = (_iota((hc, RET_W), 0) // ck) == (_iota((hc, RET_W), 1) // RET_DK)
    lg_lane = _lane_log_gamma((ck, RET_W), 1, RET_DK)
    tl = _iota((ck, RET_W), 0).astype(F32)
    ret_eg = jnp.exp((tl + 1.0) * lg_lane)
    ret_kdec = jnp.exp((ck - 1.0 - tl) * lg_lane)
    ret_gc = jnp.exp(ck * _lane_log_gamma((1, RET_W), 1, RET_DK))
    bd = (_iota((RET_W, RET_W), 0) // RET_DK) == (_iota((RET_W, RET_W), 1) // RET_DK)

    a = {}
    b = {}

    def a_norm():
        a["x"] = x_ref[...]
        a["u"] = _rmsnorm(a["x"], nw_ref[...]).astype(BF16)

    def a_qkv_piece(p):
        cols = slice(2 * DN_DK * p, 2 * DN_DK * (p + 1))
        qkvbuf[8:8 + tc, cols] = jnp.dot(a["u"], wqkv_ref[:, cols], preferred_element_type=F32)

    def a_conv_piece(p):
        cols = slice(2 * DN_DK * p, 2 * DN_DK * (p + 1))
        conv = _causal_conv(qkvbuf[8:8 + tc, cols], qkvbuf[0:8, cols], dncw_ref[:, cols])
        qkvbuf[0:8, cols] = qkvbuf[tc:tc + 8, cols]
        act = _silu(conv)
        kind, first_head = divmod(2 * p, DN_HEADS)
        for j in range(2):
            v = act[:, j * DN_DK:(j + 1) * DN_DK]
            if kind == 0:
                q_s[first_head + j] = (v * lax.rsqrt(jnp.sum(v * v, axis=-1, keepdims=True) + EPS)
                                       * (DN_DK ** -0.5))
            elif kind == 1:
                k_s[first_head + j] = v * lax.rsqrt(jnp.sum(v * v, axis=-1, keepdims=True) + EPS)
            else:
                v_s[first_head + j] = v

    def proj(lo, hi):
        return jnp.dot(a["u"], wrest_ref[:, lo:hi], preferred_element_type=F32)

    def a_lru_in():
        lru = proj(512, 1024)
        lrubuf[8:8 + tc, :] = lru[:, :LRU_W]
        xl = _causal_conv(lru[:, :LRU_W], lrubuf[0:8, :], lcw_ref[...])
        lrubuf[0:8, :] = lrubuf[tc:tc + 8, :]
        a["xl"] = xl + lcb_ref[...]
        a["lru_y"] = lru[:, LRU_W:]

    def a_ret_q():
        a["ret_q"] = proj(1024, 1280)

    def a_ret_kv():
        kv = proj(1280, 1792)
        cos = cos_ref[...]
        sin = sin_ref[...]
        a["rq"] = _rope(a["ret_q"], cos, sin)
        a["rk"] = _rope(kv[:, :RET_W], cos, sin) * (RET_DK ** -0.5)
        a["rv"] = kv[:, RET_W:]

    def a_z():
        a["dn_z"] = proj(0, 512)

    def a_g():
        a["ret_g"] = proj(1792, 2048)

    def a_gates():
        abt = lax.dot_general(wabt_ref[...], a["u"], (((1,), (1,)), ((), ())), preferred_element_type=F32)
        g_row = -jnp.exp(alog_c_ref[...]) * _softplus(abt + dtb_c_ref[...])
        rr = _iota((tc, tc), 0)
        cc = _iota((tc, tc), 1)
        upper = jnp.where(((rr // ck) == (cc // ck)) & (rr <= cc), 1.0, 0.0)
        big_g_row = _mm_01_right(g_row, upper)
        is_g = _iota((8, tc), 0) < DN_HEADS
        gb_s[...] = jnp.where(is_g, big_g_row, _sigmoid(abt)).T
        for c in chunks:
            grow_s[c] = jnp.concatenate(
                [big_g_row[h:h + 1, c * ck:(c + 1) * ck] for h in range(DN_HEADS)], axis=1)

    def a_lru_scan_steps():
        ga, gb = _lru_coeffs(a["xl"], wgate_ref[...], bgate_ref[...], lam_ref[...])
        row = _iota((tc, LRU_W), 0)
        gb = gb + jnp.where(row == 0, ga * hst[...], 0.0)
        yield
        s = 1
        while s < tc:
            keep = row >= s
            gb = ga * jnp.where(keep, pltpu.roll(gb, s, 0), 0.0) + gb
            ga = ga * jnp.where(keep, pltpu.roll(ga, s, 0), 1.0)
            s *= 2
            yield
        hst[...] = gb[tc - 1:tc, :]
        a["o_lru"] = (gb * jax.nn.gelu(a["lru_y"])).astype(BF16)
        yield

    def a_kkqk():
        def stack(ref, c, g):
            return jnp.concatenate([ref[h, rows[c], :] for h in range(g * hg, (g + 1) * hg)], axis=0)

        def stack_col(c, g, first):
            return jnp.concatenate([gb_s[rows[c], first + h:first + h + 1]
                                    for h in range(g * hg, (g + 1) * hg)], axis=0)

        a["kst"] = [stack(k_s, c, g) for c, g in probs]
        a["qst"] = [stack(q_s, c, g) for c, g in probs]
        a["vst"] = [stack(v_s, c, g) for c, g in probs]
        gst = [stack_col(c, g, 0) for c, g in probs]
        a["gst"] = gst
        a["bst"] = [stack_col(c, g, DN_HEADS) for c, g in probs]
        dstrict = []
        for k, (c, g) in enumerate(probs):
            diff = gst[k] - grow_s[c][:, g * gc:(g + 1) * gc]
            dstrict.append(jnp.where(strict, jnp.exp(jnp.where(strict, diff, 0.0)), 0.0))
        a["eg"] = [jnp.exp(g) for g in gst]
        a["amat"] = [a["bst"][k] * _mm_nt(a["kst"][k], a["kst"][k]) * dstrict[k] for k in range(len(probs))]
        a["pmat"] = [(_mm_nt(a["qst"][k], a["kst"][k]) * (dstrict[k] + eye_g)).astype(BF16)
                     for k in range(len(probs))]

    tinv = []

    def a_sol():
        kst, qst, vst, gst, bst, eg = (a[k] for k in ("kst", "qst", "vst", "gst", "bst", "eg"))
        sol = [_mm(tinv[k], jnp.concatenate([bst[k] * vst[k], (bst[k] * eg[k]) * kst[k]], axis=1))
               for k in range(len(probs))]
        a["uv"] = [s[:, :DN_DK] for s in sol]
        wq, egl, kd = {}, {}, {}
        for k, (c, g) in enumerate(probs):
            for j in range(hg):
                lo, hi = j * ck, (j + 1) * ck
                key = (c, g * hg + j)
                wq[key] = jnp.concatenate([sol[k][lo:hi, DN_DK:], eg[k][lo:hi] * qst[k][lo:hi]],
                                          axis=0).astype(BF16)
                g_last = gst[k][hi - 1:hi]
                egl[key] = jnp.exp(g_last)
                kd[key] = (kst[k][lo:hi] * jnp.exp(g_last - gst[k][lo:hi])).astype(BF16)
        a["wq"], a["egl"], a["kd"] = wq, egl, kd

    def a_ret_scores():
        qc = [a["rq"][rs] for rs in rows]
        kc = [a["rk"][rs] for rs in rows]
        qp = [jnp.where(head_sel, jnp.concatenate([q] * RET_HEADS, axis=0), 0.0) for q in qc]
        a["qc"] = qc
        a["sc"] = [(_mm_nt(qp[c], kc[c]) * ret_dec).astype(BF16) for c in chunks]

    def a_ret_intra():
        vc = [a["rv"][rs].astype(BF16) for rs in rows]
        o_intra = []
        for c in chunks:
            op = jnp.where(head_sel, jnp.dot(a["sc"][c], vc[c], preferred_element_type=F32), 0.0)
            acc = op[0:ck]
            for h in range(1, RET_HEADS):
                acc = acc + op[h * ck:(h + 1) * ck]
            o_intra.append(acc)
        a["o_intra"] = o_intra
        a["s_upd"] = [jnp.where(bd, _mm_tn(a["rk"][rows[c]] * ret_kdec, vc[c]), 0.0) for c in chunks]

    def a_handoff():
        for k, (c, g) in enumerate(probs):
            h_uv[c, g * gc:(g + 1) * gc, :] = a["uv"][k]
            h_pm[c, g] = a["pmat"][k]
        for c in chunks:
            h_rq[c] = a["qc"][c].astype(BF16)
            h_oi[c] = a["o_intra"][c]
            h_su[c] = a["s_upd"][c]
            for h in range(DN_HEADS):
                h_wq[c, h] = a["wq"][c, h]
                h_kd[c, h] = a["kd"][c, h]
                h_egl[c, h:h + 1, :] = jnp.broadcast_to(a["egl"][c, h], (1, DN_DK))
        h_z[...] = a["dn_z"]
        h_rg[...] = a["ret_g"]
        h_x[...] = a["x"]
        h_lru[...] = a["o_lru"]

    def b_post_a(c):
        ws_qs = [_mm(h_wq[c, h], sdn[h]) for h in range(DN_HEADS)]
        b["u"] = [h_uv[c, lo:hi, :] - ws_qs[h][:ck] for h, (lo, hi) in enumerate(heads)]
        b["qs"] = [w[ck:] for w in ws_qs]

    def b_post_b(c):
        for g in range(ng):
            group = range(g * hg, (g + 1) * hg)
            ost = (jnp.concatenate([b["qs"][h] for h in group], axis=0)
                   + _mm(h_pm[c, g], jnp.concatenate([b["u"][h] for h in group], axis=0)))
            for j, h in enumerate(group):
                odn_s[rows[c], h * DN_DK:(h + 1) * DN_DK] = ost[j * ck:(j + 1) * ck]
        for h in range(DN_HEADS):
            sdn[h] = h_egl[c, h:h + 1, :] * sdn[h] + _mm_tn(h_kd[c, h], b["u"][h])

    def b_ret():
        for c in chunks:
            s0 = sret[...]
            oret_s[rows[c], :] = ret_eg * _mm(h_rq[c], s0) + h_oi[c]
            sret[...] = ret_gc * s0 + h_su[c]

    def b_out():
        nw = dnnw_ref[...]
        for h in range(DN_HEADS):
            sl = slice(h * DN_DK, (h + 1) * DN_DK)
            oh = odn_s[:, sl]
            oh = oh * lax.rsqrt(jnp.mean(oh * oh, axis=-1, keepdims=True) + EPS) * nw
            omix_s[:, sl] = (oh * _silu(h_z[:, sl])).astype(BF16)
        omix_s[:, 512:768] = h_lru[...]
        omix_s[:, 768:1024] = _ret_norm_gate(oret_s[...], h_rg[...], _head_avg_matrix()).astype(BF16)
        y_ref[...] = h_x[...] + jnp.dot(omix_s[...], wout_ref[...], preferred_element_type=F32)

    n_piece = 3 * DN_W // (2 * DN_DK)
    a_seq = [a_norm, functools.partial(a_qkv_piece, 0)]
    for p in range(1, n_piece):
        a_seq.append(lambda p=p: (a_qkv_piece(p), a_conv_piece(p - 1)))
    a_seq += [lambda: (a_lru_in(), a_conv_piece(n_piece - 1)), lambda: (a_gates(), a_ret_q()),
              a_ret_kv, a_z, a_g, a_kkqk]
    b_seq = []
    for c in chunks:
        b_seq += [functools.partial(b_post_a, c), functools.partial(b_post_b, c)]
    b_seq += [b_ret, b_out]
    b_seq.pop(0)()
    for fa in a_seq:
        fa()
        if b_seq:
            b_seq.pop(0)()
    for fb in b_seq:
        fb()
    scan = a_lru_scan_steps()
    for _ in _tri_inv_steps(a["amat"], n_sub, tinv):
        next(scan, None)
    for _ in scan:
        pass
    a_sol()
    a_ret_scores()
    a_ret_intra()
    a_handoff()

    @pl.when((t_a == nt - 1) & (i < n_blocks))
    def _():
        dnt_o_ref[0] = qkvbuf[0:8, :]
        h_o_ref[0] = hst[...]
        lrt_o_ref[0] = lrubuf[0:8, :]

    @pl.when((t_b == nt - 1) & (i > 0))
    def _():
        sdn_o_ref[0] = sdn[...]
        s_bd = sret[...]
        sretbd_o_ref[0] = s_bd
        for h in range(RET_HEADS):
            sret_o_ref[0, h] = s_bd[h * RET_DK:(h + 1) * RET_DK, h * RET_DK:(h + 1) * RET_DK]


def _mix_pipe(x, lw, cos, sin, init, *, batch, seq, tc, ck):
    nt = seq // tc
    nc = tc // ck
    hc = DN_HEADS * ck
    hg = _heads_per_group(ck)
    ng, gc = DN_HEADS // hg, hg * ck
    n_blocks = batch * nt
    blk_a = lambda i: jnp.minimum(i, n_blocks - 1)
    blk_b = lambda i: jnp.maximum(i - 1, 0)
    const = lambda shape: pl.BlockSpec(shape, lambda i: (0,) * len(shape))
    in_specs = [
        pl.BlockSpec((tc, D_MODEL), lambda i: (blk_a(i), 0)),
        const((1, D_MODEL)), const((D_MODEL, 3 * DN_W)), const((8, D_MODEL)),
        const((D_MODEL, 2048)), const((CONV_W, 3 * DN_W)),
        const((8, 1)), const((8, 1)), const((1, DN_DK)),
        const((CONV_W, LRU_W)), const((1, LRU_W)), const((LRU_W, 2 * LRU_W)), const((1, 2 * LRU_W)),
        const((1, LRU_W)),
        pl.BlockSpec((tc, RET_W), lambda i: (blk_a(i) % nt, 0)),
        pl.BlockSpec((tc, RET_W), lambda i: (blk_a(i) % nt, 0)),
        const((D_MODEL, D_MODEL)),
        const((DN_HEADS, DN_DK, DN_DK)), const((8, 3 * DN_W)), const((1, LRU_W)), const((8, LRU_W)),
        const((RET_W, RET_W)),
    ]
    out_shape = [
        jax.ShapeDtypeStruct((batch * seq, D_MODEL), F32),
        jax.ShapeDtypeStruct((batch, DN_HEADS, DN_DK, DN_DK), F32),
        jax.ShapeDtypeStruct((batch, 8, 3 * DN_W), F32),
        jax.ShapeDtypeStruct((batch, 1, LRU_W), F32),
        jax.ShapeDtypeStruct((batch, 8, LRU_W), F32),
        jax.ShapeDtypeStruct((batch, RET_W, RET_W), F32),
        jax.ShapeDtypeStruct((batch, RET_HEADS, RET_DK, RET_DK), F32),
    ]
    seq_a = lambda i: blk_a(i) // nt
    seq_b = lambda i: blk_b(i) // nt
    out_specs = [
        pl.BlockSpec((tc, D_MODEL), lambda i: (blk_b(i), 0)),
        pl.BlockSpec((1, DN_HEADS, DN_DK, DN_DK), lambda i: (seq_b(i), 0, 0, 0)),
        pl.BlockSpec((1, 8, 3 * DN_W), lambda i: (seq_a(i), 0, 0)),
        pl.BlockSpec((1, 1, LRU_W), lambda i: (seq_a(i), 0, 0)),
        pl.BlockSpec((1, 8, LRU_W), lambda i: (seq_a(i), 0, 0)),
        pl.BlockSpec((1, RET_W, RET_W), lambda i: (seq_b(i), 0, 0)),
        pl.BlockSpec((1, RET_HEADS, RET_DK, RET_DK), lambda i: (seq_b(i), 0, 0, 0)),
    ]
    scratch = [
        pltpu.VMEM((DN_HEADS, DN_DK, DN_DK), F32),
        pltpu.VMEM((RET_W, RET_W), F32),
        pltpu.VMEM((1, LRU_W), F32),
        pltpu.VMEM((tc + 8, 3 * DN_W), F32),
        pltpu.VMEM((tc + 8, LRU_W), F32),
        pltpu.VMEM((DN_HEADS, tc, DN_DK), F32),
        pltpu.VMEM((DN_HEADS, tc, DN_DK), F32),
        pltpu.VMEM((DN_HEADS, tc, DN_DK), F32),
        pltpu.VMEM((tc, 8), F32),
        pltpu.VMEM((nc, 1, hc), F32),
        pltpu.VMEM((tc, DN_W), F32),
        pltpu.VMEM((tc, RET_W), F32),
        pltpu.VMEM((tc, D_MODEL), BF16),
        pltpu.VMEM((nc, hc, DN_DK), F32),
        pltpu.VMEM((nc, DN_HEADS, 2 * ck, DN_DK), BF16),
        pltpu.VMEM((nc, ng, gc, gc), BF16),
        pltpu.VMEM((nc, DN_HEADS, ck, DN_DK), BF16),
        pltpu.VMEM((nc, 8, DN_DK), F32),
        pltpu.VMEM((nc, ck, RET_W), BF16),
        pltpu.VMEM((nc, ck, RET_W), F32),
        pltpu.VMEM((nc, RET_W, RET_W), F32),
        pltpu.VMEM((tc, DN_W), F32),
        pltpu.VMEM((tc, RET_W), F32),
        pltpu.VMEM((tc, D_MODEL), F32),
        pltpu.VMEM((tc, LRU_W), BF16),
    ]
    args = [x, lw["norm_mix"], lw["wqkv"], lw["wabt"], lw["wrest"], lw["dn_conv_w"],
            lw["alog_c"], lw["dtb_c"], lw["dn_norm_w"],
            lw["lru_conv_w"], lw["lru_conv_b"], lw["wgate"], lw["bgate"], lw["lam"], cos, sin,
            lw["w_out"], *init]
    return pl.pallas_call(
        functools.partial(_mix_pipe_body, tc=tc, ck=ck, nt=nt, n_blocks=n_blocks),
        grid=(n_blocks + 1,),
        in_specs=in_specs,
        out_specs=out_specs,
        out_shape=out_shape,
        scratch_shapes=scratch,
        compiler_params=pltpu.CompilerParams(
            dimension_semantics=("arbitrary",), vmem_limit_bytes=VMEM_LIMIT_BYTES),
        name=f"mix_pipe_c{ck}",
    )(*args)


def _mix_step_body(*refs, bb, layer):
    (x_ref, nw_ref, wqkv_ref, wab_ref, wrest_ref, dncw_ref, alog_r_ref, dtb_r_ref,
     dnnw_ref, lcw_ref, lcb_ref, wgate_ref, bgate_ref, lam_ref, cos_ref, sin_ref,
     sdn_ref, dnc_ref, h_ref, lrc_ref) = refs[:20]
    refs = refs[20:]
    if layer > 0:
        prev_sdn_ref = refs[0]
        refs = refs[1:]
    (omix_o_ref, retin_o_ref, sdn_o_ref, dnc_o_ref, h_o_ref, lrc_o_ref,
     q_s, k_s, v_s, eg_s, beta_s, odn_s) = refs
    if layer > 0:
        sdn_o_ref[0:layer] = prev_sdn_ref[...]
    x = x_ref[...]
    u = _rmsnorm(x, nw_ref[...]).astype(BF16)

    qkv_pre = jnp.dot(u, wqkv_ref[...], preferred_element_type=F32)
    cw = dncw_ref[...]
    conv = (dnc_ref[0] * cw[0:1, :] + dnc_ref[1] * cw[1:2, :] + dnc_ref[2] * cw[2:3, :]
            + qkv_pre * cw[3:4, :])
    dnc_o_ref[0] = dnc_ref[1]
    dnc_o_ref[1] = dnc_ref[2]
    dnc_o_ref[2] = qkv_pre
    qkv = _silu(conv)
    for h in range(DN_HEADS):
        sl = slice(h * DN_DK, (h + 1) * DN_DK)
        qh = qkv[:, sl]
        kh = qkv[:, DN_W + h * DN_DK:DN_W + (h + 1) * DN_DK]
        q_s[:, sl] = qh * lax.rsqrt(jnp.sum(qh * qh, axis=-1, keepdims=True) + EPS) * (DN_DK ** -0.5)
        k_s[:, sl] = kh * lax.rsqrt(jnp.sum(kh * kh, axis=-1, keepdims=True) + EPS)
    v_s[...] = qkv[:, 2 * DN_W:]
    ab = jnp.dot(u, wab_ref[...], preferred_element_type=F32)
    eg_s[...] = jnp.exp(-jnp.exp(alog_r_ref[...]) * _softplus(ab[:, :128] + dtb_r_ref[...]))
    beta_s[...] = _sigmoid(ab[:, 128:])

    rest = jnp.dot(u, wrest_ref[...], preferred_element_type=F32)
    dn_z = rest[:, 0:512]
    lru_x = rest[:, 512:768]
    lru_y = rest[:, 768:1024]
    ret_g = rest[:, 1792:2048]

    lw = lcw_ref[...]
    xl = (lrc_ref[0] * lw[0:1, :] + lrc_ref[1] * lw[1:2, :] + lrc_ref[2] * lw[2:3, :]
          + lru_x * lw[3:4, :] + lcb_ref[...])
    lrc_o_ref[0] = lrc_ref[1]
    lrc_o_ref[1] = lrc_ref[2]
    lrc_o_ref[2] = lru_x
    a, b = _lru_coeffs(xl, wgate_ref[...], bgate_ref[...], lam_ref[...])
    h_new = a * h_ref[...] + b
    h_o_ref[...] = h_new
    o_lru = h_new * jax.nn.gelu(lru_y)

    cos = cos_ref[...]
    sin = sin_ref[...]
    retin_o_ref[:, 0:256] = _rope(rest[:, 1024:1280], cos, sin)
    retin_o_ref[:, 256:512] = _rope(rest[:, 1280:1536], cos, sin) * (RET_DK ** -0.5)
    retin_o_ref[:, 512:768] = rest[:, 1536:1792]
    retin_o_ref[:, 768:1024] = ret_g

    def per_tile(i8, carry):
        r0 = pl.multiple_of(i8 * 8, 8)
        eg_t = eg_s[pl.ds(r0, 8), :]
        beta_t = beta_s[pl.ds(r0, 8), :]
        for h in range(DN_HEADS):
            sl = slice(h * DN_DK, (h + 1) * DN_DK)
            kt = k_s[pl.ds(r0, 8), sl].T
            qt = q_s[pl.ds(r0, 8), sl].T
            v_t = v_s[pl.ds(r0, 8), sl]
            rows = []
            for j in range(8):
                kcol = kt[:, j:j + 1]
                qcol = qt[:, j:j + 1]
                s0 = sdn_ref[r0 + j, h]
                eg = eg_t[j:j + 1, h:h + 1]
                ks = jnp.sum(s0 * kcol, axis=0, keepdims=True)
                qs = jnp.sum(s0 * qcol, axis=0, keepdims=True)
                qk = jnp.sum(qcol * kcol, axis=0, keepdims=True)
                uu = beta_t[j:j + 1, h:h + 1] * (v_t[j:j + 1, :] - eg * ks)
                rows.append(eg * qs + qk * uu)
                sdn_o_ref[layer, r0 + j, h] = eg * s0 + kcol * uu
            odn_s[pl.ds(r0, 8), sl] = jnp.concatenate(rows, axis=0)
        return carry

    lax.fori_loop(0, bb // 8, per_tile, 0)

    nw = dnnw_ref[...]
    for h in range(DN_HEADS):
        sl = slice(h * DN_DK, (h + 1) * DN_DK)
        oh = odn_s[:, sl]
        oh = oh * lax.rsqrt(jnp.mean(oh * oh, axis=-1, keepdims=True) + EPS) * nw
        omix_o_ref[:, sl] = (oh * _silu(dn_z[:, sl])).astype(BF16)
    omix_o_ref[:, DN_W:DN_W + LRU_W] = o_lru.astype(BF16)


def _mix_step(x, lw, cos, sin, layer, sdn, dnc, hl, lrc, prev_sdn, *, bb):
    n = x.shape[0]
    const = lambda shape: pl.BlockSpec(shape, lambda i: (0,) * len(shape))
    in_specs = [
        pl.BlockSpec((bb, D_MODEL), lambda i: (i, 0)),
        const((1, D_MODEL)), const((D_MODEL, 3 * DN_W)), const((D_MODEL, 256)), const((D_MODEL, 2048)),
        const((CONV_W, 3 * DN_W)), const((1, 128)), const((1, 128)), const((1, DN_DK)),
        const((CONV_W, LRU_W)), const((1, LRU_W)), const((LRU_W, 2 * LRU_W)), const((1, 2 * LRU_W)),
        const((1, LRU_W)), const((1, RET_W)), const((1, RET_W)),
        pl.BlockSpec((None, bb, DN_HEADS, DN_DK, DN_DK), lambda i: (layer, i, 0, 0, 0)),
        pl.BlockSpec((CONV_W - 1, bb, 3 * DN_W), lambda i: (0, i, 0)),
        pl.BlockSpec((None, bb, LRU_W), lambda i: (layer, i, 0)),
        pl.BlockSpec((CONV_W - 1, bb, LRU_W), lambda i: (0, i, 0)),
    ]
    if layer > 0:
        in_specs.append(pl.BlockSpec((layer, bb, DN_HEADS, DN_DK, DN_DK), lambda i: (0, i, 0, 0, 0)))
    out_shape = [
        jax.ShapeDtypeStruct((n, DN_W + LRU_W), BF16),
        jax.ShapeDtypeStruct((n, 4 * RET_W), F32),
        jax.ShapeDtypeStruct((layer + 1, n, DN_HEADS, DN_DK, DN_DK), F32),
        jax.ShapeDtypeStruct((CONV_W - 1, n, 3 * DN_W), F32),
        jax.ShapeDtypeStruct((n, LRU_W), F32),
        jax.ShapeDtypeStruct((CONV_W - 1, n, LRU_W), F32),
    ]
    out_specs = [
        pl.BlockSpec((bb, DN_W + LRU_W), lambda i: (i, 0)),
        pl.BlockSpec((bb, 4 * RET_W), lambda i: (i, 0)),
        pl.BlockSpec((layer + 1, bb, DN_HEADS, DN_DK, DN_DK), lambda i: (0, i, 0, 0, 0)),
        pl.BlockSpec((CONV_W - 1, bb, 3 * DN_W), lambda i: (0, i, 0)),
        pl.BlockSpec((bb, LRU_W), lambda i: (i, 0)),
        pl.BlockSpec((CONV_W - 1, bb, LRU_W), lambda i: (0, i, 0)),
    ]
    scratch = [
        pltpu.VMEM((bb, DN_W), F32), pltpu.VMEM((bb, DN_W), F32), pltpu.VMEM((bb, DN_W), F32),
        pltpu.VMEM((bb, 128), F32), pltpu.VMEM((bb, 128), F32), pltpu.VMEM((bb, DN_W), F32),
    ]
    args = [x, lw["norm_mix"], lw["wqkv"], lw["wab"], lw["wrest"], lw["dn_conv_w"], lw["alog_r"],
            lw["dtb_r"], lw["dn_norm_w"], lw["lru_conv_w"], lw["lru_conv_b"], lw["wgate"], lw["bgate"],
            lw["lam"], cos, sin, sdn, dnc, hl, lrc]
    if layer > 0:
        args.append(prev_sdn)
    return pl.pallas_call(
        functools.partial(_mix_step_body, bb=bb, layer=layer),
        grid=(n // bb,),
        in_specs=in_specs,
        out_specs=out_specs,
        out_shape=out_shape,
        scratch_shapes=scratch,
        compiler_params=pltpu.CompilerParams(
            dimension_semantics=("arbitrary",), vmem_limit_bytes=VMEM_LIMIT_BYTES),
        name="mix_step",
    )(*args)


def _ret_step_body(*refs, layer):
    x_ref, omix_ref, retin_ref, wout_ref, s_ref = refs[:5]
    refs = refs[5:]
    if layer > 0:
        prev_ref = refs[0]
        refs = refs[1:]
    y_ref, s_o_ref, qt_s, kt_s, vt_s, ot_s = refs
    h = pl.program_id(0)
    if layer > 0:
        s_o_ref[0:layer] = prev_ref[...]

    @pl.when(h == 0)
    def _():
        qt_s[...] = retin_ref[:, 0:RET_W].T
        kt_s[...] = retin_ref[:, RET_W:2 * RET_W].T
        vt_s[...] = retin_ref[:, 2 * RET_W:3 * RET_W].T

    rows = pl.ds(pl.multiple_of(h * RET_DK, RET_DK), RET_DK)
    q = qt_s[rows, :]
    k = kt_s[rows, :]
    v = vt_s[rows, :]
    gamma = jnp.float32(math.exp(LOG_GAMMA[0]))
    for i in range(1, RET_HEADS):
        gamma = jnp.where(h == i, jnp.float32(math.exp(LOG_GAMMA[i])), gamma)
    acc = jnp.zeros_like(v)
    for d in range(RET_DK):
        s_d = s_ref[d]
        acc = acc + s_d * q[d:d + 1, :]
        s_o_ref[layer, 0, d] = gamma * s_d + k[d:d + 1, :] * v
    ot_s[rows, :] = gamma * acc + jnp.sum(q * k, axis=0, keepdims=True) * v

    @pl.when(h == pl.num_programs(0) - 1)
    def _():
        o_ret = _ret_norm_gate(ot_s[...].T, retin_ref[:, 3 * RET_W:], _head_avg_matrix())
        omix = jnp.concatenate([omix_ref[...], o_ret.astype(BF16)], axis=1)
        y_ref[...] = x_ref[...] + jnp.dot(omix, wout_ref[...], preferred_element_type=F32)


def _ret_step(x, omix_part, retin, w_out, sret_t, layer, prev, *, n):
    const = lambda shape: pl.BlockSpec(shape, lambda h: (0,) * len(shape))
    in_specs = [
        const((n, D_MODEL)), const((n, DN_W + LRU_W)), const((n, 4 * RET_W)), const((D_MODEL, D_MODEL)),
        pl.BlockSpec((None, None, RET_DK, RET_DK, n), lambda h: (layer, h, 0, 0, 0)),
    ]
    args = [x, omix_part, retin, w_out, sret_t]
    if layer > 0:
        in_specs.append(pl.BlockSpec((layer, 1, RET_DK, RET_DK, n), lambda h: (0, h, 0, 0, 0)))
        args.append(prev)
    return pl.pallas_call(
        functools.partial(_ret_step_body, layer=layer),
        grid=(RET_HEADS,),
        in_specs=in_specs,
        out_specs=[const((n, D_MODEL)),
                   pl.BlockSpec((layer + 1, 1, RET_DK, RET_DK, n), lambda h: (0, h, 0, 0, 0))],
        out_shape=[jax.ShapeDtypeStruct((n, D_MODEL), F32),
                   jax.ShapeDtypeStruct((layer + 1, RET_HEADS, RET_DK, RET_DK, n), F32)],
        scratch_shapes=[pltpu.VMEM((RET_W, n), F32)] * 4,
        compiler_params=pltpu.CompilerParams(
            dimension_semantics=("arbitrary",), vmem_limit_bytes=VMEM_LIMIT_BYTES),
        name="ret_step",
    )(*args)


def _rope_tables(pos):
    half = RET_DK // 2
    inv = ROPE_BASE ** (-jnp.arange(half, dtype=F32) / half)
    ang = pos.astype(F32)[:, None] * inv[None, :]
    cos = jnp.cos(ang)
    sin = jnp.sin(ang)
    cos_full = jnp.tile(jnp.concatenate([cos, cos], axis=-1), (1, RET_HEADS))
    sin_signed = jnp.tile(jnp.concatenate([-sin, sin], axis=-1), (1, RET_HEADS))
    return cos_full, sin_signed


def _block_diag(w):
    n, d, e = w.shape
    eye = jnp.eye(n, dtype=w.dtype)
    return (eye[:, None, :, None] * w[:, :, None, :]).reshape(n * d, n * e)


def _layer_weights(l, norm_mix, w_in, dn_conv_w, dn_a_log, dn_dt_bias, dn_norm_w, lru_conv_w, lru_conv_b,
                   lru_wa, lru_ba, lru_wx, lru_bx, lru_lambda, w_out):
    wl = w_in[l]
    o_a = 3 * DN_W
    wa = wl[:, o_a:o_a + DN_HEADS]
    wb = wl[:, o_a + DN_HEADS:o_a + 2 * DN_HEADS]
    pad_cols = lambda w: jnp.pad(w, ((0, 0), (0, 128 - DN_HEADS)))
    pad_lane = lambda v: jnp.pad(v, (0, 128 - DN_HEADS)).reshape(1, 128)
    pad_sub = lambda v: jnp.pad(v, (0, 8 - DN_HEADS)).reshape(8, 1)
    return {
        "norm_mix": norm_mix[l].reshape(1, D_MODEL),
        "wqkv": wl[:, :o_a].astype(BF16),
        "wab": jnp.concatenate([pad_cols(wa), pad_cols(wb)], axis=1).astype(BF16),
        "wabt": jnp.concatenate([wa, wb], axis=1).T.astype(BF16),
        "wrest": wl[:, o_a + 2 * DN_HEADS:].astype(BF16),
        "dn_conv_w": dn_conv_w[l],
        "alog_r": pad_lane(dn_a_log[l]), "dtb_r": pad_lane(dn_dt_bias[l]),
        "alog_c": pad_sub(dn_a_log[l]), "dtb_c": pad_sub(dn_dt_bias[l]),
        "dn_norm_w": dn_norm_w[l].reshape(1, DN_DK),
        "lru_conv_w": lru_conv_w[l], "lru_conv_b": lru_conv_b[l].reshape(1, LRU_W),
        "wgate": jnp.concatenate([_block_diag(lru_wa[l]), _block_diag(lru_wx[l])], axis=1).astype(BF16),
        "bgate": jnp.concatenate([lru_ba[l], lru_bx[l]]).reshape(1, 2 * LRU_W),
        "lam": lru_lambda[l].reshape(1, LRU_W),
        "w_out": w_out[l].astype(BF16),
    }


def kernel(x_prompt, x_sample, state_dn, state_dn_conv, state_lru, state_lru_conv, state_ret, meta_tokens, norm_ffn1, w_ffn1_in, w_ffn1_out, norm_mix, w_in, dn_conv_w, dn_a_log, dn_dt_bias, dn_norm_w, lru_conv_w, lru_conv_b, lru_wa, lru_ba, lru_wx, lru_bx, lru_lambda, w_out, norm_ffn2, w_ffn2_in, w_ffn2_out, norm_final):
    batch, seq, _ = x_prompt.shape
    n_dec = x_sample.shape[0]
    depth = w_in.shape[0]
    lws = [_layer_weights(l, norm_mix, w_in, dn_conv_w, dn_a_log, dn_dt_bias, dn_norm_w, lru_conv_w,
                          lru_conv_b, lru_wa, lru_ba, lru_wx, lru_bx, lru_lambda, w_out)
           for l in range(depth)]
    cos_m, sin_m = _rope_tables(jnp.arange(N_META))
    cos_p, sin_p = _rope_tables(N_META + jnp.arange(seq))
    cos_s, sin_s = _rope_tables(PAST_LEN + jnp.arange(1))

    n_small = n_dec + N_META
    xs = jnp.concatenate([x_sample[:, 0, :], meta_tokens.astype(F32)], axis=0)
    zero_init = (jnp.zeros((DN_HEADS, DN_DK, DN_DK), F32), jnp.zeros((8, 3 * DN_W), F32),
                 jnp.zeros((1, LRU_W), F32), jnp.zeros((8, LRU_W), F32), jnp.zeros((RET_W, RET_W), F32))
    new_s, meta_state, ffn_w = [], [], []
    sdn_all = sret_all = None
    sret_t = jnp.transpose(state_ret, (0, 2, 3, 4, 1))
    for l in range(depth):
        last = l == depth - 1
        xs, *f1 = _ffn_cast(xs, norm_ffn1[l], w_ffn1_in, w_ffn1_out, l, tf=256)
        omix_part, retin, sdn_all, dnc, hl, lrc = _mix_step(
            xs[:n_dec], lws[l], cos_s, sin_s, l, state_dn, jnp.swapaxes(state_dn_conv[l], 0, 1),
            state_lru, jnp.swapaxes(state_lru_conv[l], 0, 1), sdn_all, bb=8)
        ys, sret_all = _ret_step(xs[:n_dec], omix_part, retin, lws[l]["w_out"], sret_t, l, sret_all,
                                 n=n_dec)
        ym, m_sdn, m_dnt, m_h, m_lrt, m_sretbd, _ = _mix_pipe(
            xs[n_dec:], lws[l], cos_m, sin_m, zero_init, batch=1, seq=N_META, tc=N_META, ck=N_META)
        new_s.append((jnp.swapaxes(dnc, 0, 1), hl, jnp.swapaxes(lrc, 0, 1)))
        meta_state.append((m_sdn[0], m_dnt[0], m_h[0], m_lrt[0], m_sretbd[0]))
        xs = jnp.concatenate([ys, ym], axis=0)
        xs, *f2 = _ffn_cast(xs, norm_ffn2[l], w_ffn2_in, w_ffn2_out, l, norm_final if last else None, tf=256)
        ffn_w.append((f1, f2))
    y_sample = xs[:n_dec].reshape(n_dec, 1, D_MODEL)

    xp = x_prompt.reshape(batch * seq, D_MODEL)
    new_p = []
    for l in range(depth):
        last = l == depth - 1
        xp = _ffn(xp, norm_ffn1[l], *ffn_w[l][0], tm=1024, nf=11)
        xp, sdn, dnt, hl, lrt, _, sret = _mix_pipe(
            xp, lws[l], cos_p, sin_p, meta_state[l], batch=batch, seq=seq, tc=256, ck=CHUNK)
        new_p.append((sdn, dnt[:, 5:8], hl[:, 0], lrt[:, 5:8], sret))
        xp = _ffn(xp, norm_ffn2[l], *ffn_w[l][1], norm_final if last else None, tm=1024, nf=11)
    y_prompt = xp.reshape(batch, seq, D_MODEL)

    outs_p = [jnp.stack([s[j] for s in new_p]) for j in range(5)]
    dnc_s, lru_s, lrc_s = [jnp.stack([s[j] for s in new_s]) for j in range(3)]
    sret_s = jnp.transpose(sret_all, (0, 4, 1, 2, 3))
    return (y_prompt, y_sample, *outs_p, sdn_all, dnc_s, lru_s, lrc_s, sret_s)
```

```python
import functools
import math

import jax
import jax.numpy as jnp
from jax import lax
from jax.experimental import pallas as pl
from jax.experimental.pallas import tpu as pltpu

F32 = jnp.float32
BF16 = jnp.bfloat16

D_MODEL = 1024
N_META = 16
PAST_LEN = 16384
DN_HEADS = 4
DN_DK = 128
DN_W = 512
LRU_W = 256
LRU_BLOCKS = 4
LRU_C = 8.0
RET_HEADS = 4
RET_DK = 64
RET_W = 256
CONV_W = 4
CHUNK = 64
D_FF = 2816
ROPE_BASE = 10000.0
EPS = 1e-6
SUB = 16
LOG_GAMMA = tuple(math.log1p(-2.0 ** (-5.0 - h)) for h in range(RET_HEADS))

V7X_VMEM_BYTES = 64 * 1024 * 1024
VMEM_LIMIT_BYTES = V7X_VMEM_BYTES * 7 // 8
V7X_MXU_WIDTH = 256

FFN_TOKEN_TILE = 1024
FFN_DFF_SLICE = V7X_MXU_WIDTH
MIX_TIME_BLOCK = 4 * CHUNK
DECODE_BATCH_TILE = 8


def _mm(a, b):
    return jnp.dot(a.astype(BF16), b.astype(BF16), preferred_element_type=F32)


def _mm_nt(a, b):
    return lax.dot_general(a.astype(BF16), b.astype(BF16), (((1,), (1,)), ((), ())),
                           preferred_element_type=F32)


def _mm_tn(a, b):
    return lax.dot_general(a.astype(BF16), b.astype(BF16), (((0,), (0,)), ((), ())),
                           preferred_element_type=F32)


def _split3(x):
    hi = x.astype(BF16)
    r = x - hi.astype(F32)
    mid = r.astype(BF16)
    lo = (r - mid.astype(F32)).astype(BF16)
    return hi, mid, lo


def _mm_01_right(x, m01):
    m = m01.astype(BF16)
    return sum(jnp.dot(p, m, preferred_element_type=F32) for p in _split3(x))


def _mm_split(x, m_bf16):
    hi = x.astype(BF16)
    lo = (x - hi.astype(F32)).astype(BF16)
    return (jnp.dot(hi, m_bf16, preferred_element_type=F32)
            + jnp.dot(lo, m_bf16, preferred_element_type=F32))


def _rmsnorm(x, w):
    return x * lax.rsqrt(jnp.mean(x * x, axis=-1, keepdims=True) + EPS) * w


def _sigmoid(x):
    return 0.5 + 0.5 * jnp.tanh(0.5 * x)


def _silu(x):
    return x * _sigmoid(x)


def _causal_conv(x, tail, w):
    tc, n = x.shape
    nt8 = tc // 8
    first = _iota((nt8, 8, n), 1) == 0
    xm1, xm2, xm3 = tail[7:8], tail[6:7], tail[5:6]

    def shift1(s, carry):
        r = pltpu.roll(s.reshape(nt8, 8, n), 1, 1)
        before = jnp.concatenate([jnp.broadcast_to(carry, (1, 8, n)), r[:-1]], axis=0)
        return jnp.where(first, before, r).reshape(tc, n)

    s = shift1(x * w[0:1], xm1 * w[0:1])
    s = shift1(s + x * w[1:2], xm2 * w[0:1] + xm1 * w[1:2])
    s = shift1(s + x * w[2:3], xm3 * w[0:1] + xm2 * w[1:2] + xm1 * w[2:3])
    return s + x * w[3:4]


def _softplus(x):
    return jnp.maximum(x, 0.0) + jnp.log1p(jnp.exp(-jnp.abs(x)))


def _iota(shape, dim):
    return lax.broadcasted_iota(jnp.int32, shape, dim)


def _lane_log_gamma(shape, dim, width):
    head = _iota(shape, dim) // width
    out = jnp.full(shape, LOG_GAMMA[0], F32)
    for h in range(1, RET_HEADS):
        out = jnp.where(head == h, LOG_GAMMA[h], out)
    return out


def _rope(x, cos, sin_signed):
    n = x.shape[1]
    half = RET_DK // 2
    first = (_iota(x.shape, 1) % RET_DK) < half
    swapped = jnp.where(first, pltpu.roll(x, n - half, 1), pltpu.roll(x, half, 1))
    return x * cos + swapped * sin_signed


def _ret_norm_gate(o, gate, avg_bf16):
    mu = _mm_split(o, avg_bf16)
    d = o - mu
    var = _mm_split(d * d, avg_bf16)
    return d * lax.rsqrt(var + EPS) * _silu(gate)


def _head_avg_matrix():
    r = _iota((RET_W, RET_W), 0) // RET_DK
    c = _iota((RET_W, RET_W), 1) // RET_DK
    return jnp.where(r == c, 1.0 / RET_DK, 0.0).astype(BF16)


def _lru_coeffs(xl, wgate, bgate, lam):
    gates = _mm(xl, wgate) + bgate
    r = _sigmoid(gates[:, :LRU_W])
    i = _sigmoid(gates[:, LRU_W:])
    log_a = -LRU_C * r * _softplus(-lam)
    a = jnp.exp(log_a)
    b = jnp.sqrt(jnp.maximum(-jnp.tanh(log_a) * (a * a + 1.0), 0.0)) * (i * xl)
    return a, b


def _ffn_body(*refs, final, nf):
    if final:
        x_ref, nw_ref, wg_ref, wu_ref, wo_ref, fw_ref, o_ref = refs
    else:
        x_ref, nw_ref, wg_ref, wu_ref, wo_ref, o_ref = refs
    tf = D_FF // nf
    x = x_ref[...]
    u = _rmsnorm(x, nw_ref[...]).astype(BF16)
    acc = None
    for j in range(nf):
        gate = jnp.dot(u, wg_ref[:, j * tf:(j + 1) * tf], preferred_element_type=F32)
        up = jnp.dot(u, wu_ref[:, j * tf:(j + 1) * tf], preferred_element_type=F32)
        h = (_silu(gate) * up).astype(BF16)
        part = jnp.dot(h, wo_ref[j * tf:(j + 1) * tf, :], preferred_element_type=F32)
        acc = part if acc is None else acc + part
    y = x + 0.5 * acc
    if final:
        y = _rmsnorm(y, fw_ref[...])
    o_ref[...] = y


def _ffn(x, norm_w, wg, wu, wo, final_w=None, *, tm, nf):
    n = x.shape[0]
    final = final_w is not None
    resident = pl.Buffered(1)
    in_specs = [
        pl.BlockSpec((tm, D_MODEL), lambda i: (i, 0)),
        pl.BlockSpec((1, D_MODEL), lambda i: (0, 0)),
        pl.BlockSpec((D_MODEL, D_FF), lambda i: (0, 0), pipeline_mode=resident),
        pl.BlockSpec((D_MODEL, D_FF), lambda i: (0, 0), pipeline_mode=resident),
        pl.BlockSpec((D_FF, D_MODEL), lambda i: (0, 0), pipeline_mode=resident),
    ]
    args = [x, norm_w.reshape(1, D_MODEL), wg, wu, wo]
    if final:
        in_specs.append(pl.BlockSpec((1, D_MODEL), lambda i: (0, 0)))
        args.append(final_w.reshape(1, D_MODEL))
    return pl.pallas_call(
        functools.partial(_ffn_body, final=final, nf=nf),
        grid=(n // tm,),
        in_specs=in_specs,
        out_specs=pl.BlockSpec((tm, D_MODEL), lambda i: (i, 0)),
        out_shape=jax.ShapeDtypeStruct((n, D_MODEL), F32),
        compiler_params=pltpu.CompilerParams(
            dimension_semantics=("arbitrary",), vmem_limit_bytes=VMEM_LIMIT_BYTES),
        name="ffn_final" if final else "ffn",
    )(*args)


def _ffn_cast_body(*refs, final):
    if final:
        x_ref, nw_ref, wg_ref, wu_ref, wo_ref, fw_ref, o_ref, wg_o, wu_o, wo_o, u_s, acc_s = refs
    else:
        x_ref, nw_ref, wg_ref, wu_ref, wo_ref, o_ref, wg_o, wu_o, wo_o, u_s, acc_s = refs
    j = pl.program_id(0)

    @pl.when(j == 0)
    def _():
        u_s[...] = _rmsnorm(x_ref[...], nw_ref[...]).astype(BF16)
        acc_s[...] = jnp.zeros_like(acc_s)

    wg = wg_ref[...].astype(BF16)
    wu = wu_ref[...].astype(BF16)
    wo = wo_ref[...].astype(BF16)
    wg_o[...] = wg
    wu_o[...] = wu
    wo_o[...] = wo
    u = u_s[...]
    gate = jnp.dot(u, wg, preferred_element_type=F32)
    up = jnp.dot(u, wu, preferred_element_type=F32)
    acc_s[...] += jnp.dot((_silu(gate) * up).astype(BF16), wo, preferred_element_type=F32)

    @pl.when(j == pl.num_programs(0) - 1)
    def _():
        y = x_ref[...] + 0.5 * acc_s[...]
        if final:
            y = _rmsnorm(y, fw_ref[...])
        o_ref[...] = y


def _ffn_cast(x, norm_w, w_in, w_out, layer, final_w=None, *, tf):
    n = x.shape[0]
    nf = D_FF // tf
    final = final_w is not None
    in_specs = [
        pl.BlockSpec((n, D_MODEL), lambda j: (0, 0)),
        pl.BlockSpec((1, D_MODEL), lambda j: (0, 0)),
        pl.BlockSpec((None, D_MODEL, tf), lambda j: (layer, 0, j)),
        pl.BlockSpec((None, D_MODEL, tf), lambda j: (layer, 0, j + nf)),
        pl.BlockSpec((None, tf, D_MODEL), lambda j: (layer, j, 0)),
    ]
    args = [x, norm_w.reshape(1, D_MODEL), w_in, w_in, w_out]
    if final:
        in_specs.append(pl.BlockSpec((1, D_MODEL), lambda j: (0, 0)))
        args.append(final_w.reshape(1, D_MODEL))
    return pl.pallas_call(
        functools.partial(_ffn_cast_body, final=final),
        grid=(nf,),
        in_specs=in_specs,
        out_specs=[pl.BlockSpec((n, D_MODEL), lambda j: (0, 0)),
                   pl.BlockSpec((D_MODEL, tf), lambda j: (0, j)),
                   pl.BlockSpec((D_MODEL, tf), lambda j: (0, j)),
                   pl.BlockSpec((tf, D_MODEL), lambda j: (j, 0))],
        out_shape=[jax.ShapeDtypeStruct((n, D_MODEL), F32),
                   jax.ShapeDtypeStruct((D_MODEL, D_FF), BF16),
                   jax.ShapeDtypeStruct((D_MODEL, D_FF), BF16),
                   jax.ShapeDtypeStruct((D_FF, D_MODEL), BF16)],
        scratch_shapes=[pltpu.VMEM((n, D_MODEL), BF16), pltpu.VMEM((n, D_MODEL), F32)],
        compiler_params=pltpu.CompilerParams(
            dimension_semantics=("arbitrary",), vmem_limit_bytes=VMEM_LIMIT_BYTES),
        name="ffn_cast_final" if final else "ffn_cast",
    )(*args)


def _heads_per_group(ck):
    return max(1, min(DN_HEADS, 128 // ck))


def _tri_inv_steps(a, n_sub, out):
    m = a[0].shape[0]
    r = _iota((m, m), 0)
    c = _iota((m, m), 1)
    eye = jnp.where(r == c, 1.0, 0.0).astype(F32)
    diag = (r // SUB) == (c // SUB)
    x = [jnp.where(diag, ai, 0.0) for ai in a]
    off = [ai - xi for ai, xi in zip(a, x)]
    p = [eye - xi for xi in x]
    for _ in range(3):
        x = [_mm(xi, xi) for xi in x]
        yield
        p = [pi + _mm(pi, xi) for pi, xi in zip(p, x)]
        yield
    if n_sub == 1:
        out.extend(p)
        return
    n = [_mm(pi, oi) for pi, oi in zip(p, off)]
    yield
    n2 = [_mm(ni, ni) for ni in n]
    yield
    rr = [eye - ni + n2i - _mm(ni, n2i) for ni, n2i in zip(n, n2)]
    yield
    out.extend(_mm(ri, pi) for ri, pi in zip(rr, p))
    yield


def _mix_pipe_body(x_ref, nw_ref, wqkv_ref, wabt_ref, wrest_ref, dncw_ref,
                   alog_c_ref, dtb_c_ref, dnnw_ref,
                   lcw_ref, lcb_ref, wgate_ref, bgate_ref, lam_ref, cos_ref, sin_ref, wout_ref,
                   sdn0_ref, dnt0_ref, h0_ref, lrt0_ref, sret0_ref,
                   y_ref, sdn_o_ref, dnt_o_ref, h_o_ref, lrt_o_ref, sretbd_o_ref, sret_o_ref,
                   sdn, sret, hst, dn_tail, lru_tail, q_s, k_s, v_s, gb_s, grow_s, odn_s, oret_s, omix_s,
                   h_uv, h_wq, h_pm, h_kd, h_egl, h_rq, h_oi, h_su, h_z, h_rg, h_x, h_lru,
                   *, tc, ck, nt, n_blocks):
    i = pl.program_id(0)
    t_a = i % nt
    t_b = (i + nt - 1) % nt
    nc = tc // ck
    hc = DN_HEADS * ck
    n_sub = ck // SUB
    chunks = range(nc)
    rows = [slice(c * ck, (c + 1) * ck) for c in chunks]
    heads = [(h * ck, (h + 1) * ck) for h in range(DN_HEADS)]
    hg = _heads_per_group(ck)
    ng = DN_HEADS // hg
    gc = hg * ck
    probs = [(c, g) for c in chunks for g in range(ng)]

    @pl.when(i == 0)
    def _():
        for ref in (h_uv, h_wq, h_pm, h_kd, h_egl, h_rq, h_oi, h_su, h_z, h_rg, h_x, h_lru):
            ref[...] = jnp.zeros_like(ref)

    @pl.when(t_a == 0)
    def _():
        hst[...] = h0_ref[...]
        dn_tail[...] = dnt0_ref[...]
        lru_tail[...] = lrt0_ref[...]

    @pl.when((t_b == 0) | (i == 0))
    def _():
        sdn[...] = sdn0_ref[...]
        sret[...] = sret0_ref[...]

    sr = _iota((gc, gc), 0)
    sc = _iota((gc, gc), 1)
    strict = ((sr // ck) == (sc // ck)) & (sr > sc)
    eye_g = jnp.where(sr == sc, 1.0, 0.0).astype(F32)
    lg_rows = _lane_log_gamma((hc, ck), 0, ck)
    tpos = _iota((hc, ck), 0) % ck
    ipos = _iota((hc, ck), 1)
    causal = tpos >= ipos
    ret_dec = jnp.where(causal, jnp.exp(jnp.where(causal, (tpos - ipos).astype(F32) * lg_rows, 0.0)), 0.0)
    head_sel = (_iota((hc, RET_W), 0) // ck) == (_iota((hc, RET_W), 1) // RET_DK)
    lg_lane = _lane_log_gamma((ck, RET_W), 1, RET_DK)
    tl = _iota((ck, RET_W), 0).astype(F32)
    ret_eg = jnp.exp((tl + 1.0) * lg_lane)
    ret_kdec = jnp.exp((ck - 1.0 - tl) * lg_lane)
    ret_gc = jnp.exp(ck * _lane_log_gamma((1, RET_W), 1, RET_DK))
    bd = (_iota((RET_W, RET_W), 0) // RET_DK) == (_iota((RET_W, RET_W), 1) // RET_DK)

    a = {}
    b = {}

    def a_norm():
        a["x"] = x_ref[...]
        a["u"] = _rmsnorm(a["x"], nw_ref[...]).astype(BF16)

    def a_qkv_piece(p):
        cols = slice(2 * DN_DK * p, 2 * DN_DK * (p + 1))
        a["pre", p] = jnp.dot(a["u"], wqkv_ref[:, cols], preferred_element_type=F32)

    def a_conv_piece(p):
        cols = slice(2 * DN_DK * p, 2 * DN_DK * (p + 1))
        pre = a.pop(("pre", p))
        conv = _causal_conv(pre, dn_tail[:, cols], dncw_ref[:, cols])
        dn_tail[:, cols] = pre[tc - 8:tc]
        act = _silu(conv)
        kind, first_head = divmod(2 * p, DN_HEADS)
        for j in range(2):
            v = act[:, j * DN_DK:(j + 1) * DN_DK]
            if kind == 0:
                q_s[first_head + j] = (v * lax.rsqrt(jnp.sum(v * v, axis=-1, keepdims=True) + EPS)
                                       * (DN_DK ** -0.5))
            elif kind == 1:
                k_s[first_head + j] = v * lax.rsqrt(jnp.sum(v * v, axis=-1, keepdims=True) + EPS)
            else:
                v_s[first_head + j] = v

    def proj(lo, hi):
        return jnp.dot(a["u"], wrest_ref[:, lo:hi], preferred_element_type=F32)

    def a_lru_in():
        lru = proj(512, 1024)
        xl = _causal_conv(lru[:, :LRU_W], lru_tail[...], lcw_ref[...])
        lru_tail[...] = lru[tc - 8:tc, :LRU_W]
        a["xl"] = xl + lcb_ref[...]
        a["lru_y"] = lru[:, LRU_W:]

    def a_ret_q():
        a["ret_q"] = proj(1024, 1280)

    def a_ret_kv():
        kv = proj(1280, 1792)
        cos = cos_ref[...]
        sin = sin_ref[...]
        a["rq"] = _rope(a["ret_q"], cos, sin)
        a["rk"] = _rope(kv[:, :RET_W], cos, sin) * (RET_DK ** -0.5)
        a["rv"] = kv[:, RET_W:]

    def a_z():
        a["dn_z"] = proj(0, 512)

    def a_g():
        a["ret_g"] = proj(1792, 2048)

    def a_gates():
        abt = lax.dot_general(wabt_ref[...], a["u"], (((1,), (1,)), ((), ())), preferred_element_type=F32)
        g_row = -jnp.exp(alog_c_ref[...]) * _softplus(abt + dtb_c_ref[...])
        rr = _iota((tc, tc), 0)
        cc = _iota((tc, tc), 1)
        upper = jnp.where(((rr // ck) == (cc // ck)) & (rr <= cc), 1.0, 0.0)
        big_g_row = _mm_01_right(g_row, upper)
        is_g = _iota((8, tc), 0) < DN_HEADS
        gb_s[...] = jnp.where(is_g, big_g_row, _sigmoid(abt)).T
        for c in chunks:
            grow_s[c] = jnp.concatenate(
                [big_g_row[h:h + 1, c * ck:(c + 1) * ck] for h in range(DN_HEADS)], axis=1)

    def a_lru_scan_steps():
        ga, gb = _lru_coeffs(a["xl"], wgate_ref[...], bgate_ref[...], lam_ref[...])
        row = _iota((tc, LRU_W), 0)
        gb = gb + jnp.where(row == 0, ga * hst[...], 0.0)
        yield
        s = 1
        while s < tc:
            keep = row >= s
            gb = ga * jnp.where(keep, pltpu.roll(gb, s, 0), 0.0) + gb
            ga = ga * jnp.where(keep, pltpu.roll(ga, s, 0), 1.0)
            s *= 2
            yield
        hst[...] = gb[tc - 1:tc, :]
        a["o_lru"] = (gb * jax.nn.gelu(a["lru_y"])).astype(BF16)
        yield

    def a_kkqk():
        def stack(ref, c, g):
            return jnp.concatenate([ref[h, rows[c], :] for h in range(g * hg, (g + 1) * hg)], axis=0)

        def stack_col(c, g, first):
            return jnp.concatenate([gb_s[rows[c], first + h:first + h + 1]
                                    for h in range(g * hg, (g + 1) * hg)], axis=0)

        a["kst"] = [stack(k_s, c, g) for c, g in probs]
        a["qst"] = [stack(q_s, c, g) for c, g in probs]
        a["vst"] = [stack(v_s, c, g) for c, g in probs]
        gst = [stack_col(c, g, 0) for c, g in probs]
        a["gst"] = gst
        a["bst"] = [stack_col(c, g, DN_HEADS) for c, g in probs]
        dstrict = []
        for k, (c, g) in enumerate(probs):
            diff = gst[k] - grow_s[c][:, g * gc:(g + 1) * gc]
            dstrict.append(jnp.where(strict, jnp.exp(jnp.where(strict, diff, 0.0)), 0.0))
        a["eg"] = [jnp.exp(g) for g in gst]
        a["amat"] = [a["bst"][k] * _mm_nt(a["kst"][k], a["kst"][k]) * dstrict[k] for k in range(len(probs))]
        a["pmat"] = [(_mm_nt(a["qst"][k], a["kst"][k]) * (dstrict[k] + eye_g)).astype(BF16)
                     for k in range(len(probs))]

    tinv = []

    def a_sol():
        kst, qst, vst, gst, bst, eg = (a[k] for k in ("kst", "qst", "vst", "gst", "bst", "eg"))
        sol = [_mm(tinv[k], jnp.concatenate([bst[k] * vst[k], (bst[k] * eg[k]) * kst[k]], axis=1))
               for k in range(len(probs))]
        a["uv"] = [s[:, :DN_DK] for s in sol]
        wq, egl, kd = {}, {}, {}
        for k, (c, g) in enumerate(probs):
            for j in range(hg):
                lo, hi = j * ck, (j + 1) * ck
                key = (c, g * hg + j)
                wq[key] = jnp.concatenate([sol[k][lo:hi, DN_DK:], eg[k][lo:hi] * qst[k][lo:hi]],
                                          axis=0).astype(BF16)
                g_last = gst[k][hi - 1:hi]
                egl[key] = jnp.exp(g_last)
                kd[key] = (kst[k][lo:hi] * jnp.exp(g_last - gst[k][lo:hi])).astype(BF16)
        a["wq"], a["egl"], a["kd"] = wq, egl, kd

    def a_ret_scores():
        qc = [a["rq"][rs] for rs in rows]
        kc = [a["rk"][rs] for rs in rows]
        qp = [jnp.where(head_sel, jnp.concatenate([q] * RET_HEADS, axis=0), 0.0) for q in qc]
        a["qc"] = qc
        a["sc"] = [(_mm_nt(qp[c], kc[c]) * ret_dec).astype(BF16) for c in chunks]

    def a_ret_intra():
        vc = [a["rv"][rs].astype(BF16) for rs in rows]
        o_intra = []
        for c in chunks:
            op = jnp.where(head_sel, jnp.dot(a["sc"][c], vc[c], preferred_element_type=F32), 0.0)
            acc = op[0:ck]
            for h in range(1, RET_HEADS):
                acc = acc + op[h * ck:(h + 1) * ck]
            o_intra.append(acc)
        a["o_intra"] = o_intra
        a["s_upd"] = [jnp.where(bd, _mm_tn(a["rk"][rows[c]] * ret_kdec, vc[c]), 0.0) for c in chunks]

    def a_handoff():
        for k, (c, g) in enumerate(probs):
            h_uv[c, g * gc:(g + 1) * gc, :] = a["uv"][k]
            h_pm[c, g] = a["pmat"][k]
        for c in chunks:
            h_rq[c] = a["qc"][c].astype(BF16)
            h_oi[c] = a["o_intra"][c]
            h_su[c] = a["s_upd"][c]
            for h in range(DN_HEADS):
                h_wq[c, h] = a["wq"][c, h]
                h_kd[c, h] = a["kd"][c, h]
                h_egl[c, h:h + 1, :] = jnp.broadcast_to(a["egl"][c, h], (1, DN_DK))
        h_z[...] = a["dn_z"]
        h_rg[...] = a["ret_g"]
        h_x[...] = a["x"]
        h_lru[...] = a["o_lru"]

    def b_post_a(c):
        ws_qs = [_mm(h_wq[c, h], sdn[h]) for h in range(DN_HEADS)]
        b["u"] = [h_uv[c, lo:hi, :] - ws_qs[h][:ck] for h, (lo, hi) in enumerate(heads)]
        b["qs"] = [w[ck:] for w in ws_qs]

    def b_post_b(c):
        for g in range(ng):
            group = range(g * hg, (g + 1) * hg)
            ost = (jnp.concatenate([b["qs"][h] for h in group], axis=0)
                   + _mm(h_pm[c, g], jnp.concatenate([b["u"][h] for h in group], axis=0)))
            for j, h in enumerate(group):
                odn_s[rows[c], h * DN_DK:(h + 1) * DN_DK] = ost[j * ck:(j + 1) * ck]
        for h in range(DN_HEADS):
            sdn[h] = h_egl[c, h:h + 1, :] * sdn[h] + _mm_tn(h_kd[c, h], b["u"][h])

    def b_ret():
        for c in chunks:
            s0 = sret[...]
            oret_s[rows[c], :] = ret_eg * _mm(h_rq[c], s0) + h_oi[c]
            sret[...] = ret_gc * s0 + h_su[c]

    def b_out():
        nw = dnnw_ref[...]
        for h in range(DN_HEADS):
            sl = slice(h * DN_DK, (h + 1) * DN_DK)
            oh = odn_s[:, sl]
            oh = oh * lax.rsqrt(jnp.mean(oh * oh, axis=-1, keepdims=True) + EPS) * nw
            omix_s[:, sl] = (oh * _silu(h_z[:, sl])).astype(BF16)
        omix_s[:, 512:768] = h_lru[...]
        omix_s[:, 768:1024] = _ret_norm_gate(oret_s[...], h_rg[...], _head_avg_matrix()).astype(BF16)
        y_ref[...] = h_x[...] + jnp.dot(omix_s[...], wout_ref[...], preferred_element_type=F32)

    n_piece = 3 * DN_W // (2 * DN_DK)
    a_seq = [a_norm]
    for p in range(n_piece):
        a_seq.append(lambda p=p: (a_qkv_piece(p), a_conv_piece(p)))
    a_seq += [a_lru_in, lambda: (a_gates(), a_ret_q()), a_ret_kv, a_z, a_g, a_kkqk]
    b_seq = []
    for c in chunks:
        b_seq += [functools.partial(b_post_a, c), functools.partial(b_post_b, c)]
    b_seq += [b_ret, b_out]
    b_seq.pop(0)()
    for fa in a_seq:
        fa()
        if b_seq:
            b_seq.pop(0)()
    for fb in b_seq:
        fb()
    scan = a_lru_scan_steps()
    for _ in _tri_inv_steps(a["amat"], n_sub, tinv):
        next(scan, None)
    for _ in scan:
        pass
    a_sol()
    a_ret_scores()
    a_ret_intra()
    a_handoff()

    @pl.when((t_a == nt - 1) & (i < n_blocks))
    def _():
        dnt_o_ref[0] = dn_tail[...]
        h_o_ref[0] = hst[...]
        lrt_o_ref[0] = lru_tail[...]

    @pl.when((t_b == nt - 1) & (i > 0))
    def _():
        sdn_o_ref[0] = sdn[...]
        s_bd = sret[...]
        sretbd_o_ref[0] = s_bd
        for h in range(RET_HEADS):
            sret_o_ref[0, h] = s_bd[h * RET_DK:(h + 1) * RET_DK, h * RET_DK:(h + 1) * RET_DK]


def _mix_pipe(x, lw, cos, sin, init, *, batch, seq, tc, ck):
    nt = seq // tc
    nc = tc // ck
    hc = DN_HEADS * ck
    hg = _heads_per_group(ck)
    ng, gc = DN_HEADS // hg, hg * ck
    n_blocks = batch * nt
    blk_a = lambda i: jnp.minimum(i, n_blocks - 1)
    blk_b = lambda i: jnp.maximum(i - 1, 0)
    const = lambda shape: pl.BlockSpec(shape, lambda i: (0,) * len(shape))
    in_specs = [
        pl.BlockSpec((tc, D_MODEL), lambda i: (blk_a(i), 0)),
        const((1, D_MODEL)), const((D_MODEL, 3 * DN_W)), const((8, D_MODEL)),
        const((D_MODEL, 2048)), const((CONV_W, 3 * DN_W)),
        const((8, 1)), const((8, 1)), const((1, DN_DK)),
        const((CONV_W, LRU_W)), const((1, LRU_W)), const((LRU_W, 2 * LRU_W)), const((1, 2 * LRU_W)),
        const((1, LRU_W)),
        pl.BlockSpec((tc, RET_W), lambda i: (blk_a(i) % nt, 0)),
        pl.BlockSpec((tc, RET_W), lambda i: (blk_a(i) % nt, 0)),
        const((D_MODEL, D_MODEL)),
        const((DN_HEADS, DN_DK, DN_DK)), const((8, 3 * DN_W)), const((1, LRU_W)), const((8, LRU_W)),
        const((RET_W, RET_W)),
    ]
    out_shape = [
        jax.ShapeDtypeStruct((batch * seq, D_MODEL), F32),
        jax.ShapeDtypeStruct((batch, DN_HEADS, DN_DK, DN_DK), F32),
        jax.ShapeDtypeStruct((batch, 8, 3 * DN_W), F32),
        jax.ShapeDtypeStruct((batch, 1, LRU_W), F32),
        jax.ShapeDtypeStruct((batch, 8, LRU_W), F32),
        jax.ShapeDtypeStruct((batch, RET_W, RET_W), F32),
        jax.ShapeDtypeStruct((batch, RET_HEADS, RET_DK, RET_DK), F32),
    ]
    seq_a = lambda i: blk_a(i) // nt
    seq_b = lambda i: blk_b(i) // nt
    out_specs = [
        pl.BlockSpec((tc, D_MODEL), lambda i: (blk_b(i), 0)),
        pl.BlockSpec((1, DN_HEADS, DN_DK, DN_DK), lambda i: (seq_b(i), 0, 0, 0)),
        pl.BlockSpec((1, 8, 3 * DN_W), lambda i: (seq_a(i), 0, 0)),
        pl.BlockSpec((1, 1, LRU_W), lambda i: (seq_a(i), 0, 0)),
        pl.BlockSpec((1, 8, LRU_W), lambda i: (seq_a(i), 0, 0)),
        pl.BlockSpec((1, RET_W, RET_W), lambda i: (seq_b(i), 0, 0)),
        pl.BlockSpec((1, RET_HEADS, RET_DK, RET_DK), lambda i: (seq_b(i), 0, 0, 0)),
    ]
    scratch = [
        pltpu.VMEM((DN_HEADS, DN_DK, DN_DK), F32),
        pltpu.VMEM((RET_W, RET_W), F32),
        pltpu.VMEM((1, LRU_W), F32),
        pltpu.VMEM((8, 3 * DN_W), F32),
        pltpu.VMEM((8, LRU_W), F32),
        pltpu.VMEM((DN_HEADS, tc, DN_DK), F32),
        pltpu.VMEM((DN_HEADS, tc, DN_DK), F32),
        pltpu.VMEM((DN_HEADS, tc, DN_DK), F32),
        pltpu.VMEM((tc, 8), F32),
        pltpu.VMEM((nc, 1, hc), F32),
        pltpu.VMEM((tc, DN_W), F32),
        pltpu.VMEM((tc, RET_W), F32),
        pltpu.VMEM((tc, D_MODEL), BF16),
        pltpu.VMEM((nc, hc, DN_DK), F32),
        pltpu.VMEM((nc, DN_HEADS, 2 * ck, DN_DK), BF16),
        pltpu.VMEM((nc, ng, gc, gc), BF16),
        pltpu.VMEM((nc, DN_HEADS, ck, DN_DK), BF16),
        pltpu.VMEM((nc, 8, DN_DK), F32),
        pltpu.VMEM((nc, ck, RET_W), BF16),
        pltpu.VMEM((nc, ck, RET_W), F32),
        pltpu.VMEM((nc, RET_W, RET_W), F32),
        pltpu.VMEM((tc, DN_W), F32),
        pltpu.VMEM((tc, RET_W), F32),
        pltpu.VMEM((tc, D_MODEL), F32),
        pltpu.VMEM((tc, LRU_W), BF16),
    ]
    args = [x, lw["norm_mix"], lw["wqkv"], lw["wabt"], lw["wrest"], lw["dn_conv_w"],
            lw["alog_c"], lw["dtb_c"], lw["dn_norm_w"],
            lw["lru_conv_w"], lw["lru_conv_b"], lw["wgate"], lw["bgate"], lw["lam"], cos, sin,
            lw["w_out"], *init]
    return pl.pallas_call(
        functools.partial(_mix_pipe_body, tc=tc, ck=ck, nt=nt, n_blocks=n_blocks),
        grid=(n_blocks + 1,),
        in_specs=in_specs,
        out_specs=out_specs,
        out_shape=out_shape,
        scratch_shapes=scratch,
        compiler_params=pltpu.CompilerParams(
            dimension_semantics=("arbitrary",), vmem_limit_bytes=VMEM_LIMIT_BYTES),
        name=f"mix_pipe_c{ck}",
    )(*args)


def _mix_step_body(*refs, bb, layer):
    (x_ref, nw_ref, wqkv_ref, wab_ref, wrest_ref, dncw_ref, alog_r_ref, dtb_r_ref,
     dnnw_ref, lcw_ref, lcb_ref, wgate_ref, bgate_ref, lam_ref, cos_ref, sin_ref,
     sdn_ref, dnc_ref, h_ref, lrc_ref) = refs[:20]
    refs = refs[20:]
    if layer > 0:
        prev_sdn_ref = refs[0]
        refs = refs[1:]
    (omix_o_ref, retin_o_ref, sdn_o_ref, dnc_o_ref, h_o_ref, lrc_o_ref,
     q_s, k_s, v_s, eg_s, beta_s, odn_s) = refs
    if layer > 0:
        sdn_o_ref[0:layer] = prev_sdn_ref[...]
    x = x_ref[...]
    u = _rmsnorm(x, nw_ref[...]).astype(BF16)

    qkv_pre = jnp.dot(u, wqkv_ref[...], preferred_element_type=F32)
    cw = dncw_ref[...]
    conv = (dnc_ref[0] * cw[0:1, :] + dnc_ref[1] * cw[1:2, :] + dnc_ref[2] * cw[2:3, :]
            + qkv_pre * cw[3:4, :])
    dnc_o_ref[0] = dnc_ref[1]
    dnc_o_ref[1] = dnc_ref[2]
    dnc_o_ref[2] = qkv_pre
    qkv = _silu(conv)
    for h in range(DN_HEADS):
        sl = slice(h * DN_DK, (h + 1) * DN_DK)
        qh = qkv[:, sl]
        kh = qkv[:, DN_W + h * DN_DK:DN_W + (h + 1) * DN_DK]
        q_s[:, sl] = qh * lax.rsqrt(jnp.sum(qh * qh, axis=-1, keepdims=True) + EPS) * (DN_DK ** -0.5)
        k_s[:, sl] = kh * lax.rsqrt(jnp.sum(kh * kh, axis=-1, keepdims=True) + EPS)
    v_s[...] = qkv[:, 2 * DN_W:]
    ab = jnp.dot(u, wab_ref[...], preferred_element_type=F32)
    eg_s[...] = jnp.exp(-jnp.exp(alog_r_ref[...]) * _softplus(ab[:, :128] + dtb_r_ref[...]))
    beta_s[...] = _sigmoid(ab[:, 128:])

    rest = jnp.dot(u, wrest_ref[...], preferred_element_type=F32)
    dn_z = rest[:, 0:512]
    lru_x = rest[:, 512:768]
    lru_y = rest[:, 768:1024]
    ret_g = rest[:, 1792:2048]

    lw = lcw_ref[...]
    xl = (lrc_ref[0] * lw[0:1, :] + lrc_ref[1] * lw[1:2, :] + lrc_ref[2] * lw[2:3, :]
          + lru_x * lw[3:4, :] + lcb_ref[...])
    lrc_o_ref[0] = lrc_ref[1]
    lrc_o_ref[1] = lrc_ref[2]
    lrc_o_ref[2] = lru_x
    a, b = _lru_coeffs(xl, wgate_ref[...], bgate_ref[...], lam_ref[...])
    h_new = a * h_ref[...] + b
    h_o_ref[...] = h_new
    o_lru = h_new * jax.nn.gelu(lru_y)

    cos = cos_ref[...]
    sin = sin_ref[...]
    retin_o_ref[:, 0:256] = _rope(rest[:, 1024:1280], cos, sin)
    retin_o_ref[:, 256:512] = _rope(rest[:, 1280:1536], cos, sin) * (RET_DK ** -0.5)
    retin_o_ref[:, 512:768] = rest[:, 1536:1792]
    retin_o_ref[:, 768:1024] = ret_g

    def per_tile(i8, carry):
        r0 = pl.multiple_of(i8 * 8, 8)
        eg_t = eg_s[pl.ds(r0, 8), :]
        beta_t = beta_s[pl.ds(r0, 8), :]
        for h in range(DN_HEADS):
            sl = slice(h * DN_DK, (h + 1) * DN_DK)
            kt = k_s[pl.ds(r0, 8), sl].T
            qt = q_s[pl.ds(r0, 8), sl].T
            v_t = v_s[pl.ds(r0, 8), sl]
            rows = []
            for j in range(8):
                kcol = kt[:, j:j + 1]
                qcol = qt[:, j:j + 1]
                s0 = sdn_ref[r0 + j, h]
                eg = eg_t[j:j + 1, h:h + 1]
                ks = jnp.sum(s0 * kcol, axis=0, keepdims=True)
                qs = jnp.sum(s0 * qcol, axis=0, keepdims=True)
                qk = jnp.sum(qcol * kcol, axis=0, keepdims=True)
                uu = beta_t[j:j + 1, h:h + 1] * (v_t[j:j + 1, :] - eg * ks)
                rows.append(eg * qs + qk * uu)
                sdn_o_ref[layer, r0 + j, h] = eg * s0 + kcol * uu
            odn_s[pl.ds(r0, 8), sl] = jnp.concatenate(rows, axis=0)
        return carry

    lax.fori_loop(0, bb // 8, per_tile, 0)

    nw = dnnw_ref[...]
    for h in range(DN_HEADS):
        sl = slice(h * DN_DK, (h + 1) * DN_DK)
        oh = odn_s[:, sl]
        oh = oh * lax.rsqrt(jnp.mean(oh * oh, axis=-1, keepdims=True) + EPS) * nw
        omix_o_ref[:, sl] = (oh * _silu(dn_z[:, sl])).astype(BF16)
    omix_o_ref[:, DN_W:DN_W + LRU_W] = o_lru.astype(BF16)


def _mix_step(x, lw, cos, sin, layer, sdn, dnc, hl, lrc, prev_sdn, *, bb):
    n = x.shape[0]
    const = lambda shape: pl.BlockSpec(shape, lambda i: (0,) * len(shape))
    in_specs = [
        pl.BlockSpec((bb, D_MODEL), lambda i: (i, 0)),
        const((1, D_MODEL)), const((D_MODEL, 3 * DN_W)), const((D_MODEL, 256)), const((D_MODEL, 2048)),
        const((CONV_W, 3 * DN_W)), const((1, 128)), const((1, 128)), const((1, DN_DK)),
        const((CONV_W, LRU_W)), const((1, LRU_W)), const((LRU_W, 2 * LRU_W)), const((1, 2 * LRU_W)),
        const((1, LRU_W)), const((1, RET_W)), const((1, RET_W)),
        pl.BlockSpec((None, bb, DN_HEADS, DN_DK, DN_DK), lambda i: (layer, i, 0, 0, 0)),
        pl.BlockSpec((CONV_W - 1, bb, 3 * DN_W), lambda i: (0, i, 0)),
        pl.BlockSpec((None, bb, LRU_W), lambda i: (layer, i, 0)),
        pl.BlockSpec((CONV_W - 1, bb, LRU_W), lambda i: (0, i, 0)),
    ]
    if layer > 0:
        in_specs.append(pl.BlockSpec((layer, bb, DN_HEADS, DN_DK, DN_DK), lambda i: (0, i, 0, 0, 0)))
    out_shape = [
        jax.ShapeDtypeStruct((n, DN_W + LRU_W), BF16),
        jax.ShapeDtypeStruct((n, 4 * RET_W), F32),
        jax.ShapeDtypeStruct((layer + 1, n, DN_HEADS, DN_DK, DN_DK), F32),
        jax.ShapeDtypeStruct((CONV_W - 1, n, 3 * DN_W), F32),
        jax.ShapeDtypeStruct((n, LRU_W), F32),
        jax.ShapeDtypeStruct((CONV_W - 1, n, LRU_W), F32),
    ]
    out_specs = [
        pl.BlockSpec((bb, DN_W + LRU_W), lambda i: (i, 0)),
        pl.BlockSpec((bb, 4 * RET_W), lambda i: (i, 0)),
        pl.BlockSpec((layer + 1, bb, DN_HEADS, DN_DK, DN_DK), lambda i: (0, i, 0, 0, 0)),
        pl.BlockSpec((CONV_W - 1, bb, 3 * DN_W), lambda i: (0, i, 0)),
        pl.BlockSpec((bb, LRU_W), lambda i: (i, 0)),
        pl.BlockSpec((CONV_W - 1, bb, LRU_W), lambda i: (0, i, 0)),
    ]
    scratch = [
        pltpu.VMEM((bb, DN_W), F32), pltpu.VMEM((bb, DN_W), F32), pltpu.VMEM((bb, DN_W), F32),
        pltpu.VMEM((bb, 128), F32), pltpu.VMEM((bb, 128), F32), pltpu.VMEM((bb, DN_W), F32),
    ]
    args = [x, lw["norm_mix"], lw["wqkv"], lw["wab"], lw["wrest"], lw["dn_conv_w"], lw["alog_r"],
            lw["dtb_r"], lw["dn_norm_w"], lw["lru_conv_w"], lw["lru_conv_b"], lw["wgate"], lw["bgate"],
            lw["lam"], cos, sin, sdn, dnc, hl, lrc]
    if layer > 0:
        args.append(prev_sdn)
    return pl.pallas_call(
        functools.partial(_mix_step_body, bb=bb, layer=layer),
        grid=(n // bb,),
        in_specs=in_specs,
        out_specs=out_specs,
        out_shape=out_shape,
        scratch_shapes=scratch,
        compiler_params=pltpu.CompilerParams(
            dimension_semantics=("arbitrary",), vmem_limit_bytes=VMEM_LIMIT_BYTES),
        name="mix_step",
    )(*args)


def _ret_step_body(*refs, layer):
    x_ref, omix_ref, retin_ref, wout_ref, s_ref = refs[:5]
    refs = refs[5:]
    if layer > 0:
        prev_ref = refs[0]
        refs = refs[1:]
    y_ref, s_o_ref, qt_s, kt_s, vt_s, ot_s = refs
    h = pl.program_id(0)
    if layer > 0:
        s_o_ref[0:layer] = prev_ref[...]

    @pl.when(h == 0)
    def _():
        qt_s[...] = retin_ref[:, 0:RET_W].T
        kt_s[...] = retin_ref[:, RET_W:2 * RET_W].T
        vt_s[...] = retin_ref[:, 2 * RET_W:3 * RET_W].T

    rows = pl.ds(pl.multiple_of(h * RET_DK, RET_DK), RET_DK)
    q = qt_s[rows, :]
    k = kt_s[rows, :]
    v = vt_s[rows, :]
    gamma = jnp.float32(math.exp(LOG_GAMMA[0]))
    for i in range(1, RET_HEADS):
        gamma = jnp.where(h == i, jnp.float32(math.exp(LOG_GAMMA[i])), gamma)
    acc = jnp.zeros_like(v)
    for d in range(RET_DK):
        s_d = s_ref[d]
        acc = acc + s_d * q[d:d + 1, :]
        s_o_ref[layer, 0, d] = gamma * s_d + k[d:d + 1, :] * v
    ot_s[rows, :] = gamma * acc + jnp.sum(q * k, axis=0, keepdims=True) * v

    @pl.when(h == pl.num_programs(0) - 1)
    def _():
        o_ret = _ret_norm_gate(ot_s[...].T, retin_ref[:, 3 * RET_W:], _head_avg_matrix())
        omix = jnp.concatenate([omix_ref[...], o_ret.astype(BF16)], axis=1)
        y_ref[...] = x_ref[...] + jnp.dot(omix, wout_ref[...], preferred_element_type=F32)


def _ret_step(x, omix_part, retin, w_out, sret_t, layer, prev, *, n):
    const = lambda shape: pl.BlockSpec(shape, lambda h: (0,) * len(shape))
    in_specs = [
        const((n, D_MODEL)), const((n, DN_W + LRU_W)), const((n, 4 * RET_W)), const((D_MODEL, D_MODEL)),
        pl.BlockSpec((None, None, RET_DK, RET_DK, n), lambda h: (layer, h, 0, 0, 0)),
    ]
    args = [x, omix_part, retin, w_out, sret_t]
    if layer > 0:
        in_specs.append(pl.BlockSpec((layer, 1, RET_DK, RET_DK, n), lambda h: (0, h, 0, 0, 0)))
        args.append(prev)
    return pl.pallas_call(
        functools.partial(_ret_step_body, layer=layer),
        grid=(RET_HEADS,),
        in_specs=in_specs,
        out_specs=[const((n, D_MODEL)),
                   pl.BlockSpec((layer + 1, 1, RET_DK, RET_DK, n), lambda h: (0, h, 0, 0, 0))],
        out_shape=[jax.ShapeDtypeStruct((n, D_MODEL), F32),
                   jax.ShapeDtypeStruct((layer + 1, RET_HEADS, RET_DK, RET_DK, n), F32)],
        scratch_shapes=[pltpu.VMEM((RET_W, n), F32)] * 4,
        compiler_params=pltpu.CompilerParams(
            dimension_semantics=("arbitrary",), vmem_limit_bytes=VMEM_LIMIT_BYTES),
        name="ret_step",
    )(*args)


def _rope_tables(pos):
    half = RET_DK // 2
    inv = ROPE_BASE ** (-jnp.arange(half, dtype=F32) / half)
    ang = pos.astype(F32)[:, None] * inv[None, :]
    cos = jnp.cos(ang)
    sin = jnp.sin(ang)
    cos_full = jnp.tile(jnp.concatenate([cos, cos], axis=-1), (1, RET_HEADS))
    sin_signed = jnp.tile(jnp.concatenate([-sin, sin], axis=-1), (1, RET_HEADS))
    return cos_full, sin_signed


def _block_diag(w):
    n, d, e = w.shape
    eye = jnp.eye(n, dtype=w.dtype)
    return (eye[:, None, :, None] * w[:, :, None, :]).reshape(n * d, n * e)


def _layer_weights(l, norm_mix, w_in, dn_conv_w, dn_a_log, dn_dt_bias, dn_norm_w, lru_conv_w, lru_conv_b,
                   lru_wa, lru_ba, lru_wx, lru_bx, lru_lambda, w_out):
    wl = w_in[l]
    o_a = 3 * DN_W
    wa = wl[:, o_a:o_a + DN_HEADS]
    wb = wl[:, o_a + DN_HEADS:o_a + 2 * DN_HEADS]
    pad_cols = lambda w: jnp.pad(w, ((0, 0), (0, 128 - DN_HEADS)))
    pad_lane = lambda v: jnp.pad(v, (0, 128 - DN_HEADS)).reshape(1, 128)
    pad_sub = lambda v: jnp.pad(v, (0, 8 - DN_HEADS)).reshape(8, 1)
    return {
        "norm_mix": norm_mix[l].reshape(1, D_MODEL),
        "wqkv": wl[:, :o_a].astype(BF16),
        "wab": jnp.concatenate([pad_cols(wa), pad_cols(wb)], axis=1).astype(BF16),
        "wabt": jnp.concatenate([wa, wb], axis=1).T.astype(BF16),
        "wrest": wl[:, o_a + 2 * DN_HEADS:].astype(BF16),
        "dn_conv_w": dn_conv_w[l],
        "alog_r": pad_lane(dn_a_log[l]), "dtb_r": pad_lane(dn_dt_bias[l]),
        "alog_c": pad_sub(dn_a_log[l]), "dtb_c": pad_sub(dn_dt_bias[l]),
        "dn_norm_w": dn_norm_w[l].reshape(1, DN_DK),
        "lru_conv_w": lru_conv_w[l], "lru_conv_b": lru_conv_b[l].reshape(1, LRU_W),
        "wgate": jnp.concatenate([_block_diag(lru_wa[l]), _block_diag(lru_wx[l])], axis=1).astype(BF16),
        "bgate": jnp.concatenate([lru_ba[l], lru_bx[l]]).reshape(1, 2 * LRU_W),
        "lam": lru_lambda[l].reshape(1, LRU_W),
        "w_out": w_out[l].astype(BF16),
    }


def kernel(x_prompt, x_sample, state_dn, state_dn_conv, state_lru, state_lru_conv, state_ret, meta_tokens, norm_ffn1, w_ffn1_in, w_ffn1_out, norm_mix, w_in, dn_conv_w, dn_a_log, dn_dt_bias, dn_norm_w, lru_conv_w, lru_conv_b, lru_wa, lru_ba, lru_wx, lru_bx, lru_lambda, w_out, norm_ffn2, w_ffn2_in, w_ffn2_out, norm_final):
    batch, seq, _ = x_prompt.shape
    n_dec = x_sample.shape[0]
    depth = w_in.shape[0]
    lws = [_layer_weights(l, norm_mix, w_in, dn_conv_w, dn_a_log, dn_dt_bias, dn_norm_w, lru_conv_w,
                          lru_conv_b, lru_wa, lru_ba, lru_wx, lru_bx, lru_lambda, w_out)
           for l in range(depth)]
    cos_m, sin_m = _rope_tables(jnp.arange(N_META))
    cos_p, sin_p = _rope_tables(N_META + jnp.arange(seq))
    cos_s, sin_s = _rope_tables(PAST_LEN + jnp.arange(1))

    n_small = n_dec + N_META
    xs = jnp.concatenate([x_sample[:, 0, :], meta_tokens.astype(F32)], axis=0)
    zero_init = (jnp.zeros((DN_HEADS, DN_DK, DN_DK), F32), jnp.zeros((8, 3 * DN_W), F32),
                 jnp.zeros((1, LRU_W), F32), jnp.zeros((8, LRU_W), F32), jnp.zeros((RET_W, RET_W), F32))
    new_s, meta_state, ffn_w = [], [], []
    sdn_all = sret_all = None
    sret_t = jnp.transpose(state_ret, (0, 2, 3, 4, 1))
    for l in range(depth):
        last = l == depth - 1
        xs, *f1 = _ffn_cast(xs, norm_ffn1[l], w_ffn1_in, w_ffn1_out, l, tf=FFN_DFF_SLICE)
        omix_part, retin, sdn_all, dnc, hl, lrc = _mix_step(
            xs[:n_dec], lws[l], cos_s, sin_s, l, state_dn, jnp.swapaxes(state_dn_conv[l], 0, 1),
            state_lru, jnp.swapaxes(state_lru_conv[l], 0, 1), sdn_all, bb=DECODE_BATCH_TILE)
        ys, sret_all = _ret_step(xs[:n_dec], omix_part, retin, lws[l]["w_out"], sret_t, l, sret_all,
                                 n=n_dec)
        ym, m_sdn, m_dnt, m_h, m_lrt, m_sretbd, _ = _mix_pipe(
            xs[n_dec:], lws[l], cos_m, sin_m, zero_init, batch=1, seq=N_META, tc=N_META, ck=N_META)
        new_s.append((jnp.swapaxes(dnc, 0, 1), hl, jnp.swapaxes(lrc, 0, 1)))
        meta_state.append((m_sdn[0], m_dnt[0], m_h[0], m_lrt[0], m_sretbd[0]))
        xs = jnp.concatenate([ys, ym], axis=0)
        xs, *f2 = _ffn_cast(xs, norm_ffn2[l], w_ffn2_in, w_ffn2_out, l, norm_final if last else None,
                            tf=FFN_DFF_SLICE)
        ffn_w.append((f1, f2))
    y_sample = xs[:n_dec].reshape(n_dec, 1, D_MODEL)

    xp = x_prompt.reshape(batch * seq, D_MODEL)
    new_p = []
    for l in range(depth):
        last = l == depth - 1
        xp = _ffn(xp, norm_ffn1[l], *ffn_w[l][0], tm=FFN_TOKEN_TILE, nf=D_FF // FFN_DFF_SLICE)
        xp, sdn, dnt, hl, lrt, _, sret = _mix_pipe(
            xp, lws[l], cos_p, sin_p, meta_state[l], batch=batch, seq=seq, tc=MIX_TIME_BLOCK, ck=CHUNK)
        new_p.append((sdn, dnt[:, 5:8], hl[:, 0], lrt[:, 5:8], sret))
        xp = _ffn(xp, norm_ffn2[l], *ffn_w[l][1], norm_final if last else None, tm=FFN_TOKEN_TILE,
                  nf=D_FF // FFN_DFF_SLICE)
    y_prompt = xp.reshape(batch, seq, D_MODEL)

    outs_p = [jnp.stack([s[j] for s in new_p]) for j in range(5)]
    dnc_s, lru_s, lrc_s = [jnp.stack([s[j] for s in new_s]) for j in range(3)]
    sret_s = jnp.transpose(sret_all, (0, 4, 1, 2, 3))
    return (y_prompt, y_sample, *outs_p, sdn_all, dnc_s, lru_s, lrc_s, sret_s)
```

```python
import functools
import math

import jax
import jax.numpy as jnp
from jax import lax
from jax.experimental import pallas as pl
from jax.experimental.pallas import tpu as pltpu

F32 = jnp.float32
BF16 = jnp.bfloat16

D_MODEL = 1024
N_META = 16
PAST_LEN = 16384
DN_HEADS = 4
DN_DK = 128
DN_W = 512
LRU_W = 256
LRU_BLOCKS = 4
LRU_C = 8.0
RET_HEADS = 4
RET_DK = 64
RET_W = 256
CONV_W = 4
CHUNK = 64
D_FF = 2816
ROPE_BASE = 10000.0
EPS = 1e-6
SUB = 16
LOG_GAMMA = tuple(math.log1p(-2.0 ** (-5.0 - h)) for h in range(RET_HEADS))

V7X_VMEM_BYTES = 64 * 1024 * 1024
VMEM_LIMIT_BYTES = V7X_VMEM_BYTES * 7 // 8
V7X_MXU_WIDTH = 256

FFN_TOKEN_TILE = 1024
FFN_DFF_SLICE = V7X_MXU_WIDTH
MIX_TIME_BLOCK = 4 * CHUNK
DECODE_BATCH_TILE = 8


def _mm(a, b):
    return jnp.dot(a.astype(BF16), b.astype(BF16), preferred_element_type=F32)


def _mm_nt(a, b):
    return lax.dot_general(a.astype(BF16), b.astype(BF16), (((1,), (1,)), ((), ())),
                           preferred_element_type=F32)


def _mm_tn(a, b):
    return lax.dot_general(a.astype(BF16), b.astype(BF16), (((0,), (0,)), ((), ())),
                           preferred_element_type=F32)


def _split3(x):
    hi = x.astype(BF16)
    r = x - hi.astype(F32)
    mid = r.astype(BF16)
    lo = (r - mid.astype(F32)).astype(BF16)
    return hi, mid, lo


def _mm_01_right(x, m01):
    m = m01.astype(BF16)
    return sum(jnp.dot(p, m, preferred_element_type=F32) for p in _split3(x))


def _mm_split(x, m_bf16):
    hi = x.astype(BF16)
    lo = (x - hi.astype(F32)).astype(BF16)
    return (jnp.dot(hi, m_bf16, preferred_element_type=F32)
            + jnp.dot(lo, m_bf16, preferred_element_type=F32))


def _rmsnorm(x, w):
    return x * lax.rsqrt(jnp.mean(x * x, axis=-1, keepdims=True) + EPS) * w


def _sigmoid(x):
    return 0.5 + 0.5 * jnp.tanh(0.5 * x)


def _silu(x):
    return x * _sigmoid(x)


def _causal_conv(x, tail, w):
    tc, n = x.shape
    nt8 = tc // 8
    first = _iota((nt8, 8, n), 1) == 0
    xm1, xm2, xm3 = tail[7:8], tail[6:7], tail[5:6]

    def shift1(s, carry):
        r = pltpu.roll(s.reshape(nt8, 8, n), 1, 1)
        before = jnp.concatenate([jnp.broadcast_to(carry, (1, 8, n)), r[:-1]], axis=0)
        return jnp.where(first, before, r).reshape(tc, n)

    s = shift1(x * w[0:1], xm1 * w[0:1])
    s = shift1(s + x * w[1:2], xm2 * w[0:1] + xm1 * w[1:2])
    s = shift1(s + x * w[2:3], xm3 * w[0:1] + xm2 * w[1:2] + xm1 * w[2:3])
    return s + x * w[3:4]


def _softplus(x):
    return jnp.maximum(x, 0.0) + jnp.log1p(jnp.exp(-jnp.abs(x)))


def _iota(shape, dim):
    return lax.broadcasted_iota(jnp.int32, shape, dim)


def _lane_log_gamma(shape, dim, width):
    head = _iota(shape, dim) // width
    out = jnp.full(shape, LOG_GAMMA[0], F32)
    for h in range(1, RET_HEADS):
        out = jnp.where(head == h, LOG_GAMMA[h], out)
    return out


def _rope(x, cos, sin_signed):
    n = x.shape[1]
    half = RET_DK // 2
    first = (_iota(x.shape, 1) % RET_DK) < half
    swapped = jnp.where(first, pltpu.roll(x, n - half, 1), pltpu.roll(x, half, 1))
    return x * cos + swapped * sin_signed


def _ret_norm_gate(o, gate, avg_bf16):
    mu = _mm_split(o, avg_bf16)
    d = o - mu
    var = _mm_split(d * d, avg_bf16)
    return d * lax.rsqrt(var + EPS) * _silu(gate)


def _head_avg_matrix():
    r = _iota((RET_W, RET_W), 0) // RET_DK
    c = _iota((RET_W, RET_W), 1) // RET_DK
    return jnp.where(r == c, 1.0 / RET_DK, 0.0).astype(BF16)


def _lru_coeffs(xl, wgate, bgate, lam):
    gates = _mm(xl, wgate) + bgate
    r = _sigmoid(gates[:, :LRU_W])
    i = _sigmoid(gates[:, LRU_W:])
    log_a = -LRU_C * r * _softplus(-lam)
    a = jnp.exp(log_a)
    b = jnp.sqrt(jnp.maximum(-jnp.tanh(log_a) * (a * a + 1.0), 0.0)) * (i * xl)
    return a, b


def _ffn_body(*refs, final, nf):
    if final:
        x_ref, nw_ref, wg_ref, wu_ref, wo_ref, fw_ref, o_ref = refs
    else:
        x_ref, nw_ref, wg_ref, wu_ref, wo_ref, o_ref = refs
    tf = D_FF // nf
    x = x_ref[...]
    u = _rmsnorm(x, nw_ref[...]).astype(BF16)
    acc = None
    for j in range(nf):
        gate = jnp.dot(u, wg_ref[:, j * tf:(j + 1) * tf], preferred_element_type=F32)
        up = jnp.dot(u, wu_ref[:, j * tf:(j + 1) * tf], preferred_element_type=F32)
        h = (_silu(gate) * up).astype(BF16)
        part = jnp.dot(h, wo_ref[j * tf:(j + 1) * tf, :], preferred_element_type=F32)
        acc = part if acc is None else acc + part
    y = x + 0.5 * acc
    if final:
        y = _rmsnorm(y, fw_ref[...])
    o_ref[...] = y


def _ffn(x, norm_w, wg, wu, wo, final_w=None, *, tm, nf):
    n = x.shape[0]
    final = final_w is not None
    resident = pl.Buffered(1)
    in_specs = [
        pl.BlockSpec((tm, D_MODEL), lambda i: (i, 0)),
        pl.BlockSpec((1, D_MODEL), lambda i: (0, 0)),
        pl.BlockSpec((D_MODEL, D_FF), lambda i: (0, 0), pipeline_mode=resident),
        pl.BlockSpec((D_MODEL, D_FF), lambda i: (0, 0), pipeline_mode=resident),
        pl.BlockSpec((D_FF, D_MODEL), lambda i: (0, 0), pipeline_mode=resident),
    ]
    args = [x, norm_w.reshape(1, D_MODEL), wg, wu, wo]
    if final:
        in_specs.append(pl.BlockSpec((1, D_MODEL), lambda i: (0, 0)))
        args.append(final_w.reshape(1, D_MODEL))
    return pl.pallas_call(
        functools.partial(_ffn_body, final=final, nf=nf),
        grid=(n // tm,),
        in_specs=in_specs,
        out_specs=pl.BlockSpec((tm, D_MODEL), lambda i: (i, 0)),
        out_shape=jax.ShapeDtypeStruct((n, D_MODEL), F32),
        compiler_params=pltpu.CompilerParams(
            dimension_semantics=("arbitrary",), vmem_limit_bytes=VMEM_LIMIT_BYTES),
        name="ffn_final" if final else "ffn",
    )(*args)


def _ffn_cast_body(*refs, final):
    if final:
        x_ref, nw_ref, wg_ref, wu_ref, wo_ref, fw_ref, o_ref, wg_o, wu_o, wo_o, u_s, acc_s = refs
    else:
        x_ref, nw_ref, wg_ref, wu_ref, wo_ref, o_ref, wg_o, wu_o, wo_o, u_s, acc_s = refs
    j = pl.program_id(0)

    @pl.when(j == 0)
    def _():
        u_s[...] = _rmsnorm(x_ref[...], nw_ref[...]).astype(BF16)
        acc_s[...] = jnp.zeros_like(acc_s)

    wg = wg_ref[...].astype(BF16)
    wu = wu_ref[...].astype(BF16)
    wo = wo_ref[...].astype(BF16)
    wg_o[...] = wg
    wu_o[...] = wu
    wo_o[...] = wo
    u = u_s[...]
    gate = jnp.dot(u, wg, preferred_element_type=F32)
    up = jnp.dot(u, wu, preferred_element_type=F32)
    acc_s[...] += jnp.dot((_silu(gate) * up).astype(BF16), wo, preferred_element_type=F32)

    @pl.when(j == pl.num_programs(0) - 1)
    def _():
        y = x_ref[...] + 0.5 * acc_s[...]
        if final:
            y = _rmsnorm(y, fw_ref[...])
        o_ref[...] = y


def _ffn_cast(x, norm_w, w_in, w_out, layer, final_w=None, *, tf):
    n = x.shape[0]
    nf = D_FF // tf
    final = final_w is not None
    in_specs = [
        pl.BlockSpec((n, D_MODEL), lambda j: (0, 0)),
        pl.BlockSpec((1, D_MODEL), lambda j: (0, 0)),
        pl.BlockSpec((None, D_MODEL, tf), lambda j: (layer, 0, j)),
        pl.BlockSpec((None, D_MODEL, tf), lambda j: (layer, 0, j + nf)),
        pl.BlockSpec((None, tf, D_MODEL), lambda j: (layer, j, 0)),
    ]
    args = [x, norm_w.reshape(1, D_MODEL), w_in, w_in, w_out]
    if final:
        in_specs.append(pl.BlockSpec((1, D_MODEL), lambda j: (0, 0)))
        args.append(final_w.reshape(1, D_MODEL))
    return pl.pallas_call(
        functools.partial(_ffn_cast_body, final=final),
        grid=(nf,),
        in_specs=in_specs,
        out_specs=[pl.BlockSpec((n, D_MODEL), lambda j: (0, 0)),
                   pl.BlockSpec((D_MODEL, tf), lambda j: (0, j)),
                   pl.BlockSpec((D_MODEL, tf), lambda j: (0, j)),
                   pl.BlockSpec((tf, D_MODEL), lambda j: (j, 0))],
        out_shape=[jax.ShapeDtypeStruct((n, D_MODEL), F32),
                   jax.ShapeDtypeStruct((D_MODEL, D_FF), BF16),
                   jax.ShapeDtypeStruct((D_MODEL, D_FF), BF16),
                   jax.ShapeDtypeStruct((D_FF, D_MODEL), BF16)],
        scratch_shapes=[pltpu.VMEM((n, D_MODEL), BF16), pltpu.VMEM((n, D_MODEL), F32)],
        compiler_params=pltpu.CompilerParams(
            dimension_semantics=("arbitrary",), vmem_limit_bytes=VMEM_LIMIT_BYTES),
        name="ffn_cast_final" if final else "ffn_cast",
    )(*args)


def _heads_per_group(ck):
    return max(1, min(DN_HEADS, 128 // ck))


def _tri_inv_steps(a, n_sub, out):
    m = a[0].shape[0]
    r = _iota((m, m), 0)
    c = _iota((m, m), 1)
    eye = jnp.where(r == c, 1.0, 0.0).astype(F32)
    diag = (r // SUB) == (c // SUB)
    x = [jnp.where(diag, ai, 0.0) for ai in a]
    off = [ai - xi for ai, xi in zip(a, x)]
    p = [eye - xi for xi in x]
    for _ in range(3):
        x = [_mm(xi, xi) for xi in x]
        yield
        p = [pi + _mm(pi, xi) for pi, xi in zip(p, x)]
        yield
    if n_sub == 1:
        out.extend(p)
        return
    n = [_mm(pi, oi) for pi, oi in zip(p, off)]
    yield
    n2 = [_mm(ni, ni) for ni in n]
    yield
    rr = [eye - ni + n2i - _mm(ni, n2i) for ni, n2i in zip(n, n2)]
    yield
    out.extend(_mm(ri, pi) for ri, pi in zip(rr, p))
    yield


def _mix_pipe_body(x_ref, nw_ref, wqkv_ref, wabt_ref, wrest_ref, dncw_ref,
                   alog_c_ref, dtb_c_ref, dnnw_ref,
                   lcw_ref, lcb_ref, wgate_ref, bgate_ref, lam_ref, cos_ref, sin_ref, wout_ref,
                   sdn0_ref, dnt0_ref, h0_ref, lrt0_ref, sret0_ref,
                   y_ref, sdn_o_ref, dnt_o_ref, h_o_ref, lrt_o_ref, sretbd_o_ref, sret_o_ref,
                   sdn, sret, hst, dn_tail, lru_tail, q_s, k_s, v_s, gb_s, grow_s, odn_s, oret_s, omix_s,
                   h_uv, h_wq, h_pm, h_kd, h_egl, h_rq, h_oi, h_su, h_z, h_rg, h_x, h_lru,
                   *, tc, ck, nt, n_blocks):
    i = pl.program_id(0)
    t_a = i % nt
    t_b = (i + nt - 1) % nt
    nc = tc // ck
    hc = DN_HEADS * ck
    n_sub = ck // SUB
    chunks = range(nc)
    rows = [slice(c * ck, (c + 1) * ck) for c in chunks]
    heads = [(h * ck, (h + 1) * ck) for h in range(DN_HEADS)]
    hg = _heads_per_group(ck)
    ng = DN_HEADS // hg
    gc = hg * ck
    probs = [(c, g) for c in chunks for g in range(ng)]

    @pl.when(i == 0)
    def _():
        for ref in (h_uv, h_wq, h_pm, h_kd, h_egl, h_rq, h_oi, h_su, h_z, h_rg, h_x, h_lru):
            ref[...] = jnp.zeros_like(ref)

    @pl.when(t_a == 0)
    def _():
        hst[...] = h0_ref[...]
        dn_tail[...] = dnt0_ref[...]
        lru_tail[...] = lrt0_ref[...]

    @pl.when((t_b == 0) | (i == 0))
    def _():
        sdn[...] = sdn0_ref[...]
        sret[...] = sret0_ref[...]

    sr = _iota((gc, gc), 0)
    sc = _iota((gc, gc), 1)
    strict = ((sr // ck) == (sc // ck)) & (sr > sc)
    eye_g = jnp.where(sr == sc, 1.0, 0.0).astype(F32)
    lg_rows = _lane_log_gamma((hc, ck), 0, ck)
    tpos = _iota((hc, ck), 0) % ck
    ipos = _iota((hc, ck), 1)
    causal = tpos >= ipos
    ret_dec = jnp.where(causal, jnp.exp(jnp.where(causal, (tpos - ipos).astype(F32) * lg_rows, 0.0)), 0.0)
    head_sel = (_iota((hc, RET_W), 0) // ck) == (_iota((hc, RET_W), 1) // RET_DK)
    lg_lane = _lane_log_gamma((ck, RET_W), 1, RET_DK)
    tl = _iota((ck, RET_W), 0).astype(F32)
    ret_eg = jnp.exp((tl + 1.0) * lg_lane)
    ret_kdec = jnp.exp((ck - 1.0 - tl) * lg_lane)
    ret_gc = jnp.exp(ck * _lane_log_gamma((1, RET_W), 1, RET_DK))
    bd = (_iota((RET_W, RET_W), 0) // RET_DK) == (_iota((RET_W, RET_W), 1) // RET_DK)

    a = {}
    b = {}

    def a_norm():
        a["x"] = x_ref[...]
        a["u"] = _rmsnorm(a["x"], nw_ref[...]).astype(BF16)

    def a_qkv_piece(p):
        cols = slice(2 * DN_DK * p, 2 * DN_DK * (p + 1))
        a["pre", p] = jnp.dot(a["u"], wqkv_ref[:, cols], preferred_element_type=F32)

    def a_conv_piece(p):
        cols = slice(2 * DN_DK * p, 2 * DN_DK * (p + 1))
        pre = a.pop(("pre", p))
        conv = _causal_conv(pre, dn_tail[:, cols], dncw_ref[:, cols])
        dn_tail[:, cols] = pre[tc - 8:tc]
        act = _silu(conv)
        kind, first_head = divmod(2 * p, DN_HEADS)
        for j in range(2):
            v = act[:, j * DN_DK:(j + 1) * DN_DK]
            if kind == 0:
                q_s[first_head + j] = (v * lax.rsqrt(jnp.sum(v * v, axis=-1, keepdims=True) + EPS)
                                       * (DN_DK ** -0.5))
            elif kind == 1:
                k_s[first_head + j] = v * lax.rsqrt(jnp.sum(v * v, axis=-1, keepdims=True) + EPS)
            else:
                v_s[first_head + j] = v

    def proj(lo, hi):
        return jnp.dot(a["u"], wrest_ref[:, lo:hi], preferred_element_type=F32)

    def a_lru_in():
        lru = proj(512, 1024)
        xl = _causal_conv(lru[:, :LRU_W], lru_tail[...], lcw_ref[...])
        lru_tail[...] = lru[tc - 8:tc, :LRU_W]
        a["xl"] = xl + lcb_ref[...]
        a["lru_y"] = lru[:, LRU_W:]

    def a_ret_q():
        a["ret_q"] = proj(1024, 1280)

    def a_ret_kv():
        kv = proj(1280, 1792)
        cos = cos_ref[...]
        sin = sin_ref[...]
        a["rq"] = _rope(a["ret_q"], cos, sin)
        a["rk"] = _rope(kv[:, :RET_W], cos, sin) * (RET_DK ** -0.5)
        a["rv"] = kv[:, RET_W:]

    def a_z():
        a["dn_z"] = proj(0, 512)

    def a_g():
        a["ret_g"] = proj(1792, 2048)

    def a_gates():
        abt = lax.dot_general(wabt_ref[...], a["u"], (((1,), (1,)), ((), ())), preferred_element_type=F32)
        g_row = -jnp.exp(alog_c_ref[...]) * _softplus(abt + dtb_c_ref[...])
        rr = _iota((tc, tc), 0)
        cc = _iota((tc, tc), 1)
        upper = jnp.where(((rr // ck) == (cc // ck)) & (rr <= cc), 1.0, 0.0)
        big_g_row = _mm_01_right(g_row, upper)
        is_g = _iota((8, tc), 0) < DN_HEADS
        gb_s[...] = jnp.where(is_g, big_g_row, _sigmoid(abt)).T
        for c in chunks:
            grow_s[c] = jnp.concatenate(
                [big_g_row[h:h + 1, c * ck:(c + 1) * ck] for h in range(DN_HEADS)], axis=1)

    def a_lru_scan_steps():
        ga, gb = _lru_coeffs(a["xl"], wgate_ref[...], bgate_ref[...], lam_ref[...])
        row = _iota((tc, LRU_W), 0)
        gb = gb + jnp.where(row == 0, ga * hst[...], 0.0)
        yield
        s = 1
        while s < tc:
            keep = row >= s
            gb = ga * jnp.where(keep, pltpu.roll(gb, s, 0), 0.0) + gb
            ga = ga * jnp.where(keep, pltpu.roll(ga, s, 0), 1.0)
            s *= 2
            yield
        hst[...] = gb[tc - 1:tc, :]
        a["o_lru"] = (gb * jax.nn.gelu(a["lru_y"])).astype(BF16)
        yield

    def a_kkqk():
        def stack(ref, c, g):
            return jnp.concatenate([ref[h, rows[c], :] for h in range(g * hg, (g + 1) * hg)], axis=0)

        def stack_col(c, g, first):
            return jnp.concatenate([gb_s[rows[c], first + h:first + h + 1]
                                    for h in range(g * hg, (g + 1) * hg)], axis=0)

        a["kst"] = [stack(k_s, c, g) for c, g in probs]
        a["qst"] = [stack(q_s, c, g) for c, g in probs]
        a["vst"] = [stack(v_s, c, g) for c, g in probs]
        gst = [stack_col(c, g, 0) for c, g in probs]
        a["gst"] = gst
        a["bst"] = [stack_col(c, g, DN_HEADS) for c, g in probs]
        dstrict = []
        for k, (c, g) in enumerate(probs):
            diff = gst[k] - grow_s[c][:, g * gc:(g + 1) * gc]
            dstrict.append(jnp.where(strict, jnp.exp(jnp.where(strict, diff, 0.0)), 0.0))
        a["eg"] = [jnp.exp(g) for g in gst]
        a["amat"] = [a["bst"][k] * _mm_nt(a["kst"][k], a["kst"][k]) * dstrict[k] for k in range(len(probs))]
        a["pmat"] = [(_mm_nt(a["qst"][k], a["kst"][k]) * (dstrict[k] + eye_g)).astype(BF16)
                     for k in range(len(probs))]

    tinv = []

    def a_sol():
        kst, qst, vst, gst, bst, eg = (a[k] for k in ("kst", "qst", "vst", "gst", "bst", "eg"))
        sol = [_mm(tinv[k], jnp.concatenate([bst[k] * vst[k], (bst[k] * eg[k]) * kst[k]], axis=1))
               for k in range(len(probs))]
        a["uv"] = [s[:, :DN_DK] for s in sol]
        wq, egl, kd = {}, {}, {}
        for k, (c, g) in enumerate(probs):
            for j in range(hg):
                lo, hi = j * ck, (j + 1) * ck
                key = (c, g * hg + j)
                wq[key] = jnp.concatenate([sol[k][lo:hi, DN_DK:], eg[k][lo:hi] * qst[k][lo:hi]],
                                          axis=0).astype(BF16)
                g_last = gst[k][hi - 1:hi]
                egl[key] = jnp.exp(g_last)
                kd[key] = (kst[k][lo:hi] * jnp.exp(g_last - gst[k][lo:hi])).astype(BF16)
        a["wq"], a["egl"], a["kd"] = wq, egl, kd

    def a_ret_scores():
        qc = [a["rq"][rs] for rs in rows]
        kc = [a["rk"][rs] for rs in rows]
        qp = [jnp.where(head_sel, jnp.concatenate([q] * RET_HEADS, axis=0), 0.0) for q in qc]
        a["qc"] = qc
        a["sc"] = [(_mm_nt(qp[c], kc[c]) * ret_dec).astype(BF16) for c in chunks]

    def a_ret_intra():
        vc = [a["rv"][rs].astype(BF16) for rs in rows]
        o_intra = []
        for c in chunks:
            op = jnp.where(head_sel, jnp.dot(a["sc"][c], vc[c], preferred_element_type=F32), 0.0)
            acc = op[0:ck]
            for h in range(1, RET_HEADS):
                acc = acc + op[h * ck:(h + 1) * ck]
            o_intra.append(acc)
        a["o_intra"] = o_intra
        a["s_upd"] = [jnp.where(bd, _mm_tn(a["rk"][rows[c]] * ret_kdec, vc[c]), 0.0) for c in chunks]

    def a_handoff():
        for k, (c, g) in enumerate(probs):
            h_uv[c, g * gc:(g + 1) * gc, :] = a["uv"][k]
            h_pm[c, g] = a["pmat"][k]
        for c in chunks:
            h_rq[c] = a["qc"][c].astype(BF16)
            h_oi[c] = a["o_intra"][c]
            h_su[c] = a["s_upd"][c]
            for h in range(DN_HEADS):
                h_wq[c, h] = a["wq"][c, h]
                h_kd[c, h] = a["kd"][c, h]
                h_egl[c, h:h + 1, :] = jnp.broadcast_to(a["egl"][c, h], (1, DN_DK))
        h_z[...] = a["dn_z"]
        h_rg[...] = a["ret_g"]
        h_x[...] = a["x"]
        h_lru[...] = a["o_lru"]

    def b_post_a(c):
        ws_qs = [_mm(h_wq[c, h], sdn[h]) for h in range(DN_HEADS)]
        b["u"] = [h_uv[c, lo:hi, :] - ws_qs[h][:ck] for h, (lo, hi) in enumerate(heads)]
        b["qs"] = [w[ck:] for w in ws_qs]

    def b_post_b(c):
        for g in range(ng):
            group = range(g * hg, (g + 1) * hg)
            ost = (jnp.concatenate([b["qs"][h] for h in group], axis=0)
                   + _mm(h_pm[c, g], jnp.concatenate([b["u"][h] for h in group], axis=0)))
            for j, h in enumerate(group):
                odn_s[rows[c], h * DN_DK:(h + 1) * DN_DK] = ost[j * ck:(j + 1) * ck]
        for h in range(DN_HEADS):
            sdn[h] = h_egl[c, h:h + 1, :] * sdn[h] + _mm_tn(h_kd[c, h], b["u"][h])

    def b_ret():
        for c in chunks:
            s0 = sret[...]
            oret_s[rows[c], :] = ret_eg * _mm(h_rq[c], s0) + h_oi[c]
            sret[...] = ret_gc * s0 + h_su[c]

    def b_out():
        nw = dnnw_ref[...]
        for h in range(DN_HEADS):
            sl = slice(h * DN_DK, (h + 1) * DN_DK)
            oh = odn_s[:, sl]
            oh = oh * lax.rsqrt(jnp.mean(oh * oh, axis=-1, keepdims=True) + EPS) * nw
            omix_s[:, sl] = (oh * _silu(h_z[:, sl])).astype(BF16)
        omix_s[:, 512:768] = h_lru[...]
        omix_s[:, 768:1024] = _ret_norm_gate(oret_s[...], h_rg[...], _head_avg_matrix()).astype(BF16)
        y_ref[...] = h_x[...] + jnp.dot(omix_s[...], wout_ref[...], preferred_element_type=F32)

    n_piece = 3 * DN_W // (2 * DN_DK)
    a_seq = [a_norm]
    for p in range(n_piece):
        a_seq.append(lambda p=p: (a_qkv_piece(p), a_conv_piece(p)))
    a_seq += [a_lru_in, lambda: (a_gates(), a_ret_q()), a_ret_kv, a_z, a_g, a_kkqk]
    b_seq = []
    for c in chunks:
        b_seq += [functools.partial(b_post_a, c), functools.partial(b_post_b, c)]
    b_seq += [b_ret, b_out]
    b_seq.pop(0)()
    for fa in a_seq:
        fa()
        if b_seq:
            b_seq.pop(0)()
    for fb in b_seq:
        fb()
    scan = a_lru_scan_steps()
    for _ in _tri_inv_steps(a["amat"], n_sub, tinv):
        next(scan, None)
    for _ in scan:
        pass
    a_sol()
    a_ret_scores()
    a_ret_intra()
    a_handoff()

    @pl.when((t_a == nt - 1) & (i < n_blocks))
    def _():
        dnt_o_ref[0] = dn_tail[...]
        h_o_ref[0] = hst[...]
        lrt_o_ref[0] = lru_tail[...]

    @pl.when((t_b == nt - 1) & (i > 0))
    def _():
        sdn_o_ref[0] = sdn[...]
        s_bd = sret[...]
        sretbd_o_ref[0] = s_bd
        for h in range(RET_HEADS):
            sret_o_ref[0, h] = s_bd[h * RET_DK:(h + 1) * RET_DK, h * RET_DK:(h + 1) * RET_DK]


def _mix_pipe(x, lw, cos, sin, init, *, batch, seq, tc, ck):
    nt = seq // tc
    nc = tc // ck
    hc = DN_HEADS * ck
    hg = _heads_per_group(ck)
    ng, gc = DN_HEADS // hg, hg * ck
    n_blocks = batch * nt
    blk_a = lambda i: jnp.minimum(i, n_blocks - 1)
    blk_b = lambda i: jnp.maximum(i - 1, 0)
    const = lambda shape: pl.BlockSpec(shape, lambda i: (0,) * len(shape))
    in_specs = [
        pl.BlockSpec((tc, D_MODEL), lambda i: (blk_a(i), 0)),
        const((1, D_MODEL)), const((D_MODEL, 3 * DN_W)), const((8, D_MODEL)),
        const((D_MODEL, 2048)), const((CONV_W, 3 * DN_W)),
        const((8, 1)), const((8, 1)), const((1, DN_DK)),
        const((CONV_W, LRU_W)), const((1, LRU_W)), const((LRU_W, 2 * LRU_W)), const((1, 2 * LRU_W)),
        const((1, LRU_W)),
        pl.BlockSpec((tc, RET_W), lambda i: (blk_a(i) % nt, 0)),
        pl.BlockSpec((tc, RET_W), lambda i: (blk_a(i) % nt, 0)),
        const((D_MODEL, D_MODEL)),
        const((DN_HEADS, DN_DK, DN_DK)), const((8, 3 * DN_W)), const((1, LRU_W)), const((8, LRU_W)),
        const((RET_W, RET_W)),
    ]
    out_shape = [
        jax.ShapeDtypeStruct((batch * seq, D_MODEL), F32),
        jax.ShapeDtypeStruct((batch, DN_HEADS, DN_DK, DN_DK), F32),
        jax.ShapeDtypeStruct((batch, 8, 3 * DN_W), F32),
        jax.ShapeDtypeStruct((batch, 1, LRU_W), F32),
        jax.ShapeDtypeStruct((batch, 8, LRU_W), F32),
        jax.ShapeDtypeStruct((batch, RET_W, RET_W), F32),
        jax.ShapeDtypeStruct((batch, RET_HEADS, RET_DK, RET_DK), F32),
    ]
    seq_a = lambda i: blk_a(i) // nt
    seq_b = lambda i: blk_b(i) // nt
    out_specs = [
        pl.BlockSpec((tc, D_MODEL), lambda i: (blk_b(i), 0)),
        pl.BlockSpec((1, DN_HEADS, DN_DK, DN_DK), lambda i: (seq_b(i), 0, 0, 0)),
        pl.BlockSpec((1, 8, 3 * DN_W), lambda i: (seq_a(i), 0, 0)),
        pl.BlockSpec((1, 1, LRU_W), lambda i: (seq_a(i), 0, 0)),
        pl.BlockSpec((1, 8, LRU_W), lambda i: (seq_a(i), 0, 0)),
        pl.BlockSpec((1, RET_W, RET_W), lambda i: (seq_b(i), 0, 0)),
        pl.BlockSpec((1, RET_HEADS, RET_DK, RET_DK), lambda i: (seq_b(i), 0, 0, 0)),
    ]
    scratch = [
        pltpu.VMEM((DN_HEADS, DN_DK, DN_DK), F32),
        pltpu.VMEM((RET_W, RET_W), F32),
        pltpu.VMEM((1, LRU_W), F32),
        pltpu.VMEM((8, 3 * DN_W), F32),
        pltpu.VMEM((8, LRU_W), F32),
        pltpu.VMEM((DN_HEADS, tc, DN_DK), F32),
        pltpu.VMEM((DN_HEADS, tc, DN_DK), F32),
        pltpu.VMEM((DN_HEADS, tc, DN_DK), F32),
        pltpu.VMEM((tc, 8), F32),
        pltpu.VMEM((nc, 1, hc), F32),
        pltpu.VMEM((tc, DN_W), F32),
        pltpu.VMEM((tc, RET_W), F32),
        pltpu.VMEM((tc, D_MODEL), BF16),
        pltpu.VMEM((nc, hc, DN_DK), F32),
        pltpu.VMEM((nc, DN_HEADS, 2 * ck, DN_DK), BF16),
        pltpu.VMEM((nc, ng, gc, gc), BF16),
        pltpu.VMEM((nc, DN_HEADS, ck, DN_DK), BF16),
        pltpu.VMEM((nc, 8, DN_DK), F32),
        pltpu.VMEM((nc, ck, RET_W), BF16),
        pltpu.VMEM((nc, ck, RET_W), F32),
        pltpu.VMEM((nc, RET_W, RET_W), F32),
        pltpu.VMEM((tc, DN_W), F32),
        pltpu.VMEM((tc, RET_W), F32),
        pltpu.VMEM((tc, D_MODEL), F32),
        pltpu.VMEM((tc, LRU_W), BF16),
    ]
    args = [x, lw["norm_mix"], lw["wqkv"], lw["wabt"], lw["wrest"], lw["dn_conv_w"],
            lw["alog_c"], lw["dtb_c"], lw["dn_norm_w"],
            lw["lru_conv_w"], lw["lru_conv_b"], lw["wgate"], lw["bgate"], lw["lam"], cos, sin,
            lw["w_out"], *init]
    return pl.pallas_call(
        functools.partial(_mix_pipe_body, tc=tc, ck=ck, nt=nt, n_blocks=n_blocks),
        grid=(n_blocks + 1,),
        in_specs=in_specs,
        out_specs=out_specs,
        out_shape=out_shape,
        scratch_shapes=scratch,
        compiler_params=pltpu.CompilerParams(
            dimension_semantics=("arbitrary",), vmem_limit_bytes=VMEM_LIMIT_BYTES),
        name=f"mix_pipe_c{ck}",
    )(*args)


def _mix_step_body(*refs, bb, layer):
    (x_ref, nw_ref, wqkv_ref, wab_ref, wrest_ref, dncw_ref, alog_r_ref, dtb_r_ref,
     dnnw_ref, lcw_ref, lcb_ref, wgate_ref, bgate_ref, lam_ref, cos_ref, sin_ref,
     sdn_ref, dnc_ref, h_ref, lrc_ref) = refs[:20]
    refs = refs[20:]
    if layer > 0:
        prev_sdn_ref = refs[0]
        refs = refs[1:]
    (omix_o_ref, retin_o_ref, sdn_o_ref, dnc_o_ref, h_o_ref, lrc_o_ref,
     q_s, k_s, v_s, eg_s, beta_s, odn_s, dnz_s) = refs
    i = pl.program_id(0)
    if layer > 0:
        sdn_o_ref[0:layer] = prev_sdn_ref[...]

    @pl.when(i == 0)
    def _():
        x = x_ref[...]
        u = _rmsnorm(x, nw_ref[...]).astype(BF16)

        qkv_pre = jnp.dot(u, wqkv_ref[...], preferred_element_type=F32)
        cw = dncw_ref[...]
        conv = (dnc_ref[0] * cw[0:1, :] + dnc_ref[1] * cw[1:2, :] + dnc_ref[2] * cw[2:3, :]
                + qkv_pre * cw[3:4, :])
        dnc_o_ref[0] = dnc_ref[1]
        dnc_o_ref[1] = dnc_ref[2]
        dnc_o_ref[2] = qkv_pre
        qkv = _silu(conv)
        for h in range(DN_HEADS):
            sl = slice(h * DN_DK, (h + 1) * DN_DK)
            qh = qkv[:, sl]
            kh = qkv[:, DN_W + h * DN_DK:DN_W + (h + 1) * DN_DK]
            q_s[:, sl] = qh * lax.rsqrt(jnp.sum(qh * qh, axis=-1, keepdims=True) + EPS) * (DN_DK ** -0.5)
            k_s[:, sl] = kh * lax.rsqrt(jnp.sum(kh * kh, axis=-1, keepdims=True) + EPS)
        v_s[...] = qkv[:, 2 * DN_W:]
        ab = jnp.dot(u, wab_ref[...], preferred_element_type=F32)
        eg_s[...] = jnp.exp(-jnp.exp(alog_r_ref[...]) * _softplus(ab[:, :128] + dtb_r_ref[...]))
        beta_s[...] = _sigmoid(ab[:, 128:])

        rest = jnp.dot(u, wrest_ref[...], preferred_element_type=F32)
        dnz_s[...] = rest[:, 0:512]
        lru_x = rest[:, 512:768]
        lru_y = rest[:, 768:1024]

        lw = lcw_ref[...]
        xl = (lrc_ref[0] * lw[0:1, :] + lrc_ref[1] * lw[1:2, :] + lrc_ref[2] * lw[2:3, :]
              + lru_x * lw[3:4, :] + lcb_ref[...])
        lrc_o_ref[0] = lrc_ref[1]
        lrc_o_ref[1] = lrc_ref[2]
        lrc_o_ref[2] = lru_x
        a, b = _lru_coeffs(xl, wgate_ref[...], bgate_ref[...], lam_ref[...])
        h_new = a * h_ref[...] + b
        h_o_ref[...] = h_new
        omix_o_ref[:, DN_W:DN_W + LRU_W] = (h_new * jax.nn.gelu(lru_y)).astype(BF16)

        cos = cos_ref[...]
        sin = sin_ref[...]
        retin_o_ref[:, 0:256] = _rope(rest[:, 1024:1280], cos, sin)
        retin_o_ref[:, 256:512] = _rope(rest[:, 1280:1536], cos, sin) * (RET_DK ** -0.5)
        retin_o_ref[:, 512:768] = rest[:, 1536:1792]
        retin_o_ref[:, 768:1024] = rest[:, 1792:2048]

    for t8 in range(bb // 8):
        r0 = pl.multiple_of(i * bb + t8 * 8, 8)
        eg_t = eg_s[pl.ds(r0, 8), :]
        beta_t = beta_s[pl.ds(r0, 8), :]
        for h in range(DN_HEADS):
            sl = slice(h * DN_DK, (h + 1) * DN_DK)
            kt = k_s[pl.ds(r0, 8), sl].T
            qt = q_s[pl.ds(r0, 8), sl].T
            v_t = v_s[pl.ds(r0, 8), sl]
            rows = []
            for j in range(8):
                kcol = kt[:, j:j + 1]
                qcol = qt[:, j:j + 1]
                s0 = sdn_ref[t8 * 8 + j, h]
                eg = eg_t[j:j + 1, h:h + 1]
                ks = jnp.sum(s0 * kcol, axis=0, keepdims=True)
                qs = jnp.sum(s0 * qcol, axis=0, keepdims=True)
                qk = jnp.sum(qcol * kcol, axis=0, keepdims=True)
                uu = beta_t[j:j + 1, h:h + 1] * (v_t[j:j + 1, :] - eg * ks)
                rows.append(eg * qs + qk * uu)
                sdn_o_ref[layer, t8 * 8 + j, h] = eg * s0 + kcol * uu
            odn_s[pl.ds(r0, 8), sl] = jnp.concatenate(rows, axis=0)

    @pl.when(i == pl.num_programs(0) - 1)
    def _():
        nw = dnnw_ref[...]
        for h in range(DN_HEADS):
            sl = slice(h * DN_DK, (h + 1) * DN_DK)
            oh = odn_s[:, sl]
            oh = oh * lax.rsqrt(jnp.mean(oh * oh, axis=-1, keepdims=True) + EPS) * nw
            omix_o_ref[:, sl] = (oh * _silu(dnz_s[:, sl])).astype(BF16)


def _mix_step(x, lw, cos, sin, layer, sdn, dnc, hl, lrc, prev_sdn, *, bb):
    n = x.shape[0]
    const = lambda shape: pl.BlockSpec(shape, lambda i: (0,) * len(shape))
    in_specs = [
        const((n, D_MODEL)),
        const((1, D_MODEL)), const((D_MODEL, 3 * DN_W)), const((D_MODEL, 256)), const((D_MODEL, 2048)),
        const((CONV_W, 3 * DN_W)), const((1, 128)), const((1, 128)), const((1, DN_DK)),
        const((CONV_W, LRU_W)), const((1, LRU_W)), const((LRU_W, 2 * LRU_W)), const((1, 2 * LRU_W)),
        const((1, LRU_W)), const((1, RET_W)), const((1, RET_W)),
        pl.BlockSpec((None, bb, DN_HEADS, DN_DK, DN_DK), lambda i: (layer, i, 0, 0, 0)),
        const((CONV_W - 1, n, 3 * DN_W)),
        pl.BlockSpec((None, n, LRU_W), lambda i: (layer, 0, 0)),
        const((CONV_W - 1, n, LRU_W)),
    ]
    if layer > 0:
        in_specs.append(pl.BlockSpec((layer, bb, DN_HEADS, DN_DK, DN_DK), lambda i: (0, i, 0, 0, 0)))
    out_shape = [
        jax.ShapeDtypeStruct((n, DN_W + LRU_W), BF16),
        jax.ShapeDtypeStruct((n, 4 * RET_W), F32),
        jax.ShapeDtypeStruct((layer + 1, n, DN_HEADS, DN_DK, DN_DK), F32),
        jax.ShapeDtypeStruct((CONV_W - 1, n, 3 * DN_W), F32),
        jax.ShapeDtypeStruct((n, LRU_W), F32),
        jax.ShapeDtypeStruct((CONV_W - 1, n, LRU_W), F32),
    ]
    out_specs = [
        const((n, DN_W + LRU_W)),
        const((n, 4 * RET_W)),
        pl.BlockSpec((layer + 1, bb, DN_HEADS, DN_DK, DN_DK), lambda i: (0, i, 0, 0, 0)),
        const((CONV_W - 1, n, 3 * DN_W)),
        const((n, LRU_W)),
        const((CONV_W - 1, n, LRU_W)),
    ]
    scratch = [
        pltpu.VMEM((n, DN_W), F32), pltpu.VMEM((n, DN_W), F32), pltpu.VMEM((n, DN_W), F32),
        pltpu.VMEM((n, 128), F32), pltpu.VMEM((n, 128), F32),
        pltpu.VMEM((n, DN_W), F32), pltpu.VMEM((n, DN_W), F32),
    ]
    args = [x, lw["norm_mix"], lw["wqkv"], lw["wab"], lw["wrest"], lw["dn_conv_w"], lw["alog_r"],
            lw["dtb_r"], lw["dn_norm_w"], lw["lru_conv_w"], lw["lru_conv_b"], lw["wgate"], lw["bgate"],
            lw["lam"], cos, sin, sdn, dnc, hl, lrc]
    if layer > 0:
        args.append(prev_sdn)
    return pl.pallas_call(
        functools.partial(_mix_step_body, bb=bb, layer=layer),
        grid=(n // bb,),
        in_specs=in_specs,
        out_specs=out_specs,
        out_shape=out_shape,
        scratch_shapes=scratch,
        compiler_params=pltpu.CompilerParams(
            dimension_semantics=("arbitrary",), vmem_limit_bytes=VMEM_LIMIT_BYTES),
        name="mix_step",
    )(*args)


def _ret_step_body(*refs, layer):
    x_ref, omix_ref, retin_ref, wout_ref, s_ref = refs[:5]
    refs = refs[5:]
    if layer > 0:
        prev_ref = refs[0]
        refs = refs[1:]
    y_ref, s_o_ref, qt_s, kt_s, vt_s, ot_s = refs
    h = pl.program_id(0)
    if layer > 0:
        s_o_ref[0:layer] = prev_ref[...]

    @pl.when(h == 0)
    def _():
        qt_s[...] = retin_ref[:, 0:RET_W].T
        kt_s[...] = retin_ref[:, RET_W:2 * RET_W].T
        vt_s[...] = retin_ref[:, 2 * RET_W:3 * RET_W].T

    rows = pl.ds(pl.multiple_of(h * RET_DK, RET_DK), RET_DK)
    q = qt_s[rows, :]
    k = kt_s[rows, :]
    v = vt_s[rows, :]
    gamma = jnp.float32(math.exp(LOG_GAMMA[0]))
    for i in range(1, RET_HEADS):
        gamma = jnp.where(h == i, jnp.float32(math.exp(LOG_GAMMA[i])), gamma)
    acc = jnp.zeros_like(v)
    for d in range(RET_DK):
        s_d = s_ref[d]
        acc = acc + s_d * q[d:d + 1, :]
        s_o_ref[layer, 0, d] = gamma * s_d + k[d:d + 1, :] * v
    ot_s[rows, :] = gamma * acc + jnp.sum(q * k, axis=0, keepdims=True) * v

    @pl.when(h == pl.num_programs(0) - 1)
    def _():
        o_ret = _ret_norm_gate(ot_s[...].T, retin_ref[:, 3 * RET_W:], _head_avg_matrix())
        omix = jnp.concatenate([omix_ref[...], o_ret.astype(BF16)], axis=1)
        y_ref[...] = x_ref[...] + jnp.dot(omix, wout_ref[...], preferred_element_type=F32)


def _ret_step(x, omix_part, retin, w_out, sret_t, layer, prev, *, n):
    const = lambda shape: pl.BlockSpec(shape, lambda h: (0,) * len(shape))
    in_specs = [
        const((n, D_MODEL)), const((n, DN_W + LRU_W)), const((n, 4 * RET_W)), const((D_MODEL, D_MODEL)),
        pl.BlockSpec((None, None, RET_DK, RET_DK, n), lambda h: (layer, h, 0, 0, 0)),
    ]
    args = [x, omix_part, retin, w_out, sret_t]
    if layer > 0:
        in_specs.append(pl.BlockSpec((layer, 1, RET_DK, RET_DK, n), lambda h: (0, h, 0, 0, 0)))
        args.append(prev)
    return pl.pallas_call(
        functools.partial(_ret_step_body, layer=layer),
        grid=(RET_HEADS,),
        in_specs=in_specs,
        out_specs=[const((n, D_MODEL)),
                   pl.BlockSpec((layer + 1, 1, RET_DK, RET_DK, n), lambda h: (0, h, 0, 0, 0))],
        out_shape=[jax.ShapeDtypeStruct((n, D_MODEL), F32),
                   jax.ShapeDtypeStruct((layer + 1, RET_HEADS, RET_DK, RET_DK, n), F32)],
        scratch_shapes=[pltpu.VMEM((RET_W, n), F32)] * 4,
        compiler_params=pltpu.CompilerParams(
            dimension_semantics=("arbitrary",), vmem_limit_bytes=VMEM_LIMIT_BYTES),
        name="ret_step",
    )(*args)


def _rope_tables(pos):
    half = RET_DK // 2
    inv = ROPE_BASE ** (-jnp.arange(half, dtype=F32) / half)
    ang = pos.astype(F32)[:, None] * inv[None, :]
    cos = jnp.cos(ang)
    sin = jnp.sin(ang)
    cos_full = jnp.tile(jnp.concatenate([cos, cos], axis=-1), (1, RET_HEADS))
    sin_signed = jnp.tile(jnp.concatenate([-sin, sin], axis=-1), (1, RET_HEADS))
    return cos_full, sin_signed


def _block_diag(w):
    n, d, e = w.shape
    eye = jnp.eye(n, dtype=w.dtype)
    return (eye[:, None, :, None] * w[:, :, None, :]).reshape(n * d, n * e)


def _layer_weights(l, norm_mix, w_in, dn_conv_w, dn_a_log, dn_dt_bias, dn_norm_w, lru_conv_w, lru_conv_b,
                   lru_wa, lru_ba, lru_wx, lru_bx, lru_lambda, w_out):
    wl = w_in[l]
    o_a = 3 * DN_W
    wa = wl[:, o_a:o_a + DN_HEADS]
    wb = wl[:, o_a + DN_HEADS:o_a + 2 * DN_HEADS]
    pad_cols = lambda w: jnp.pad(w, ((0, 0), (0, 128 - DN_HEADS)))
    pad_lane = lambda v: jnp.pad(v, (0, 128 - DN_HEADS)).reshape(1, 128)
    pad_sub = lambda v: jnp.pad(v, (0, 8 - DN_HEADS)).reshape(8, 1)
    return {
        "norm_mix": norm_mix[l].reshape(1, D_MODEL),
        "wqkv": wl[:, :o_a].astype(BF16),
        "wab": jnp.concatenate([pad_cols(wa), pad_cols(wb)], axis=1).astype(BF16),
        "wabt": jnp.concatenate([wa, wb], axis=1).T.astype(BF16),
        "wrest": wl[:, o_a + 2 * DN_HEADS:].astype(BF16),
        "dn_conv_w": dn_conv_w[l],
        "alog_r": pad_lane(dn_a_log[l]), "dtb_r": pad_lane(dn_dt_bias[l]),
        "alog_c": pad_sub(dn_a_log[l]), "dtb_c": pad_sub(dn_dt_bias[l]),
        "dn_norm_w": dn_norm_w[l].reshape(1, DN_DK),
        "lru_conv_w": lru_conv_w[l], "lru_conv_b": lru_conv_b[l].reshape(1, LRU_W),
        "wgate": jnp.concatenate([_block_diag(lru_wa[l]), _block_diag(lru_wx[l])], axis=1).astype(BF16),
        "bgate": jnp.concatenate([lru_ba[l], lru_bx[l]]).reshape(1, 2 * LRU_W),
        "lam": lru_lambda[l].reshape(1, LRU_W),
        "w_out": w_out[l].astype(BF16),
    }


def kernel(x_prompt, x_sample, state_dn, state_dn_conv, state_lru, state_lru_conv, state_ret, meta_tokens, norm_ffn1, w_ffn1_in, w_ffn1_out, norm_mix, w_in, dn_conv_w, dn_a_log, dn_dt_bias, dn_norm_w, lru_conv_w, lru_conv_b, lru_wa, lru_ba, lru_wx, lru_bx, lru_lambda, w_out, norm_ffn2, w_ffn2_in, w_ffn2_out, norm_final):
    batch, seq, _ = x_prompt.shape
    n_dec = x_sample.shape[0]
    depth = w_in.shape[0]
    lws = [_layer_weights(l, norm_mix, w_in, dn_conv_w, dn_a_log, dn_dt_bias, dn_norm_w, lru_conv_w,
                          lru_conv_b, lru_wa, lru_ba, lru_wx, lru_bx, lru_lambda, w_out)
           for l in range(depth)]
    cos_m, sin_m = _rope_tables(jnp.arange(N_META))
    cos_p, sin_p = _rope_tables(N_META + jnp.arange(seq))
    cos_s, sin_s = _rope_tables(PAST_LEN + jnp.arange(1))

    n_small = n_dec + N_META
    xs = jnp.concatenate([x_sample[:, 0, :], meta_tokens.astype(F32)], axis=0)
    zero_init = (jnp.zeros((DN_HEADS, DN_DK, DN_DK), F32), jnp.zeros((8, 3 * DN_W), F32),
                 jnp.zeros((1, LRU_W), F32), jnp.zeros((8, LRU_W), F32), jnp.zeros((RET_W, RET_W), F32))
    new_s, meta_state, ffn_w = [], [], []
    sdn_all = sret_all = None
    sret_t = jnp.transpose(state_ret, (0, 2, 3, 4, 1))
    for l in range(depth):
        last = l == depth - 1
        xs, *f1 = _ffn_cast(xs, norm_ffn1[l], w_ffn1_in, w_ffn1_out, l, tf=FFN_DFF_SLICE)
        omix_part, retin, sdn_all, dnc, hl, lrc = _mix_step(
            xs[:n_dec], lws[l], cos_s, sin_s, l, state_dn, jnp.swapaxes(state_dn_conv[l], 0, 1),
            state_lru, jnp.swapaxes(state_lru_conv[l], 0, 1), sdn_all, bb=DECODE_BATCH_TILE)
        ys, sret_all = _ret_step(xs[:n_dec], omix_part, retin, lws[l]["w_out"], sret_t, l, sret_all,
                                 n=n_dec)
        ym, m_sdn, m_dnt, m_h, m_lrt, m_sretbd, _ = _mix_pipe(
            xs[n_dec:], lws[l], cos_m, sin_m, zero_init, batch=1, seq=N_META, tc=N_META, ck=N_META)
        new_s.append((jnp.swapaxes(dnc, 0, 1), hl, jnp.swapaxes(lrc, 0, 1)))
        meta_state.append((m_sdn[0], m_dnt[0], m_h[0], m_lrt[0], m_sretbd[0]))
        xs = jnp.concatenate([ys, ym], axis=0)
        xs, *f2 = _ffn_cast(xs, norm_ffn2[l], w_ffn2_in, w_ffn2_out, l, norm_final if last else None,
                            tf=FFN_DFF_SLICE)
        ffn_w.append((f1, f2))
    y_sample = xs[:n_dec].reshape(n_dec, 1, D_MODEL)

    xp = x_prompt.reshape(batch * seq, D_MODEL)
    new_p = []
    for l in range(depth):
        last = l == depth - 1
        xp = _ffn(xp, norm_ffn1[l], *ffn_w[l][0], tm=FFN_TOKEN_TILE, nf=D_FF // FFN_DFF_SLICE)
        xp, sdn, dnt, hl, lrt, _, sret = _mix_pipe(
            xp, lws[l], cos_p, sin_p, meta_state[l], batch=batch, seq=seq, tc=MIX_TIME_BLOCK, ck=CHUNK)
        new_p.append((sdn, dnt[:, 5:8], hl[:, 0], lrt[:, 5:8], sret))
        xp = _ffn(xp, norm_ffn2[l], *ffn_w[l][1], norm_final if last else None, tm=FFN_TOKEN_TILE,
                  nf=D_FF // FFN_DFF_SLICE)
    y_prompt = xp.reshape(batch, seq, D_MODEL)

    outs_p = [jnp.stack([s[j] for s in new_p]) for j in range(5)]
    dnc_s, lru_s, lrc_s = [jnp.stack([s[j] for s in new_s]) for j in range(3)]
    sret_s = jnp.transpose(sret_all, (0, 4, 1, 2, 3))
    return (y_prompt, y_sample, *outs_p, sdn_all, dnc_s, lru_s, lrc_s, sret_s)
```

```python
import functools
import math

import jax
import jax.numpy as jnp
from jax import lax
from jax.experimental import pallas as pl
from jax.experimental.pallas import tpu as pltpu

F32 = jnp.float32
BF16 = jnp.bfloat16

D_MODEL = 1024
N_META = 16
PAST_LEN = 16384
DN_HEADS = 4
DN_DK = 128
DN_W = 512
LRU_W = 256
LRU_BLOCKS = 4
LRU_C = 8.0
RET_HEADS = 4
RET_DK = 64
RET_W = 256
CONV_W = 4
CHUNK = 64
D_FF = 2816
ROPE_BASE = 10000.0
EPS = 1e-6
SUB = 16
LOG_GAMMA = tuple(math.log1p(-2.0 ** (-5.0 - h)) for h in range(RET_HEADS))

V7X_VMEM_BYTES = 64 * 1024 * 1024
VMEM_LIMIT_BYTES = V7X_VMEM_BYTES * 7 // 8
V7X_MXU_WIDTH = 256

FFN_TOKEN_TILE = 1024
FFN_DFF_SLICE = V7X_MXU_WIDTH
MIX_TIME_BLOCK = 4 * CHUNK
DECODE_BATCH_TILE = 8


def _mm(a, b):
    return jnp.dot(a.astype(BF16), b.astype(BF16), preferred_element_type=F32)


def _mm_nt(a, b):
    return lax.dot_general(a.astype(BF16), b.astype(BF16), (((1,), (1,)), ((), ())),
                           preferred_element_type=F32)


def _mm_tn(a, b):
    return lax.dot_general(a.astype(BF16), b.astype(BF16), (((0,), (0,)), ((), ())),
                           preferred_element_type=F32)


def _split3(x):
    hi = x.astype(BF16)
    r = x - hi.astype(F32)
    mid = r.astype(BF16)
    lo = (r - mid.astype(F32)).astype(BF16)
    return hi, mid, lo


def _mm_01_right(x, m01):
    m = m01.astype(BF16)
    return sum(jnp.dot(p, m, preferred_element_type=F32) for p in _split3(x))


def _mm_split(x, m_bf16):
    hi = x.astype(BF16)
    lo = (x - hi.astype(F32)).astype(BF16)
    return (jnp.dot(hi, m_bf16, preferred_element_type=F32)
            + jnp.dot(lo, m_bf16, preferred_element_type=F32))


def _rmsnorm(x, w):
    return x * lax.rsqrt(jnp.mean(x * x, axis=-1, keepdims=True) + EPS) * w


def _sigmoid(x):
    return 0.5 + 0.5 * jnp.tanh(0.5 * x)


def _silu(x):
    return x * _sigmoid(x)


def _causal_conv(x, tail, w):
    tc, n = x.shape
    nt8 = tc // 8
    first = _iota((nt8, 8, n), 1) == 0
    xm1, xm2, xm3 = tail[7:8], tail[6:7], tail[5:6]

    def shift1(s, carry):
        r = pltpu.roll(s.reshape(nt8, 8, n), 1, 1)
        before = jnp.concatenate([jnp.broadcast_to(carry, (1, 8, n)), r[:-1]], axis=0)
        return jnp.where(first, before, r).reshape(tc, n)

    s = shift1(x * w[0:1], xm1 * w[0:1])
    s = shift1(s + x * w[1:2], xm2 * w[0:1] + xm1 * w[1:2])
    s = shift1(s + x * w[2:3], xm3 * w[0:1] + xm2 * w[1:2] + xm1 * w[2:3])
    return s + x * w[3:4]


def _softplus(x):
    return jnp.maximum(x, 0.0) + jnp.log1p(jnp.exp(-jnp.abs(x)))


def _iota(shape, dim):
    return lax.broadcasted_iota(jnp.int32, shape, dim)


def _lane_log_gamma(shape, dim, width):
    head = _iota(shape, dim) // width
    out = jnp.full(shape, LOG_GAMMA[0], F32)
    for h in range(1, RET_HEADS):
        out = jnp.where(head == h, LOG_GAMMA[h], out)
    return out


def _rope(x, cos, sin_signed):
    n = x.shape[1]
    half = RET_DK // 2
    first = (_iota(x.shape, 1) % RET_DK) < half
    swapped = jnp.where(first, pltpu.roll(x, n - half, 1), pltpu.roll(x, half, 1))
    return x * cos + swapped * sin_signed


def _ret_norm_gate(o, gate, avg_bf16):
    mu = _mm_split(o, avg_bf16)
    d = o - mu
    var = _mm_split(d * d, avg_bf16)
    return d * lax.rsqrt(var + EPS) * _silu(gate)


def _head_avg_matrix():
    r = _iota((RET_W, RET_W), 0) // RET_DK
    c = _iota((RET_W, RET_W), 1) // RET_DK
    return jnp.where(r == c, 1.0 / RET_DK, 0.0).astype(BF16)


def _lru_coeffs(xl, wgate, bgate, lam):
    gates = _mm(xl, wgate) + bgate
    r = _sigmoid(gates[:, :LRU_W])
    i = _sigmoid(gates[:, LRU_W:])
    log_a = -LRU_C * r * _softplus(-lam)
    a = jnp.exp(log_a)
    b = jnp.sqrt(jnp.maximum(-jnp.tanh(log_a) * (a * a + 1.0), 0.0)) * (i * xl)
    return a, b


def _ffn_body(*refs, final, nf):
    if final:
        x_ref, nw_ref, wg_ref, wu_ref, wo_ref, fw_ref, o_ref = refs
    else:
        x_ref, nw_ref, wg_ref, wu_ref, wo_ref, o_ref = refs
    tf = D_FF // nf
    x = x_ref[...]
    u = _rmsnorm(x, nw_ref[...]).astype(BF16)
    acc = None
    for j in range(nf):
        gate = jnp.dot(u, wg_ref[:, j * tf:(j + 1) * tf], preferred_element_type=F32)
        up = jnp.dot(u, wu_ref[:, j * tf:(j + 1) * tf], preferred_element_type=F32)
        h = (_silu(gate) * up).astype(BF16)
        part = jnp.dot(h, wo_ref[j * tf:(j + 1) * tf, :], preferred_element_type=F32)
        acc = part if acc is None else acc + part
    y = x + 0.5 * acc
    if final:
        y = _rmsnorm(y, fw_ref[...])
    o_ref[...] = y


def _ffn(x, norm_w, wg, wu, wo, final_w=None, *, tm, nf):
    n = x.shape[0]
    final = final_w is not None
    resident = pl.Buffered(1)
    in_specs = [
        pl.BlockSpec((tm, D_MODEL), lambda i: (i, 0)),
        pl.BlockSpec((1, D_MODEL), lambda i: (0, 0)),
        pl.BlockSpec((D_MODEL, D_FF), lambda i: (0, 0), pipeline_mode=resident),
        pl.BlockSpec((D_MODEL, D_FF), lambda i: (0, 0), pipeline_mode=resident),
        pl.BlockSpec((D_FF, D_MODEL), lambda i: (0, 0), pipeline_mode=resident),
    ]
    args = [x, norm_w.reshape(1, D_MODEL), wg, wu, wo]
    if final:
        in_specs.append(pl.BlockSpec((1, D_MODEL), lambda i: (0, 0)))
        args.append(final_w.reshape(1, D_MODEL))
    return pl.pallas_call(
        functools.partial(_ffn_body, final=final, nf=nf),
        grid=(n // tm,),
        in_specs=in_specs,
        out_specs=pl.BlockSpec((tm, D_MODEL), lambda i: (i, 0)),
        out_shape=jax.ShapeDtypeStruct((n, D_MODEL), F32),
        compiler_params=pltpu.CompilerParams(
            dimension_semantics=("arbitrary",), vmem_limit_bytes=VMEM_LIMIT_BYTES),
        name="ffn_final" if final else "ffn",
    )(*args)


def _ffn_cast_body(*refs, final):
    if final:
        x_ref, nw_ref, wg_ref, wu_ref, wo_ref, fw_ref, o_ref, wg_o, wu_o, wo_o, u_s, acc_s = refs
    else:
        x_ref, nw_ref, wg_ref, wu_ref, wo_ref, o_ref, wg_o, wu_o, wo_o, u_s, acc_s = refs
    j = pl.program_id(0)

    @pl.when(j == 0)
    def _():
        u_s[...] = _rmsnorm(x_ref[...], nw_ref[...]).astype(BF16)
        acc_s[...] = jnp.zeros_like(acc_s)

    wg = wg_ref[...].astype(BF16)
    wu = wu_ref[...].astype(BF16)
    wo = wo_ref[...].astype(BF16)
    wg_o[...] = wg
    wu_o[...] = wu
    wo_o[...] = wo
    u = u_s[...]
    gate = jnp.dot(u, wg, preferred_element_type=F32)
    up = jnp.dot(u, wu, preferred_element_type=F32)
    acc_s[...] += jnp.dot((_silu(gate) * up).astype(BF16), wo, preferred_element_type=F32)

    @pl.when(j == pl.num_programs(0) - 1)
    def _():
        y = x_ref[...] + 0.5 * acc_s[...]
        if final:
            y = _rmsnorm(y, fw_ref[...])
        o_ref[...] = y


def _ffn_cast(x, norm_w, w_in, w_out, layer, final_w=None, *, tf):
    n = x.shape[0]
    nf = D_FF // tf
    final = final_w is not None
    in_specs = [
        pl.BlockSpec((n, D_MODEL), lambda j: (0, 0)),
        pl.BlockSpec((1, D_MODEL), lambda j: (0, 0)),
        pl.BlockSpec((None, D_MODEL, tf), lambda j: (layer, 0, j)),
        pl.BlockSpec((None, D_MODEL, tf), lambda j: (layer, 0, j + nf)),
        pl.BlockSpec((None, tf, D_MODEL), lambda j: (layer, j, 0)),
    ]
    args = [x, norm_w.reshape(1, D_MODEL), w_in, w_in, w_out]
    if final:
        in_specs.append(pl.BlockSpec((1, D_MODEL), lambda j: (0, 0)))
        args.append(final_w.reshape(1, D_MODEL))
    return pl.pallas_call(
        functools.partial(_ffn_cast_body, final=final),
        grid=(nf,),
        in_specs=in_specs,
        out_specs=[pl.BlockSpec((n, D_MODEL), lambda j: (0, 0)),
                   pl.BlockSpec((D_MODEL, tf), lambda j: (0, j)),
                   pl.BlockSpec((D_MODEL, tf), lambda j: (0, j)),
                   pl.BlockSpec((tf, D_MODEL), lambda j: (j, 0))],
        out_shape=[jax.ShapeDtypeStruct((n, D_MODEL), F32),
                   jax.ShapeDtypeStruct((D_MODEL, D_FF), BF16),
                   jax.ShapeDtypeStruct((D_MODEL, D_FF), BF16),
                   jax.ShapeDtypeStruct((D_FF, D_MODEL), BF16)],
        scratch_shapes=[pltpu.VMEM((n, D_MODEL), BF16), pltpu.VMEM((n, D_MODEL), F32)],
        compiler_params=pltpu.CompilerParams(
            dimension_semantics=("arbitrary",), vmem_limit_bytes=VMEM_LIMIT_BYTES),
        name="ffn_cast_final" if final else "ffn_cast",
    )(*args)


def _heads_per_group(ck):
    return max(1, min(DN_HEADS, 128 // ck))


def _tri_inv_steps(a, n_sub, out):
    m = a[0].shape[0]
    r = _iota((m, m), 0)
    c = _iota((m, m), 1)
    eye = jnp.where(r == c, 1.0, 0.0).astype(F32)
    diag = (r // SUB) == (c // SUB)
    x = [jnp.where(diag, ai, 0.0) for ai in a]
    off = [ai - xi for ai, xi in zip(a, x)]
    p = [eye - xi for xi in x]
    for _ in range(3):
        x = [_mm(xi, xi) for xi in x]
        yield
        p = [pi + _mm(pi, xi) for pi, xi in zip(p, x)]
        yield
    if n_sub == 1:
        out.extend(p)
        return
    n = [_mm(pi, oi) for pi, oi in zip(p, off)]
    yield
    n2 = [_mm(ni, ni) for ni in n]
    yield
    rr = [eye - ni + n2i - _mm(ni, n2i) for ni, n2i in zip(n, n2)]
    yield
    out.extend(_mm(ri, pi) for ri, pi in zip(rr, p))
    yield


def _mix_pipe_body(x_ref, nw_ref, wqkv_ref, wabt_ref, wrest_ref, dncw_ref,
                   alog_c_ref, dtb_c_ref, dnnw_ref,
                   lcw_ref, lcb_ref, wgate_ref, bgate_ref, lam_ref, cos_ref, sin_ref, wout_ref,
                   sdn0_ref, dnt0_ref, h0_ref, lrt0_ref, sret0_ref,
                   y_ref, sdn_o_ref, dnt_o_ref, h_o_ref, lrt_o_ref, sretbd_o_ref, sret_o_ref,
                   sdn, sret, hst, dn_tail, lru_tail, q_s, k_s, v_s, gb_s, grow_s, odn_s, oret_s, omix_s,
                   h_uv, h_wq, h_pm, h_kd, h_egl, h_rq, h_oi, h_su, h_z, h_rg, h_x, h_lru,
                   *, tc, ck, nt, n_blocks):
    i = pl.program_id(0)
    t_a = i % nt
    t_b = (i + nt - 1) % nt
    nc = tc // ck
    hc = DN_HEADS * ck
    n_sub = ck // SUB
    chunks = range(nc)
    rows = [slice(c * ck, (c + 1) * ck) for c in chunks]
    heads = [(h * ck, (h + 1) * ck) for h in range(DN_HEADS)]
    hg = _heads_per_group(ck)
    ng = DN_HEADS // hg
    gc = hg * ck
    probs = [(c, g) for c in chunks for g in range(ng)]

    @pl.when(i == 0)
    def _():
        for ref in (h_uv, h_wq, h_pm, h_kd, h_egl, h_rq, h_oi, h_su, h_z, h_rg, h_x, h_lru):
            ref[...] = jnp.zeros_like(ref)

    @pl.when(t_a == 0)
    def _():
        hst[...] = h0_ref[...]
        dn_tail[...] = dnt0_ref[...]
        lru_tail[...] = lrt0_ref[...]

    @pl.when((t_b == 0) | (i == 0))
    def _():
        sdn[...] = sdn0_ref[...]
        sret[...] = sret0_ref[...]

    sr = _iota((gc, gc), 0)
    sc = _iota((gc, gc), 1)
    strict = ((sr // ck) == (sc // ck)) & (sr > sc)
    eye_g = jnp.where(sr == sc, 1.0, 0.0).astype(F32)
    lg_rows = _lane_log_gamma((hc, ck), 0, ck)
    tpos = _iota((hc, ck), 0) % ck
    ipos = _iota((hc, ck), 1)
    causal = tpos >= ipos
    ret_dec = jnp.where(causal, jnp.exp(jnp.where(causal, (tpos - ipos).astype(F32) * lg_rows, 0.0)), 0.0)
    head_sel = (_iota((hc, RET_W), 0) // ck) == (_iota((hc, RET_W), 1) // RET_DK)
    lg_lane = _lane_log_gamma((ck, RET_W), 1, RET_DK)
    tl = _iota((ck, RET_W), 0).astype(F32)
    ret_eg = jnp.exp((tl + 1.0) * lg_lane)
    ret_kdec = jnp.exp((ck - 1.0 - tl) * lg_lane)
    ret_gc = jnp.exp(ck * _lane_log_gamma((1, RET_W), 1, RET_DK))
    bd = (_iota((RET_W, RET_W), 0) // RET_DK) == (_iota((RET_W, RET_W), 1) // RET_DK)

    a = {}
    b = {}

    def a_norm():
        a["x"] = x_ref[...]
        a["u"] = _rmsnorm(a["x"], nw_ref[...]).astype(BF16)

    def a_qkv_piece(p):
        cols = slice(2 * DN_DK * p, 2 * DN_DK * (p + 1))
        a["pre", p] = jnp.dot(a["u"], wqkv_ref[:, cols], preferred_element_type=F32)

    def a_conv_piece(p):
        cols = slice(2 * DN_DK * p, 2 * DN_DK * (p + 1))
        pre = a.pop(("pre", p))
        conv = _causal_conv(pre, dn_tail[:, cols], dncw_ref[:, cols])
        dn_tail[:, cols] = pre[tc - 8:tc]
        act = _silu(conv)
        kind, first_head = divmod(2 * p, DN_HEADS)
        for j in range(2):
            v = act[:, j * DN_DK:(j + 1) * DN_DK]
            if kind == 0:
                q_s[first_head + j] = (v * lax.rsqrt(jnp.sum(v * v, axis=-1, keepdims=True) + EPS)
                                       * (DN_DK ** -0.5))
            elif kind == 1:
                k_s[first_head + j] = v * lax.rsqrt(jnp.sum(v * v, axis=-1, keepdims=True) + EPS)
            else:
                v_s[first_head + j] = v

    def proj(lo, hi):
        return jnp.dot(a["u"], wrest_ref[:, lo:hi], preferred_element_type=F32)

    def a_lru_in():
        lru = proj(512, 1024)
        xl = _causal_conv(lru[:, :LRU_W], lru_tail[...], lcw_ref[...])
        lru_tail[...] = lru[tc - 8:tc, :LRU_W]
        a["xl"] = xl + lcb_ref[...]
        a["lru_y"] = lru[:, LRU_W:]

    def a_ret_q():
        a["ret_q"] = proj(1024, 1280)

    def a_ret_kv():
        kv = proj(1280, 1792)
        cos = cos_ref[...]
        sin = sin_ref[...]
        a["rq"] = _rope(a["ret_q"], cos, sin)
        a["rk"] = _rope(kv[:, :RET_W], cos, sin) * (RET_DK ** -0.5)
        a["rv"] = kv[:, RET_W:]

    def a_z():
        a["dn_z"] = proj(0, 512)

    def a_g():
        a["ret_g"] = proj(1792, 2048)

    def a_gates():
        abt = lax.dot_general(wabt_ref[...], a["u"], (((1,), (1,)), ((), ())), preferred_element_type=F32)
        g_row = -jnp.exp(alog_c_ref[...]) * _softplus(abt + dtb_c_ref[...])
        rr = _iota((tc, tc), 0)
        cc = _iota((tc, tc), 1)
        upper = jnp.where(((rr // ck) == (cc // ck)) & (rr <= cc), 1.0, 0.0)
        big_g_row = _mm_01_right(g_row, upper)
        is_g = _iota((8, tc), 0) < DN_HEADS
        gb_s[...] = jnp.where(is_g, big_g_row, _sigmoid(abt)).T
        for c in chunks:
            grow_s[c] = jnp.concatenate(
                [big_g_row[h:h + 1, c * ck:(c + 1) * ck] for h in range(DN_HEADS)], axis=1)

    def a_lru_scan_steps():
        ga, gb = _lru_coeffs(a["xl"], wgate_ref[...], bgate_ref[...], lam_ref[...])
        nt8 = tc // 8
        ga = ga.reshape(nt8, 8, LRU_W)
        gb = gb.reshape(nt8, 8, LRU_W)
        row = _iota((nt8, 8, LRU_W), 1)
        yield
        for s in (1, 2, 4):
            keep = row >= s
            gb = ga * jnp.where(keep, pltpu.roll(gb, s, 1), 0.0) + gb
            ga = ga * jnp.where(keep, pltpu.roll(ga, s, 1), 1.0)
            yield
        carry = hst[...]
        h_in = []
        for t8 in range(nt8):
            h_in.append(carry)
            carry = ga[t8, 7:8] * carry + gb[t8, 7:8]
        hst[...] = carry
        yield
        h = gb + ga * jnp.stack(h_in, axis=0)
        a["o_lru"] = (h.reshape(tc, LRU_W) * jax.nn.gelu(a["lru_y"])).astype(BF16)
        yield

    def a_kkqk():
        def stack(ref, c, g):
            return jnp.concatenate([ref[h, rows[c], :] for h in range(g * hg, (g + 1) * hg)], axis=0)

        def stack_col(c, g, first):
            return jnp.concatenate([gb_s[rows[c], first + h:first + h + 1]
                                    for h in range(g * hg, (g + 1) * hg)], axis=0)

        a["kst"] = [stack(k_s, c, g) for c, g in probs]
        a["qst"] = [stack(q_s, c, g) for c, g in probs]
        a["vst"] = [stack(v_s, c, g) for c, g in probs]
        gst = [stack_col(c, g, 0) for c, g in probs]
        a["gst"] = gst
        a["bst"] = [stack_col(c, g, DN_HEADS) for c, g in probs]
        dstrict = []
        for k, (c, g) in enumerate(probs):
            diff = gst[k] - grow_s[c][:, g * gc:(g + 1) * gc]
            dstrict.append(jnp.where(strict, jnp.exp(jnp.where(strict, diff, 0.0)), 0.0))
        a["eg"] = [jnp.exp(g) for g in gst]
        a["amat"] = [a["bst"][k] * _mm_nt(a["kst"][k], a["kst"][k]) * dstrict[k] for k in range(len(probs))]
        a["pmat"] = [(_mm_nt(a["qst"][k], a["kst"][k]) * (dstrict[k] + eye_g)).astype(BF16)
                     for k in range(len(probs))]

    tinv = []

    def a_sol():
        kst, qst, vst, gst, bst, eg = (a[k] for k in ("kst", "qst", "vst", "gst", "bst", "eg"))
        sol = [_mm(tinv[k], jnp.concatenate([bst[k] * vst[k], (bst[k] * eg[k]) * kst[k]], axis=1))
               for k in range(len(probs))]
        a["uv"] = [s[:, :DN_DK] for s in sol]
        wq, egl, kd = {}, {}, {}
        for k, (c, g) in enumerate(probs):
            for j in range(hg):
                lo, hi = j * ck, (j + 1) * ck
                key = (c, g * hg + j)
                wq[key] = jnp.concatenate([sol[k][lo:hi, DN_DK:], eg[k][lo:hi] * qst[k][lo:hi]],
                                          axis=0).astype(BF16)
                g_last = gst[k][hi - 1:hi]
                egl[key] = jnp.exp(g_last)
                kd[key] = (kst[k][lo:hi] * jnp.exp(g_last - gst[k][lo:hi])).astype(BF16)
        a["wq"], a["egl"], a["kd"] = wq, egl, kd

    def a_ret_scores():
        qc = [a["rq"][rs] for rs in rows]
        kc = [a["rk"][rs] for rs in rows]
        qp = [jnp.where(head_sel, jnp.concatenate([q] * RET_HEADS, axis=0), 0.0) for q in qc]
        a["qc"] = qc
        a["sc"] = [(_mm_nt(qp[c], kc[c]) * ret_dec).astype(BF16) for c in chunks]

    def a_ret_intra():
        vc = [a["rv"][rs].astype(BF16) for rs in rows]
        o_intra = []
        for c in chunks:
            op = jnp.where(head_sel, jnp.dot(a["sc"][c], vc[c], preferred_element_type=F32), 0.0)
            acc = op[0:ck]
            for h in range(1, RET_HEADS):
                acc = acc + op[h * ck:(h + 1) * ck]
            o_intra.append(acc)
        a["o_intra"] = o_intra
        a["s_upd"] = [jnp.where(bd, _mm_tn(a["rk"][rows[c]] * ret_kdec, vc[c]), 0.0) for c in chunks]

    def a_handoff():
        for k, (c, g) in enumerate(probs):
            h_uv[c, g * gc:(g + 1) * gc, :] = a["uv"][k]
            h_pm[c, g] = a["pmat"][k]
        for c in chunks:
            h_rq[c] = a["qc"][c].astype(BF16)
            h_oi[c] = a["o_intra"][c]
            h_su[c] = a["s_upd"][c]
            for h in range(DN_HEADS):
                h_wq[c, h] = a["wq"][c, h]
                h_kd[c, h] = a["kd"][c, h]
                h_egl[c, h:h + 1, :] = jnp.broadcast_to(a["egl"][c, h], (1, DN_DK))
        h_z[...] = a["dn_z"]
        h_rg[...] = a["ret_g"]
        h_x[...] = a["x"]
        h_lru[...] = a["o_lru"]

    def b_post_a(c):
        ws_qs = [_mm(h_wq[c, h], sdn[h]) for h in range(DN_HEADS)]
        b["u"] = [h_uv[c, lo:hi, :] - ws_qs[h][:ck] for h, (lo, hi) in enumerate(heads)]
        b["qs"] = [w[ck:] for w in ws_qs]

    def b_post_b(c):
        for g in range(ng):
            group = range(g * hg, (g + 1) * hg)
            ost = (jnp.concatenate([b["qs"][h] for h in group], axis=0)
                   + _mm(h_pm[c, g], jnp.concatenate([b["u"][h] for h in group], axis=0)))
            for j, h in enumerate(group):
                odn_s[rows[c], h * DN_DK:(h + 1) * DN_DK] = ost[j * ck:(j + 1) * ck]
        for h in range(DN_HEADS):
            sdn[h] = h_egl[c, h:h + 1, :] * sdn[h] + _mm_tn(h_kd[c, h], b["u"][h])

    def b_ret():
        for c in chunks:
            s0 = sret[...]
            oret_s[rows[c], :] = ret_eg * _mm(h_rq[c], s0) + h_oi[c]
            sret[...] = ret_gc * s0 + h_su[c]

    def b_out():
        nw = dnnw_ref[...]
        for h in range(DN_HEADS):
            sl = slice(h * DN_DK, (h + 1) * DN_DK)
            oh = odn_s[:, sl]
            oh = oh * lax.rsqrt(jnp.mean(oh * oh, axis=-1, keepdims=True) + EPS) * nw
            omix_s[:, sl] = (oh * _silu(h_z[:, sl])).astype(BF16)
        omix_s[:, 512:768] = h_lru[...]
        omix_s[:, 768:1024] = _ret_norm_gate(oret_s[...], h_rg[...], _head_avg_matrix()).astype(BF16)
        y_ref[...] = h_x[...] + jnp.dot(omix_s[...], wout_ref[...], preferred_element_type=F32)

    n_piece = 3 * DN_W // (2 * DN_DK)
    a_seq = [a_norm]
    for p in range(n_piece):
        a_seq.append(lambda p=p: (a_qkv_piece(p), a_conv_piece(p)))
    a_seq += [a_lru_in, lambda: (a_gates(), a_ret_q()), a_ret_kv, a_z, a_g, a_kkqk]
    b_seq = []
    for c in chunks:
        b_seq += [functools.partial(b_post_a, c), functools.partial(b_post_b, c)]
    b_seq += [b_ret, b_out]
    b_seq.pop(0)()
    for fa in a_seq:
        fa()
        if b_seq:
            b_seq.pop(0)()
    for fb in b_seq:
        fb()
    scan = a_lru_scan_steps()
    for _ in _tri_inv_steps(a["amat"], n_sub, tinv):
        next(scan, None)
    for _ in scan:
        pass
    a_sol()
    a_ret_scores()
    a_ret_intra()
    a_handoff()

    @pl.when((t_a == nt - 1) & (i < n_blocks))
    def _():
        dnt_o_ref[0] = dn_tail[...]
        h_o_ref[0] = hst[...]
        lrt_o_ref[0] = lru_tail[...]

    @pl.when((t_b == nt - 1) & (i > 0))
    def _():
        sdn_o_ref[0] = sdn[...]
        s_bd = sret[...]
        sretbd_o_ref[0] = s_bd
        for h in range(RET_HEADS):
            sret_o_ref[0, h] = s_bd[h * RET_DK:(h + 1) * RET_DK, h * RET_DK:(h + 1) * RET_DK]


def _mix_pipe(x, lw, cos, sin, init, *, batch, seq, tc, ck):
    nt = seq // tc
    nc = tc // ck
    hc = DN_HEADS * ck
    hg = _heads_per_group(ck)
    ng, gc = DN_HEADS // hg, hg * ck
    n_blocks = batch * nt
    blk_a = lambda i: jnp.minimum(i, n_blocks - 1)
    blk_b = lambda i: jnp.maximum(i - 1, 0)
    const = lambda shape: pl.BlockSpec(shape, lambda i: (0,) * len(shape))
    in_specs = [
        pl.BlockSpec((tc, D_MODEL), lambda i: (blk_a(i), 0)),
        const((1, D_MODEL)), const((D_MODEL, 3 * DN_W)), const((8, D_MODEL)),
        const((D_MODEL, 2048)), const((CONV_W, 3 * DN_W)),
        const((8, 1)), const((8, 1)), const((1, DN_DK)),
        const((CONV_W, LRU_W)), const((1, LRU_W)), const((LRU_W, 2 * LRU_W)), const((1, 2 * LRU_W)),
        const((1, LRU_W)),
        pl.BlockSpec((tc, RET_W), lambda i: (blk_a(i) % nt, 0)),
        pl.BlockSpec((tc, RET_W), lambda i: (blk_a(i) % nt, 0)),
        const((D_MODEL, D_MODEL)),
        const((DN_HEADS, DN_DK, DN_DK)), const((8, 3 * DN_W)), const((1, LRU_W)), const((8, LRU_W)),
        const((RET_W, RET_W)),
    ]
    out_shape = [
        jax.ShapeDtypeStruct((batch * seq, D_MODEL), F32),
        jax.ShapeDtypeStruct((batch, DN_HEADS, DN_DK, DN_DK), F32),
        jax.ShapeDtypeStruct((batch, 8, 3 * DN_W), F32),
        jax.ShapeDtypeStruct((batch, 1, LRU_W), F32),
        jax.ShapeDtypeStruct((batch, 8, LRU_W), F32),
        jax.ShapeDtypeStruct((batch, RET_W, RET_W), F32),
        jax.ShapeDtypeStruct((batch, RET_HEADS, RET_DK, RET_DK), F32),
    ]
    seq_a = lambda i: blk_a(i) // nt
    seq_b = lambda i: blk_b(i) // nt
    out_specs = [
        pl.BlockSpec((tc, D_MODEL), lambda i: (blk_b(i), 0)),
        pl.BlockSpec((1, DN_HEADS, DN_DK, DN_DK), lambda i: (seq_b(i), 0, 0, 0)),
        pl.BlockSpec((1, 8, 3 * DN_W), lambda i: (seq_a(i), 0, 0)),
        pl.BlockSpec((1, 1, LRU_W), lambda i: (seq_a(i), 0, 0)),
        pl.BlockSpec((1, 8, LRU_W), lambda i: (seq_a(i), 0, 0)),
        pl.BlockSpec((1, RET_W, RET_W), lambda i: (seq_b(i), 0, 0)),
        pl.BlockSpec((1, RET_HEADS, RET_DK, RET_DK), lambda i: (seq_b(i), 0, 0, 0)),
    ]
    scratch = [
        pltpu.VMEM((DN_HEADS, DN_DK, DN_DK), F32),
        pltpu.VMEM((RET_W, RET_W), F32),
        pltpu.VMEM((1, LRU_W), F32),
        pltpu.VMEM((8, 3 * DN_W), F32),
        pltpu.VMEM((8, LRU_W), F32),
        pltpu.VMEM((DN_HEADS, tc, DN_DK), F32),
        pltpu.VMEM((DN_HEADS, tc, DN_DK), F32),
        pltpu.VMEM((DN_HEADS, tc, DN_DK), F32),
        pltpu.VMEM((tc, 8), F32),
        pltpu.VMEM((nc, 1, hc), F32),
        pltpu.VMEM((tc, DN_W), F32),
        pltpu.VMEM((tc, RET_W), F32),
        pltpu.VMEM((tc, D_MODEL), BF16),
        pltpu.VMEM((nc, hc, DN_DK), F32),
        pltpu.VMEM((nc, DN_HEADS, 2 * ck, DN_DK), BF16),
        pltpu.VMEM((nc, ng, gc, gc), BF16),
        pltpu.VMEM((nc, DN_HEADS, ck, DN_DK), BF16),
        pltpu.VMEM((nc, 8, DN_DK), F32),
        pltpu.VMEM((nc, ck, RET_W), BF16),
        pltpu.VMEM((nc, ck, RET_W), F32),
        pltpu.VMEM((nc, RET_W, RET_W), F32),
        pltpu.VMEM((tc, DN_W), F32),
        pltpu.VMEM((tc, RET_W), F32),
        pltpu.VMEM((tc, D_MODEL), F32),
        pltpu.VMEM((tc, LRU_W), BF16),
    ]
    args = [x, lw["norm_mix"], lw["wqkv"], lw["wabt"], lw["wrest"], lw["dn_conv_w"],
            lw["alog_c"], lw["dtb_c"], lw["dn_norm_w"],
            lw["lru_conv_w"], lw["lru_conv_b"], lw["wgate"], lw["bgate"], lw["lam"], cos, sin,
            lw["w_out"], *init]
    return pl.pallas_call(
        functools.partial(_mix_pipe_body, tc=tc, ck=ck, nt=nt, n_blocks=n_blocks),
        grid=(n_blocks + 1,),
        in_specs=in_specs,
        out_specs=out_specs,
        out_shape=out_shape,
        scratch_shapes=scratch,
        compiler_params=pltpu.CompilerParams(
            dimension_semantics=("arbitrary",), vmem_limit_bytes=VMEM_LIMIT_BYTES),
        name=f"mix_pipe_c{ck}",
    )(*args)


def _mix_step_body(*refs, bb, layer):
    (x_ref, nw_ref, wqkv_ref, wab_ref, wrest_ref, dncw_ref, alog_r_ref, dtb_r_ref,
     dnnw_ref, lcw_ref, lcb_ref, wgate_ref, bgate_ref, lam_ref, cos_ref, sin_ref,
     sdn_ref, dnc_ref, h_ref, lrc_ref) = refs[:20]
    refs = refs[20:]
    if layer > 0:
        prev_sdn_ref = refs[0]
        refs = refs[1:]
    (omix_o_ref, retin_o_ref, sdn_o_ref, dnc_o_ref, h_o_ref, lrc_o_ref,
     q_s, k_s, v_s, eg_s, beta_s, odn_s, dnz_s) = refs
    i = pl.program_id(0)
    if layer > 0:
        sdn_o_ref[0:layer] = prev_sdn_ref[...]

    @pl.when(i == 0)
    def _():
        x = x_ref[...]
        u = _rmsnorm(x, nw_ref[...]).astype(BF16)

        qkv_pre = jnp.dot(u, wqkv_ref[...], preferred_element_type=F32)
        cw = dncw_ref[...]
        conv = (dnc_ref[0] * cw[0:1, :] + dnc_ref[1] * cw[1:2, :] + dnc_ref[2] * cw[2:3, :]
                + qkv_pre * cw[3:4, :])
        dnc_o_ref[0] = dnc_ref[1]
        dnc_o_ref[1] = dnc_ref[2]
        dnc_o_ref[2] = qkv_pre
        qkv = _silu(conv)
        for h in range(DN_HEADS):
            sl = slice(h * DN_DK, (h + 1) * DN_DK)
            qh = qkv[:, sl]
            kh = qkv[:, DN_W + h * DN_DK:DN_W + (h + 1) * DN_DK]
            q_s[:, sl] = qh * lax.rsqrt(jnp.sum(qh * qh, axis=-1, keepdims=True) + EPS) * (DN_DK ** -0.5)
            k_s[:, sl] = kh * lax.rsqrt(jnp.sum(kh * kh, axis=-1, keepdims=True) + EPS)
        v_s[...] = qkv[:, 2 * DN_W:]
        ab = jnp.dot(u, wab_ref[...], preferred_element_type=F32)
        eg_s[...] = jnp.exp(-jnp.exp(alog_r_ref[...]) * _softplus(ab[:, :128] + dtb_r_ref[...]))
        beta_s[...] = _sigmoid(ab[:, 128:])

        rest = jnp.dot(u, wrest_ref[...], preferred_element_type=F32)
        dnz_s[...] = rest[:, 0:512]
        lru_x = rest[:, 512:768]
        lru_y = rest[:, 768:1024]

        lw = lcw_ref[...]
        xl = (lrc_ref[0] * lw[0:1, :] + lrc_ref[1] * lw[1:2, :] + lrc_ref[2] * lw[2:3, :]
              + lru_x * lw[3:4, :] + lcb_ref[...])
        lrc_o_ref[0] = lrc_ref[1]
        lrc_o_ref[1] = lrc_ref[2]
        lrc_o_ref[2] = lru_x
        a, b = _lru_coeffs(xl, wgate_ref[...], bgate_ref[...], lam_ref[...])
        h_new = a * h_ref[...] + b
        h_o_ref[...] = h_new
        omix_o_ref[:, DN_W:DN_W + LRU_W] = (h_new * jax.nn.gelu(lru_y)).astype(BF16)

        cos = cos_ref[...]
        sin = sin_ref[...]
        retin_o_ref[:, 0:256] = _rope(rest[:, 1024:1280], cos, sin)
        retin_o_ref[:, 256:512] = _rope(rest[:, 1280:1536], cos, sin) * (RET_DK ** -0.5)
        retin_o_ref[:, 512:768] = rest[:, 1536:1792]
        retin_o_ref[:, 768:1024] = rest[:, 1792:2048]

    for t8 in range(bb // 8):
        r0 = pl.multiple_of(i * bb + t8 * 8, 8)
        eg_t = eg_s[pl.ds(r0, 8), :]
        beta_t = beta_s[pl.ds(r0, 8), :]
        for h in range(DN_HEADS):
            sl = slice(h * DN_DK, (h + 1) * DN_DK)
            kt = k_s[pl.ds(r0, 8), sl].T
            qt = q_s[pl.ds(r0, 8), sl].T
            v_t = v_s[pl.ds(r0, 8), sl]
            rows = []
            for j in range(8):
                kcol = kt[:, j:j + 1]
                qcol = qt[:, j:j + 1]
                s0 = sdn_ref[t8 * 8 + j, h]
                eg = eg_t[j:j + 1, h:h + 1]
                ks = jnp.sum(s0 * kcol, axis=0, keepdims=True)
                qs = jnp.sum(s0 * qcol, axis=0, keepdims=True)
                qk = jnp.sum(qcol * kcol, axis=0, keepdims=True)
                uu = beta_t[j:j + 1, h:h + 1] * (v_t[j:j + 1, :] - eg * ks)
                rows.append(eg * qs + qk * uu)
                sdn_o_ref[layer, t8 * 8 + j, h] = eg * s0 + kcol * uu
            odn_s[pl.ds(r0, 8), sl] = jnp.concatenate(rows, axis=0)

    @pl.when(i == pl.num_programs(0) - 1)
    def _():
        nw = dnnw_ref[...]
        for h in range(DN_HEADS):
            sl = slice(h * DN_DK, (h + 1) * DN_DK)
            oh = odn_s[:, sl]
            oh = oh * lax.rsqrt(jnp.mean(oh * oh, axis=-1, keepdims=True) + EPS) * nw
            omix_o_ref[:, sl] = (oh * _silu(dnz_s[:, sl])).astype(BF16)


def _mix_step(x, lw, cos, sin, layer, sdn, dnc, hl, lrc, prev_sdn, *, bb):
    n = x.shape[0]
    const = lambda shape: pl.BlockSpec(shape, lambda i: (0,) * len(shape))
    in_specs = [
        const((n, D_MODEL)),
        const((1, D_MODEL)), const((D_MODEL, 3 * DN_W)), const((D_MODEL, 256)), const((D_MODEL, 2048)),
        const((CONV_W, 3 * DN_W)), const((1, 128)), const((1, 128)), const((1, DN_DK)),
        const((CONV_W, LRU_W)), const((1, LRU_W)), const((LRU_W, 2 * LRU_W)), const((1, 2 * LRU_W)),
        const((1, LRU_W)), const((1, RET_W)), const((1, RET_W)),
        pl.BlockSpec((None, bb, DN_HEADS, DN_DK, DN_DK), lambda i: (layer, i, 0, 0, 0)),
        const((CONV_W - 1, n, 3 * DN_W)),
        pl.BlockSpec((None, n, LRU_W), lambda i: (layer, 0, 0)),
        const((CONV_W - 1, n, LRU_W)),
    ]
    if layer > 0:
        in_specs.append(pl.BlockSpec((layer, bb, DN_HEADS, DN_DK, DN_DK), lambda i: (0, i, 0, 0, 0)))
    out_shape = [
        jax.ShapeDtypeStruct((n, DN_W + LRU_W), BF16),
        jax.ShapeDtypeStruct((n, 4 * RET_W), F32),
        jax.ShapeDtypeStruct((layer + 1, n, DN_HEADS, DN_DK, DN_DK), F32),
        jax.ShapeDtypeStruct((CONV_W - 1, n, 3 * DN_W), F32),
        jax.ShapeDtypeStruct((n, LRU_W), F32),
        jax.ShapeDtypeStruct((CONV_W - 1, n, LRU_W), F32),
    ]
    out_specs = [
        const((n, DN_W + LRU_W)),
        const((n, 4 * RET_W)),
        pl.BlockSpec((layer + 1, bb, DN_HEADS, DN_DK, DN_DK), lambda i: (0, i, 0, 0, 0)),
        const((CONV_W - 1, n, 3 * DN_W)),
        const((n, LRU_W)),
        const((CONV_W - 1, n, LRU_W)),
    ]
    scratch = [
        pltpu.VMEM((n, DN_W), F32), pltpu.VMEM((n, DN_W), F32), pltpu.VMEM((n, DN_W), F32),
        pltpu.VMEM((n, 128), F32), pltpu.VMEM((n, 128), F32),
        pltpu.VMEM((n, DN_W), F32), pltpu.VMEM((n, DN_W), F32),
    ]
    args = [x, lw["norm_mix"], lw["wqkv"], lw["wab"], lw["wrest"], lw["dn_conv_w"], lw["alog_r"],
            lw["dtb_r"], lw["dn_norm_w"], lw["lru_conv_w"], lw["lru_conv_b"], lw["wgate"], lw["bgate"],
            lw["lam"], cos, sin, sdn, dnc, hl, lrc]
    if layer > 0:
        args.append(prev_sdn)
    return pl.pallas_call(
        functools.partial(_mix_step_body, bb=bb, layer=layer),
        grid=(n // bb,),
        in_specs=in_specs,
        out_specs=out_specs,
        out_shape=out_shape,
        scratch_shapes=scratch,
        compiler_params=pltpu.CompilerParams(
            dimension_semantics=("arbitrary",), vmem_limit_bytes=VMEM_LIMIT_BYTES),
        name="mix_step",
    )(*args)


def _ret_step_body(*refs, layer):
    x_ref, omix_ref, retin_ref, wout_ref, s_ref = refs[:5]
    refs = refs[5:]
    if layer > 0:
        prev_ref = refs[0]
        refs = refs[1:]
    y_ref, s_o_ref, qt_s, kt_s, vt_s, ot_s = refs
    h = pl.program_id(0)
    if layer > 0:
        s_o_ref[0:layer] = prev_ref[...]

    @pl.when(h == 0)
    def _():
        qt_s[...] = retin_ref[:, 0:RET_W].T
        kt_s[...] = retin_ref[:, RET_W:2 * RET_W].T
        vt_s[...] = retin_ref[:, 2 * RET_W:3 * RET_W].T

    rows = pl.ds(pl.multiple_of(h * RET_DK, RET_DK), RET_DK)
    q = qt_s[rows, :]
    k = kt_s[rows, :]
    v = vt_s[rows, :]
    gamma = jnp.float32(math.exp(LOG_GAMMA[0]))
    for i in range(1, RET_HEADS):
        gamma = jnp.where(h == i, jnp.float32(math.exp(LOG_GAMMA[i])), gamma)
    acc = jnp.zeros_like(v)
    for d in range(RET_DK):
        s_d = s_ref[d]
        acc = acc + s_d * q[d:d + 1, :]
        s_o_ref[layer, 0, d] = gamma * s_d + k[d:d + 1, :] * v
    ot_s[rows, :] = gamma * acc + jnp.sum(q * k, axis=0, keepdims=True) * v

    @pl.when(h == pl.num_programs(0) - 1)
    def _():
        o_ret = _ret_norm_gate(ot_s[...].T, retin_ref[:, 3 * RET_W:], _head_avg_matrix())
        omix = jnp.concatenate([omix_ref[...], o_ret.astype(BF16)], axis=1)
        y_ref[...] = x_ref[...] + jnp.dot(omix, wout_ref[...], preferred_element_type=F32)


def _ret_step(x, omix_part, retin, w_out, sret_t, layer, prev, *, n):
    const = lambda shape: pl.BlockSpec(shape, lambda h: (0,) * len(shape))
    in_specs = [
        const((n, D_MODEL)), const((n, DN_W + LRU_W)), const((n, 4 * RET_W)), const((D_MODEL, D_MODEL)),
        pl.BlockSpec((None, None, RET_DK, RET_DK, n), lambda h: (layer, h, 0, 0, 0)),
    ]
    args = [x, omix_part, retin, w_out, sret_t]
    if layer > 0:
        in_specs.append(pl.BlockSpec((layer, 1, RET_DK, RET_DK, n), lambda h: (0, h, 0, 0, 0)))
        args.append(prev)
    return pl.pallas_call(
        functools.partial(_ret_step_body, layer=layer),
        grid=(RET_HEADS,),
        in_specs=in_specs,
        out_specs=[const((n, D_MODEL)),
                   pl.BlockSpec((layer + 1, 1, RET_DK, RET_DK, n), lambda h: (0, h, 0, 0, 0))],
        out_shape=[jax.ShapeDtypeStruct((n, D_MODEL), F32),
                   jax.ShapeDtypeStruct((layer + 1, RET_HEADS, RET_DK, RET_DK, n), F32)],
        scratch_shapes=[pltpu.VMEM((RET_W, n), F32)] * 4,
        compiler_params=pltpu.CompilerParams(
            dimension_semantics=("arbitrary",), vmem_limit_bytes=VMEM_LIMIT_BYTES),
        name="ret_step",
    )(*args)


def _rope_tables(pos):
    half = RET_DK // 2
    inv = ROPE_BASE ** (-jnp.arange(half, dtype=F32) / half)
    ang = pos.astype(F32)[:, None] * inv[None, :]
    cos = jnp.cos(ang)
    sin = jnp.sin(ang)
    cos_full = jnp.tile(jnp.concatenate([cos, cos], axis=-1), (1, RET_HEADS))
    sin_signed = jnp.tile(jnp.concatenate([-sin, sin], axis=-1), (1, RET_HEADS))
    return cos_full, sin_signed


def _block_diag(w):
    n, d, e = w.shape
    eye = jnp.eye(n, dtype=w.dtype)
    return (eye[:, None, :, None] * w[:, :, None, :]).reshape(n * d, n * e)


def _layer_weights(l, norm_mix, w_in, dn_conv_w, dn_a_log, dn_dt_bias, dn_norm_w, lru_conv_w, lru_conv_b,
                   lru_wa, lru_ba, lru_wx, lru_bx, lru_lambda, w_out):
    wl = w_in[l]
    o_a = 3 * DN_W
    wa = wl[:, o_a:o_a + DN_HEADS]
    wb = wl[:, o_a + DN_HEADS:o_a + 2 * DN_HEADS]
    pad_cols = lambda w: jnp.pad(w, ((0, 0), (0, 128 - DN_HEADS)))
    pad_lane = lambda v: jnp.pad(v, (0, 128 - DN_HEADS)).reshape(1, 128)
    pad_sub = lambda v: jnp.pad(v, (0, 8 - DN_HEADS)).reshape(8, 1)
    return {
        "norm_mix": norm_mix[l].reshape(1, D_MODEL),
        "wqkv": wl[:, :o_a].astype(BF16),
        "wab": jnp.concatenate([pad_cols(wa), pad_cols(wb)], axis=1).astype(BF16),
        "wabt": jnp.concatenate([wa, wb], axis=1).T.astype(BF16),
        "wrest": wl[:, o_a + 2 * DN_HEADS:].astype(BF16),
        "dn_conv_w": dn_conv_w[l],
        "alog_r": pad_lane(dn_a_log[l]), "dtb_r": pad_lane(dn_dt_bias[l]),
        "alog_c": pad_sub(dn_a_log[l]), "dtb_c": pad_sub(dn_dt_bias[l]),
        "dn_norm_w": dn_norm_w[l].reshape(1, DN_DK),
        "lru_conv_w": lru_conv_w[l], "lru_conv_b": lru_conv_b[l].reshape(1, LRU_W),
        "wgate": jnp.concatenate([_block_diag(lru_wa[l]), _block_diag(lru_wx[l])], axis=1).astype(BF16),
        "bgate": jnp.concatenate([lru_ba[l], lru_bx[l]]).reshape(1, 2 * LRU_W),
        "lam": lru_lambda[l].reshape(1, LRU_W),
        "w_out": w_out[l].astype(BF16),
    }


def kernel(x_prompt, x_sample, state_dn, state_dn_conv, state_lru, state_lru_conv, state_ret, meta_tokens, norm_ffn1, w_ffn1_in, w_ffn1_out, norm_mix, w_in, dn_conv_w, dn_a_log, dn_dt_bias, dn_norm_w, lru_conv_w, lru_conv_b, lru_wa, lru_ba, lru_wx, lru_bx, lru_lambda, w_out, norm_ffn2, w_ffn2_in, w_ffn2_out, norm_final):
    batch, seq, _ = x_prompt.shape
    n_dec = x_sample.shape[0]
    depth = w_in.shape[0]
    lws = [_layer_weights(l, norm_mix, w_in, dn_conv_w, dn_a_log, dn_dt_bias, dn_norm_w, lru_conv_w,
                          lru_conv_b, lru_wa, lru_ba, lru_wx, lru_bx, lru_lambda, w_out)
           for l in range(depth)]
    cos_m, sin_m = _rope_tables(jnp.arange(N_META))
    cos_p, sin_p = _rope_tables(N_META + jnp.arange(seq))
    cos_s, sin_s = _rope_tables(PAST_LEN + jnp.arange(1))

    n_small = n_dec + N_META
    xs = jnp.concatenate([x_sample[:, 0, :], meta_tokens.astype(F32)], axis=0)
    zero_init = (jnp.zeros((DN_HEADS, DN_DK, DN_DK), F32), jnp.zeros((8, 3 * DN_W), F32),
                 jnp.zeros((1, LRU_W), F32), jnp.zeros((8, LRU_W), F32), jnp.zeros((RET_W, RET_W), F32))
    new_s, meta_state, ffn_w = [], [], []
    sdn_all = sret_all = None
    sret_t = jnp.transpose(state_ret, (0, 2, 3, 4, 1))
    for l in range(depth):
        last = l == depth - 1
        xs, *f1 = _ffn_cast(xs, norm_ffn1[l], w_ffn1_in, w_ffn1_out, l, tf=FFN_DFF_SLICE)
        omix_part, retin, sdn_all, dnc, hl, lrc = _mix_step(
            xs[:n_dec], lws[l], cos_s, sin_s, l, state_dn, jnp.swapaxes(state_dn_conv[l], 0, 1),
            state_lru, jnp.swapaxes(state_lru_conv[l], 0, 1), sdn_all, bb=DECODE_BATCH_TILE)
        ys, sret_all = _ret_step(xs[:n_dec], omix_part, retin, lws[l]["w_out"], sret_t, l, sret_all,
                                 n=n_dec)
        ym, m_sdn, m_dnt, m_h, m_lrt, m_sretbd, _ = _mix_pipe(
            xs[n_dec:], lws[l], cos_m, sin_m, zero_init, batch=1, seq=N_META, tc=N_META, ck=N_META)
        new_s.append((jnp.swapaxes(dnc, 0, 1), hl, jnp.swapaxes(lrc, 0, 1)))
        meta_state.append((m_sdn[0], m_dnt[0], m_h[0], m_lrt[0], m_sretbd[0]))
        xs = jnp.concatenate([ys, ym], axis=0)
        xs, *f2 = _ffn_cast(xs, norm_ffn2[l], w_ffn2_in, w_ffn2_out, l, norm_final if last else None,
                            tf=FFN_DFF_SLICE)
        ffn_w.append((f1, f2))
    y_sample = xs[:n_dec].reshape(n_dec, 1, D_MODEL)

    xp = x_prompt.reshape(batch * seq, D_MODEL)
    new_p = []
    for l in range(depth):
        last = l == depth - 1
        xp = _ffn(xp, norm_ffn1[l], *ffn_w[l][0], tm=FFN_TOKEN_TILE, nf=D_FF // FFN_DFF_SLICE)
        xp, sdn, dnt, hl, lrt, _, sret = _mix_pipe(
            xp, lws[l], cos_p, sin_p, meta_state[l], batch=batch, seq=seq, tc=MIX_TIME_BLOCK, ck=CHUNK)
        new_p.append((sdn, dnt[:, 5:8], hl[:, 0], lrt[:, 5:8], sret))
        xp = _ffn(xp, norm_ffn2[l], *ffn_w[l][1], norm_final if last else None, tm=FFN_TOKEN_TILE,
                  nf=D_FF // FFN_DFF_SLICE)
    y_prompt = xp.reshape(batch, seq, D_MODEL)

    outs_p = [jnp.stack([s[j] for s in new_p]) for j in range(5)]
    dnc_s, lru_s, lrc_s = [jnp.stack([s[j] for s in new_s]) for j in range(3)]
    sret_s = jnp.transpose(sret_all, (0, 4, 1, 2, 3))
    return (y_prompt, y_sample, *outs_p, sdn_all, dnc_s, lru_s, lrc_s, sret_s)
```

```python
import functools
import math

import jax
import jax.numpy as jnp
from jax import lax
from jax.experimental import pallas as pl
from jax.experimental.pallas import tpu as pltpu

F32 = jnp.float32
BF16 = jnp.bfloat16

D_MODEL = 1024
N_META = 16
PAST_LEN = 16384
DN_HEADS = 4
DN_DK = 128
DN_W = 512
LRU_W = 256
LRU_BLOCKS = 4
LRU_C = 8.0
RET_HEADS = 4
RET_DK = 64
RET_W = 256
CONV_W = 4
CHUNK = 64
D_FF = 2816
ROPE_BASE = 10000.0
EPS = 1e-6
SUB = 16
LOG_GAMMA = tuple(math.log1p(-2.0 ** (-5.0 - h)) for h in range(RET_HEADS))

V7X_VMEM_BYTES = 64 * 1024 * 1024
VMEM_LIMIT_BYTES = V7X_VMEM_BYTES * 7 // 8
V7X_MXU_WIDTH = 256

FFN_TOKEN_TILE = 1024
FFN_DFF_SLICE = V7X_MXU_WIDTH
MIX_TIME_BLOCK = 4 * CHUNK
DECODE_BATCH_TILE = 8


def _mm(a, b):
    return jnp.dot(a.astype(BF16), b.astype(BF16), preferred_element_type=F32)


def _mm_nt(a, b):
    return lax.dot_general(a.astype(BF16), b.astype(BF16), (((1,), (1,)), ((), ())),
                           preferred_element_type=F32)


def _mm_tn(a, b):
    return lax.dot_general(a.astype(BF16), b.astype(BF16), (((0,), (0,)), ((), ())),
                           preferred_element_type=F32)


def _split3(x):
    hi = x.astype(BF16)
    r = x - hi.astype(F32)
    mid = r.astype(BF16)
    lo = (r - mid.astype(F32)).astype(BF16)
    return hi, mid, lo


def _mm_01_right(x, m01):
    m = m01.astype(BF16)
    return sum(jnp.dot(p, m, preferred_element_type=F32) for p in _split3(x))


def _mm_split(x, m_bf16):
    hi = x.astype(BF16)
    lo = (x - hi.astype(F32)).astype(BF16)
    return (jnp.dot(hi, m_bf16, preferred_element_type=F32)
            + jnp.dot(lo, m_bf16, preferred_element_type=F32))


def _rmsnorm(x, w):
    return x * lax.rsqrt(jnp.mean(x * x, axis=-1, keepdims=True) + EPS) * w


def _sigmoid(x):
    return 0.5 + 0.5 * jnp.tanh(0.5 * x)


def _silu(x):
    return x * _sigmoid(x)


def _causal_conv(x, tail, w):
    tc, n = x.shape
    nt8 = tc // 8
    first = _iota((nt8, 8, n), 1) == 0
    xm1, xm2, xm3 = tail[7:8], tail[6:7], tail[5:6]

    def shift1(s, carry):
        r = pltpu.roll(s.reshape(nt8, 8, n), 1, 1)
        before = jnp.concatenate([jnp.broadcast_to(carry, (1, 8, n)), r[:-1]], axis=0)
        return jnp.where(first, before, r).reshape(tc, n)

    s = shift1(x * w[0:1], xm1 * w[0:1])
    s = shift1(s + x * w[1:2], xm2 * w[0:1] + xm1 * w[1:2])
    s = shift1(s + x * w[2:3], xm3 * w[0:1] + xm2 * w[1:2] + xm1 * w[2:3])
    return s + x * w[3:4]


def _softplus(x):
    return jnp.maximum(x, 0.0) + jnp.log1p(jnp.exp(-jnp.abs(x)))


def _iota(shape, dim):
    return lax.broadcasted_iota(jnp.int32, shape, dim)


def _lane_log_gamma(shape, dim, width):
    head = _iota(shape, dim) // width
    out = jnp.full(shape, LOG_GAMMA[0], F32)
    for h in range(1, RET_HEADS):
        out = jnp.where(head == h, LOG_GAMMA[h], out)
    return out


def _rope(x, cos, sin_signed):
    n = x.shape[1]
    half = RET_DK // 2
    first = (_iota(x.shape, 1) % RET_DK) < half
    swapped = jnp.where(first, pltpu.roll(x, n - half, 1), pltpu.roll(x, half, 1))
    return x * cos + swapped * sin_signed


def _ret_norm_gate(o, gate, avg_bf16):
    mu = _mm_split(o, avg_bf16)
    d = o - mu
    var = _mm_split(d * d, avg_bf16)
    return d * lax.rsqrt(var + EPS) * _silu(gate)


def _head_avg_matrix():
    r = _iota((RET_W, RET_W), 0) // RET_DK
    c = _iota((RET_W, RET_W), 1) // RET_DK
    return jnp.where(r == c, 1.0 / RET_DK, 0.0).astype(BF16)


def _lru_coeffs(xl, wgate, bgate, lam):
    gates = _mm(xl, wgate) + bgate
    r = _sigmoid(gates[:, :LRU_W])
    i = _sigmoid(gates[:, LRU_W:])
    log_a = -LRU_C * r * _softplus(-lam)
    a = jnp.exp(log_a)
    b = jnp.sqrt(jnp.maximum(-jnp.tanh(log_a) * (a * a + 1.0), 0.0)) * (i * xl)
    return a, b


def _ffn_body(*refs, final, nf):
    if final:
        x_ref, nw_ref, wg_ref, wu_ref, wo_ref, fw_ref, o_ref = refs
    else:
        x_ref, nw_ref, wg_ref, wu_ref, wo_ref, o_ref = refs
    tf = D_FF // nf
    x = x_ref[...]
    u = _rmsnorm(x, nw_ref[...]).astype(BF16)
    acc = None
    for j in range(nf):
        gate = jnp.dot(u, wg_ref[:, j * tf:(j + 1) * tf], preferred_element_type=F32)
        up = jnp.dot(u, wu_ref[:, j * tf:(j + 1) * tf], preferred_element_type=F32)
        h = (_silu(gate) * up).astype(BF16)
        part = jnp.dot(h, wo_ref[j * tf:(j + 1) * tf, :], preferred_element_type=F32)
        acc = part if acc is None else acc + part
    y = x + 0.5 * acc
    if final:
        y = _rmsnorm(y, fw_ref[...])
    o_ref[...] = y


def _ffn(x, norm_w, wg, wu, wo, final_w=None, *, tm, nf):
    n = x.shape[0]
    final = final_w is not None
    resident = pl.Buffered(1)
    in_specs = [
        pl.BlockSpec((tm, D_MODEL), lambda i: (i, 0)),
        pl.BlockSpec((1, D_MODEL), lambda i: (0, 0)),
        pl.BlockSpec((D_MODEL, D_FF), lambda i: (0, 0), pipeline_mode=resident),
        pl.BlockSpec((D_MODEL, D_FF), lambda i: (0, 0), pipeline_mode=resident),
        pl.BlockSpec((D_FF, D_MODEL), lambda i: (0, 0), pipeline_mode=resident),
    ]
    args = [x, norm_w.reshape(1, D_MODEL), wg, wu, wo]
    if final:
        in_specs.append(pl.BlockSpec((1, D_MODEL), lambda i: (0, 0)))
        args.append(final_w.reshape(1, D_MODEL))
    return pl.pallas_call(
        functools.partial(_ffn_body, final=final, nf=nf),
        grid=(n // tm,),
        in_specs=in_specs,
        out_specs=pl.BlockSpec((tm, D_MODEL), lambda i: (i, 0)),
        out_shape=jax.ShapeDtypeStruct((n, D_MODEL), F32),
        compiler_params=pltpu.CompilerParams(
            dimension_semantics=("arbitrary",), vmem_limit_bytes=VMEM_LIMIT_BYTES),
        name="ffn_final" if final else "ffn",
    )(*args)


def _ffn_cast_body(*refs, final):
    if final:
        x_ref, nw_ref, wg_ref, wu_ref, wo_ref, fw_ref, o_ref, wg_o, wu_o, wo_o, u_s, acc_s = refs
    else:
        x_ref, nw_ref, wg_ref, wu_ref, wo_ref, o_ref, wg_o, wu_o, wo_o, u_s, acc_s = refs
    j = pl.program_id(0)

    @pl.when(j == 0)
    def _():
        u_s[...] = _rmsnorm(x_ref[...], nw_ref[...]).astype(BF16)
        acc_s[...] = jnp.zeros_like(acc_s)

    wg = wg_ref[...].astype(BF16)
    wu = wu_ref[...].astype(BF16)
    wo = wo_ref[...].astype(BF16)
    wg_o[...] = wg
    wu_o[...] = wu
    wo_o[...] = wo
    u = u_s[...]
    gate = jnp.dot(u, wg, preferred_element_type=F32)
    up = jnp.dot(u, wu, preferred_element_type=F32)
    acc_s[...] += jnp.dot((_silu(gate) * up).astype(BF16), wo, preferred_element_type=F32)

    @pl.when(j == pl.num_programs(0) - 1)
    def _():
        y = x_ref[...] + 0.5 * acc_s[...]
        if final:
            y = _rmsnorm(y, fw_ref[...])
        o_ref[...] = y


def _ffn_cast(x, norm_w, w_in, w_out, layer, final_w=None, *, tf):
    n = x.shape[0]
    nf = D_FF // tf
    final = final_w is not None
    in_specs = [
        pl.BlockSpec((n, D_MODEL), lambda j: (0, 0)),
        pl.BlockSpec((1, D_MODEL), lambda j: (0, 0)),
        pl.BlockSpec((None, D_MODEL, tf), lambda j: (layer, 0, j)),
        pl.BlockSpec((None, D_MODEL, tf), lambda j: (layer, 0, j + nf)),
        pl.BlockSpec((None, tf, D_MODEL), lambda j: (layer, j, 0)),
    ]
    args = [x, norm_w.reshape(1, D_MODEL), w_in, w_in, w_out]
    if final:
        in_specs.append(pl.BlockSpec((1, D_MODEL), lambda j: (0, 0)))
        args.append(final_w.reshape(1, D_MODEL))
    return pl.pallas_call(
        functools.partial(_ffn_cast_body, final=final),
        grid=(nf,),
        in_specs=in_specs,
        out_specs=[pl.BlockSpec((n, D_MODEL), lambda j: (0, 0)),
                   pl.BlockSpec((D_MODEL, tf), lambda j: (0, j)),
                   pl.BlockSpec((D_MODEL, tf), lambda j: (0, j)),
                   pl.BlockSpec((tf, D_MODEL), lambda j: (j, 0))],
        out_shape=[jax.ShapeDtypeStruct((n, D_MODEL), F32),
                   jax.ShapeDtypeStruct((D_MODEL, D_FF), BF16),
                   jax.ShapeDtypeStruct((D_MODEL, D_FF), BF16),
                   jax.ShapeDtypeStruct((D_FF, D_MODEL), BF16)],
        scratch_shapes=[pltpu.VMEM((n, D_MODEL), BF16), pltpu.VMEM((n, D_MODEL), F32)],
        compiler_params=pltpu.CompilerParams(
            dimension_semantics=("arbitrary",), vmem_limit_bytes=VMEM_LIMIT_BYTES),
        name="ffn_cast_final" if final else "ffn_cast",
    )(*args)


def _heads_per_group(ck):
    return max(1, min(DN_HEADS, 128 // ck))


def _tri_inv_steps(a, n_sub, out):
    m = a[0].shape[0]
    r = _iota((m, m), 0)
    c = _iota((m, m), 1)
    eye = jnp.where(r == c, 1.0, 0.0).astype(F32)
    diag = (r // SUB) == (c // SUB)
    x = [jnp.where(diag, ai, 0.0) for ai in a]
    off = [ai - xi for ai, xi in zip(a, x)]
    p = [eye - xi for xi in x]
    for _ in range(3):
        x = [_mm(xi, xi) for xi in x]
        yield
        p = [pi + _mm(pi, xi) for pi, xi in zip(p, x)]
        yield
    if n_sub == 1:
        out.extend(p)
        return
    n = [_mm(pi, oi) for pi, oi in zip(p, off)]
    yield
    n2 = [_mm(ni, ni) for ni in n]
    yield
    rr = [eye - ni + n2i - _mm(ni, n2i) for ni, n2i in zip(n, n2)]
    yield
    out.extend(_mm(ri, pi) for ri, pi in zip(rr, p))
    yield


def _mix_pipe_body(x_ref, nw_ref, wqkv_ref, wabt_ref, wrest_ref, dncw_ref,
                   alog_c_ref, dtb_c_ref, dnnw_ref,
                   lcw_ref, lcb_ref, wgate_ref, bgate_ref, lam_ref, cos_ref, sin_ref, wout_ref,
                   sdn0_ref, dnt0_ref, h0_ref, lrt0_ref, sret0_ref,
                   y_ref, sdn_o_ref, dnt_o_ref, h_o_ref, lrt_o_ref, sretbd_o_ref, sret_o_ref,
                   sdn, sret, hst, dn_tail, lru_tail, q_s, k_s, v_s, gb_s, grow_s, odn_s, oret_s, omix_s,
                   h_uv, h_wq, h_pm, h_kd, h_egl, h_rq, h_oi, h_su, h_z, h_rg, h_x, h_lru,
                   *, tc, ck, nt, n_blocks):
    i = pl.program_id(0)
    t_a = i % nt
    t_b = (i + nt - 1) % nt
    nc = tc // ck
    hc = DN_HEADS * ck
    n_sub = ck // SUB
    chunks = range(nc)
    rows = [slice(c * ck, (c + 1) * ck) for c in chunks]
    heads = [(h * ck, (h + 1) * ck) for h in range(DN_HEADS)]
    hg = _heads_per_group(ck)
    ng = DN_HEADS // hg
    gc = hg * ck
    probs = [(c, g) for c in chunks for g in range(ng)]

    @pl.when(i == 0)
    def _():
        for ref in (h_uv, h_wq, h_pm, h_kd, h_egl, h_rq, h_oi, h_su, h_z, h_rg, h_x, h_lru):
            ref[...] = jnp.zeros_like(ref)

    @pl.when(t_a == 0)
    def _():
        hst[...] = h0_ref[...]
        dn_tail[...] = dnt0_ref[...]
        lru_tail[...] = lrt0_ref[...]

    @pl.when((t_b == 0) | (i == 0))
    def _():
        sdn[...] = sdn0_ref[...]
        sret[...] = sret0_ref[...]

    sr = _iota((gc, gc), 0)
    sc = _iota((gc, gc), 1)
    strict = ((sr // ck) == (sc // ck)) & (sr > sc)
    eye_g = jnp.where(sr == sc, 1.0, 0.0).astype(F32)
    lg_rows = _lane_log_gamma((hc, ck), 0, ck)
    tpos = _iota((hc, ck), 0) % ck
    ipos = _iota((hc, ck), 1)
    causal = tpos >= ipos
    ret_dec = jnp.where(causal, jnp.exp(jnp.where(causal, (tpos - ipos).astype(F32) * lg_rows, 0.0)), 0.0)
    head_sel = (_iota((hc, RET_W), 0) // ck) == (_iota((hc, RET_W), 1) // RET_DK)
    lg_lane = _lane_log_gamma((ck, RET_W), 1, RET_DK)
    tl = _iota((ck, RET_W), 0).astype(F32)
    ret_eg = jnp.exp((tl + 1.0) * lg_lane)
    ret_kdec = jnp.exp((ck - 1.0 - tl) * lg_lane)
    ret_gc = jnp.exp(ck * _lane_log_gamma((1, RET_W), 1, RET_DK))
    bd = (_iota((RET_W, RET_W), 0) // RET_DK) == (_iota((RET_W, RET_W), 1) // RET_DK)

    a = {}
    b = {}

    def a_norm():
        a["x"] = x_ref[...]
        a["u"] = _rmsnorm(a["x"], nw_ref[...]).astype(BF16)

    def a_qkv_piece(p):
        cols = slice(2 * DN_DK * p, 2 * DN_DK * (p + 1))
        a["pre", p] = jnp.dot(a["u"], wqkv_ref[:, cols], preferred_element_type=F32)

    def a_conv_piece(p):
        cols = slice(2 * DN_DK * p, 2 * DN_DK * (p + 1))
        pre = a.pop(("pre", p))
        conv = _causal_conv(pre, dn_tail[:, cols], dncw_ref[:, cols])
        dn_tail[:, cols] = pre[tc - 8:tc]
        act = _silu(conv)
        kind, first_head = divmod(2 * p, DN_HEADS)
        for j in range(2):
            v = act[:, j * DN_DK:(j + 1) * DN_DK]
            if kind == 0:
                q_s[first_head + j] = (v * lax.rsqrt(jnp.sum(v * v, axis=-1, keepdims=True) + EPS)
                                       * (DN_DK ** -0.5))
            elif kind == 1:
                k_s[first_head + j] = v * lax.rsqrt(jnp.sum(v * v, axis=-1, keepdims=True) + EPS)
            else:
                v_s[first_head + j] = v

    def proj(lo, hi):
        return jnp.dot(a["u"], wrest_ref[:, lo:hi], preferred_element_type=F32)

    def a_lru_in():
        lru = proj(512, 1024)
        xl = _causal_conv(lru[:, :LRU_W], lru_tail[...], lcw_ref[...])
        lru_tail[...] = lru[tc - 8:tc, :LRU_W]
        a["xl"] = xl + lcb_ref[...]
        a["lru_y"] = lru[:, LRU_W:]

    def a_ret_q():
        a["ret_q"] = proj(1024, 1280)

    def a_ret_kv():
        kv = proj(1280, 1792)
        cos = cos_ref[...]
        sin = sin_ref[...]
        a["rq"] = _rope(a["ret_q"], cos, sin)
        a["rk"] = _rope(kv[:, :RET_W], cos, sin) * (RET_DK ** -0.5)
        a["rv"] = kv[:, RET_W:]

    def a_z():
        a["dn_z"] = proj(0, 512)

    def a_g():
        a["ret_g"] = proj(1792, 2048)

    def a_gates():
        abt = lax.dot_general(wabt_ref[...], a["u"], (((1,), (1,)), ((), ())), preferred_element_type=F32)
        g_row = -jnp.exp(alog_c_ref[...]) * _softplus(abt + dtb_c_ref[...])
        rr = _iota((tc, tc), 0)
        cc = _iota((tc, tc), 1)
        upper = jnp.where(((rr // ck) == (cc // ck)) & (rr <= cc), 1.0, 0.0)
        big_g_row = _mm_01_right(g_row, upper)
        is_g = _iota((8, tc), 0) < DN_HEADS
        gb_s[...] = jnp.where(is_g, big_g_row, _sigmoid(abt)).T
        for c in chunks:
            grow_s[c] = jnp.concatenate(
                [big_g_row[h:h + 1, c * ck:(c + 1) * ck] for h in range(DN_HEADS)], axis=1)

    def a_lru_scan_steps():
        ga, gb = _lru_coeffs(a["xl"], wgate_ref[...], bgate_ref[...], lam_ref[...])
        nt8 = tc // 8
        ga = ga.reshape(nt8, 8, LRU_W)
        gb = gb.reshape(nt8, 8, LRU_W)
        row = _iota((nt8, 8, LRU_W), 1)
        yield
        for s in (1, 2, 4):
            keep = row >= s
            gb = ga * jnp.where(keep, pltpu.roll(gb, s, 1), 0.0) + gb
            ga = ga * jnp.where(keep, pltpu.roll(ga, s, 1), 1.0)
            yield
        carry = hst[...]
        h_in = []
        for t8 in range(nt8):
            h_in.append(carry)
            carry = ga[t8, 7:8] * carry + gb[t8, 7:8]
        hst[...] = carry
        yield
        h = gb + ga * jnp.stack(h_in, axis=0)
        a["o_lru"] = (h.reshape(tc, LRU_W) * jax.nn.gelu(a["lru_y"])).astype(BF16)
        yield

    def a_kkqk():
        def stack(ref, c, g):
            return jnp.concatenate([ref[h, rows[c], :] for h in range(g * hg, (g + 1) * hg)], axis=0)

        def stack_col(c, g, first):
            return jnp.concatenate([gb_s[rows[c], first + h:first + h + 1]
                                    for h in range(g * hg, (g + 1) * hg)], axis=0)

        a["kst"] = [stack(k_s, c, g) for c, g in probs]
        a["qst"] = [stack(q_s, c, g) for c, g in probs]
        a["vst"] = [stack(v_s, c, g) for c, g in probs]
        gst = [stack_col(c, g, 0) for c, g in probs]
        a["gst"] = gst
        a["bst"] = [stack_col(c, g, DN_HEADS) for c, g in probs]
        dstrict = []
        for k, (c, g) in enumerate(probs):
            diff = gst[k] - grow_s[c][:, g * gc:(g + 1) * gc]
            dstrict.append(jnp.where(strict, jnp.exp(jnp.where(strict, diff, 0.0)), 0.0))
        a["eg"] = [jnp.exp(g) for g in gst]
        a["amat"] = [a["bst"][k] * _mm_nt(a["kst"][k], a["kst"][k]) * dstrict[k] for k in range(len(probs))]
        a["pmat"] = [(_mm_nt(a["qst"][k], a["kst"][k]) * (dstrict[k] + eye_g)).astype(BF16)
                     for k in range(len(probs))]

    tinv = []

    def a_sol():
        kst, qst, vst, gst, bst, eg = (a[k] for k in ("kst", "qst", "vst", "gst", "bst", "eg"))
        sol = [_mm(tinv[k], jnp.concatenate([bst[k] * vst[k], (bst[k] * eg[k]) * kst[k]], axis=1))
               for k in range(len(probs))]
        a["uv"] = [s[:, :DN_DK] for s in sol]
        wq, egl, kd = {}, {}, {}
        for k, (c, g) in enumerate(probs):
            for j in range(hg):
                lo, hi = j * ck, (j + 1) * ck
                key = (c, g * hg + j)
                wq[key] = jnp.concatenate([sol[k][lo:hi, DN_DK:], eg[k][lo:hi] * qst[k][lo:hi]],
                                          axis=0).astype(BF16)
                g_last = gst[k][hi - 1:hi]
                egl[key] = jnp.exp(g_last)
                kd[key] = (kst[k][lo:hi] * jnp.exp(g_last - gst[k][lo:hi])).astype(BF16)
        a["wq"], a["egl"], a["kd"] = wq, egl, kd

    def a_ret_scores():
        qc = [a["rq"][rs] for rs in rows]
        kc = [a["rk"][rs] for rs in rows]
        qp = [jnp.where(head_sel, jnp.concatenate([q] * RET_HEADS, axis=0), 0.0) for q in qc]
        a["qc"] = qc
        a["sc"] = [(_mm_nt(qp[c], kc[c]) * ret_dec).astype(BF16) for c in chunks]

    def a_ret_intra():
        vc = [a["rv"][rs].astype(BF16) for rs in rows]
        o_intra = []
        for c in chunks:
            op = jnp.where(head_sel, jnp.dot(a["sc"][c], vc[c], preferred_element_type=F32), 0.0)
            acc = op[0:ck]
            for h in range(1, RET_HEADS):
                acc = acc + op[h * ck:(h + 1) * ck]
            o_intra.append(acc)
        a["o_intra"] = o_intra
        a["s_upd"] = [jnp.where(bd, _mm_tn(a["rk"][rows[c]] * ret_kdec, vc[c]), 0.0) for c in chunks]

    def a_handoff():
        for k, (c, g) in enumerate(probs):
            h_uv[c, g * gc:(g + 1) * gc, :] = a["uv"][k]
            h_pm[c, g] = a["pmat"][k]
        for c in chunks:
            h_rq[c] = a["qc"][c].astype(BF16)
            h_oi[c] = a["o_intra"][c]
            h_su[c] = a["s_upd"][c]
            for h in range(DN_HEADS):
                h_wq[c, h] = a["wq"][c, h]
                h_kd[c, h] = a["kd"][c, h]
                h_egl[c, h:h + 1, :] = jnp.broadcast_to(a["egl"][c, h], (1, DN_DK))
        h_z[...] = a["dn_z"]
        h_rg[...] = a["ret_g"]
        h_x[...] = a["x"]
        h_lru[...] = a["o_lru"]

    def b_post_a(c):
        ws_qs = [_mm(h_wq[c, h], sdn[h]) for h in range(DN_HEADS)]
        b["u"] = [h_uv[c, lo:hi, :] - ws_qs[h][:ck] for h, (lo, hi) in enumerate(heads)]
        b["qs"] = [w[ck:] for w in ws_qs]

    def b_post_b(c):
        for g in range(ng):
            group = range(g * hg, (g + 1) * hg)
            ost = (jnp.concatenate([b["qs"][h] for h in group], axis=0)
                   + _mm(h_pm[c, g], jnp.concatenate([b["u"][h] for h in group], axis=0)))
            for j, h in enumerate(group):
                odn_s[rows[c], h * DN_DK:(h + 1) * DN_DK] = ost[j * ck:(j + 1) * ck]
        for h in range(DN_HEADS):
            sdn[h] = h_egl[c, h:h + 1, :] * sdn[h] + _mm_tn(h_kd[c, h], b["u"][h])

    def b_ret():
        for c in chunks:
            s0 = sret[...]
            oret_s[rows[c], :] = ret_eg * _mm(h_rq[c], s0) + h_oi[c]
            sret[...] = ret_gc * s0 + h_su[c]

    def b_out():
        nw = dnnw_ref[...]
        for h in range(DN_HEADS):
            sl = slice(h * DN_DK, (h + 1) * DN_DK)
            oh = odn_s[:, sl]
            oh = oh * lax.rsqrt(jnp.mean(oh * oh, axis=-1, keepdims=True) + EPS) * nw
            omix_s[:, sl] = (oh * _silu(h_z[:, sl])).astype(BF16)
        omix_s[:, 512:768] = h_lru[...]
        omix_s[:, 768:1024] = _ret_norm_gate(oret_s[...], h_rg[...], _head_avg_matrix()).astype(BF16)
        y_ref[...] = h_x[...] + jnp.dot(omix_s[...], wout_ref[...], preferred_element_type=F32)

    n_piece = 3 * DN_W // (2 * DN_DK)
    a_seq = [a_norm]
    for p in range(n_piece):
        a_seq.append(lambda p=p: (a_qkv_piece(p), a_conv_piece(p)))
    a_seq += [a_lru_in, lambda: (a_gates(), a_ret_q()), a_ret_kv, a_z, a_g, a_kkqk]
    b_seq = []
    for c in chunks:
        b_seq += [functools.partial(b_post_a, c), functools.partial(b_post_b, c)]
    b_seq += [b_ret, b_out]
    b_seq.pop(0)()
    for fa in a_seq:
        fa()
        if b_seq:
            b_seq.pop(0)()
    for fb in b_seq:
        fb()
    scan = a_lru_scan_steps()
    for _ in _tri_inv_steps(a["amat"], n_sub, tinv):
        next(scan, None)
    for _ in scan:
        pass
    a_sol()
    a_ret_scores()
    a_ret_intra()
    a_handoff()

    @pl.when((t_a == nt - 1) & (i < n_blocks))
    def _():
        dnt_o_ref[0] = dn_tail[...]
        h_o_ref[0] = hst[...]
        lrt_o_ref[0] = lru_tail[...]

    @pl.when((t_b == nt - 1) & (i > 0))
    def _():
        sdn_o_ref[0] = sdn[...]
        s_bd = sret[...]
        sretbd_o_ref[0] = s_bd
        for h in range(RET_HEADS):
            sret_o_ref[0, h] = s_bd[h * RET_DK:(h + 1) * RET_DK, h * RET_DK:(h + 1) * RET_DK]


def _mix_pipe(x, lw, cos, sin, init, *, batch, seq, tc, ck):
    nt = seq // tc
    nc = tc // ck
    hc = DN_HEADS * ck
    hg = _heads_per_group(ck)
    ng, gc = DN_HEADS // hg, hg * ck
    n_blocks = batch * nt
    blk_a = lambda i: jnp.minimum(i, n_blocks - 1)
    blk_b = lambda i: jnp.maximum(i - 1, 0)
    const = lambda shape: pl.BlockSpec(shape, lambda i: (0,) * len(shape))
    in_specs = [
        pl.BlockSpec((tc, D_MODEL), lambda i: (blk_a(i), 0)),
        const((1, D_MODEL)), const((D_MODEL, 3 * DN_W)), const((8, D_MODEL)),
        const((D_MODEL, 2048)), const((CONV_W, 3 * DN_W)),
        const((8, 1)), const((8, 1)), const((1, DN_DK)),
        const((CONV_W, LRU_W)), const((1, LRU_W)), const((LRU_W, 2 * LRU_W)), const((1, 2 * LRU_W)),
        const((1, LRU_W)),
        pl.BlockSpec((tc, RET_W), lambda i: (blk_a(i) % nt, 0)),
        pl.BlockSpec((tc, RET_W), lambda i: (blk_a(i) % nt, 0)),
        const((D_MODEL, D_MODEL)),
        const((DN_HEADS, DN_DK, DN_DK)), const((8, 3 * DN_W)), const((1, LRU_W)), const((8, LRU_W)),
        const((RET_W, RET_W)),
    ]
    out_shape = [
        jax.ShapeDtypeStruct((batch * seq, D_MODEL), F32),
        jax.ShapeDtypeStruct((batch, DN_HEADS, DN_DK, DN_DK), F32),
        jax.ShapeDtypeStruct((batch, 8, 3 * DN_W), F32),
        jax.ShapeDtypeStruct((batch, 1, LRU_W), F32),
        jax.ShapeDtypeStruct((batch, 8, LRU_W), F32),
        jax.ShapeDtypeStruct((batch, RET_W, RET_W), F32),
        jax.ShapeDtypeStruct((batch, RET_HEADS, RET_DK, RET_DK), F32),
    ]
    seq_a = lambda i: blk_a(i) // nt
    seq_b = lambda i: blk_b(i) // nt
    out_specs = [
        pl.BlockSpec((tc, D_MODEL), lambda i: (blk_b(i), 0)),
        pl.BlockSpec((1, DN_HEADS, DN_DK, DN_DK), lambda i: (seq_b(i), 0, 0, 0)),
        pl.BlockSpec((1, 8, 3 * DN_W), lambda i: (seq_a(i), 0, 0)),
        pl.BlockSpec((1, 1, LRU_W), lambda i: (seq_a(i), 0, 0)),
        pl.BlockSpec((1, 8, LRU_W), lambda i: (seq_a(i), 0, 0)),
        pl.BlockSpec((1, RET_W, RET_W), lambda i: (seq_b(i), 0, 0)),
        pl.BlockSpec((1, RET_HEADS, RET_DK, RET_DK), lambda i: (seq_b(i), 0, 0, 0)),
    ]
    scratch = [
        pltpu.VMEM((DN_HEADS, DN_DK, DN_DK), F32),
        pltpu.VMEM((RET_W, RET_W), F32),
        pltpu.VMEM((1, LRU_W), F32),
        pltpu.VMEM((8, 3 * DN_W), F32),
        pltpu.VMEM((8, LRU_W), F32),
        pltpu.VMEM((DN_HEADS, tc, DN_DK), F32),
        pltpu.VMEM((DN_HEADS, tc, DN_DK), F32),
        pltpu.VMEM((DN_HEADS, tc, DN_DK), F32),
        pltpu.VMEM((tc, 8), F32),
        pltpu.VMEM((nc, 1, hc), F32),
        pltpu.VMEM((tc, DN_W), F32),
        pltpu.VMEM((tc, RET_W), F32),
        pltpu.VMEM((tc, D_MODEL), BF16),
        pltpu.VMEM((nc, hc, DN_DK), F32),
        pltpu.VMEM((nc, DN_HEADS, 2 * ck, DN_DK), BF16),
        pltpu.VMEM((nc, ng, gc, gc), BF16),
        pltpu.VMEM((nc, DN_HEADS, ck, DN_DK), BF16),
        pltpu.VMEM((nc, 8, DN_DK), F32),
        pltpu.VMEM((nc, ck, RET_W), BF16),
        pltpu.VMEM((nc, ck, RET_W), F32),
        pltpu.VMEM((nc, RET_W, RET_W), F32),
        pltpu.VMEM((tc, DN_W), F32),
        pltpu.VMEM((tc, RET_W), F32),
        pltpu.VMEM((tc, D_MODEL), F32),
        pltpu.VMEM((tc, LRU_W), BF16),
    ]
    args = [x, lw["norm_mix"], lw["wqkv"], lw["wabt"], lw["wrest"], lw["dn_conv_w"],
            lw["alog_c"], lw["dtb_c"], lw["dn_norm_w"],
            lw["lru_conv_w"], lw["lru_conv_b"], lw["wgate"], lw["bgate"], lw["lam"], cos, sin,
            lw["w_out"], *init]
    return pl.pallas_call(
        functools.partial(_mix_pipe_body, tc=tc, ck=ck, nt=nt, n_blocks=n_blocks),
        grid=(n_blocks + 1,),
        in_specs=in_specs,
        out_specs=out_specs,
        out_shape=out_shape,
        scratch_shapes=scratch,
        compiler_params=pltpu.CompilerParams(
            dimension_semantics=("arbitrary",), vmem_limit_bytes=VMEM_LIMIT_BYTES),
        name=f"mix_pipe_c{ck}",
    )(*args)


def _mix_step_body(*refs, bb, layer):
    (x_ref, nw_ref, wqkv_ref, wab_ref, wrest_ref, dncw_ref, alog_r_ref, dtb_r_ref,
     dnnw_ref, lcw_ref, lcb_ref, wgate_ref, bgate_ref, lam_ref, cos_ref, sin_ref,
     sdn_ref, dnc_ref, h_ref, lrc_ref) = refs[:20]
    refs = refs[20:]
    if layer > 0:
        prev_sdn_ref = refs[0]
        refs = refs[1:]
    (omix_o_ref, retin_o_ref, sdn_o_ref, dnc_o_ref, h_o_ref, lrc_o_ref,
     q_s, k_s, v_s, eg_s, beta_s, odn_s, dnz_s) = refs
    i = pl.program_id(0)
    if layer > 0:
        sdn_o_ref[0:layer] = prev_sdn_ref[...]

    @pl.when(i == 0)
    def _():
        x = x_ref[...]
        u = _rmsnorm(x, nw_ref[...]).astype(BF16)

        qkv_pre = jnp.dot(u, wqkv_ref[...], preferred_element_type=F32)
        cw = dncw_ref[...]
        conv = (dnc_ref[0] * cw[0:1, :] + dnc_ref[1] * cw[1:2, :] + dnc_ref[2] * cw[2:3, :]
                + qkv_pre * cw[3:4, :])
        dnc_o_ref[0] = dnc_ref[1]
        dnc_o_ref[1] = dnc_ref[2]
        dnc_o_ref[2] = qkv_pre
        qkv = _silu(conv)
        for h in range(DN_HEADS):
            sl = slice(h * DN_DK, (h + 1) * DN_DK)
            qh = qkv[:, sl]
            kh = qkv[:, DN_W + h * DN_DK:DN_W + (h + 1) * DN_DK]
            q_s[:, sl] = qh * lax.rsqrt(jnp.sum(qh * qh, axis=-1, keepdims=True) + EPS) * (DN_DK ** -0.5)
            k_s[:, sl] = kh * lax.rsqrt(jnp.sum(kh * kh, axis=-1, keepdims=True) + EPS)
        v_s[...] = qkv[:, 2 * DN_W:]
        ab = jnp.dot(u, wab_ref[...], preferred_element_type=F32)
        eg_s[...] = jnp.exp(-jnp.exp(alog_r_ref[...]) * _softplus(ab[:, :128] + dtb_r_ref[...]))
        beta_s[...] = _sigmoid(ab[:, 128:])

        rest = jnp.dot(u, wrest_ref[...], preferred_element_type=F32)
        dnz_s[...] = rest[:, 0:512]
        lru_x = rest[:, 512:768]
        lru_y = rest[:, 768:1024]

        lw = lcw_ref[...]
        xl = (lrc_ref[0] * lw[0:1, :] + lrc_ref[1] * lw[1:2, :] + lrc_ref[2] * lw[2:3, :]
              + lru_x * lw[3:4, :] + lcb_ref[...])
        lrc_o_ref[0] = lrc_ref[1]
        lrc_o_ref[1] = lrc_ref[2]
        lrc_o_ref[2] = lru_x
        a, b = _lru_coeffs(xl, wgate_ref[...], bgate_ref[...], lam_ref[...])
        h_new = a * h_ref[...] + b
        h_o_ref[...] = h_new
        omix_o_ref[:, DN_W:DN_W + LRU_W] = (h_new * jax.nn.gelu(lru_y)).astype(BF16)

        cos = cos_ref[...]
        sin = sin_ref[...]
        retin_o_ref[:, 0:256] = _rope(rest[:, 1024:1280], cos, sin)
        retin_o_ref[:, 256:512] = _rope(rest[:, 1280:1536], cos, sin) * (RET_DK ** -0.5)
        retin_o_ref[:, 512:768] = rest[:, 1536:1792]
        retin_o_ref[:, 768:1024] = rest[:, 1792:2048]

    for t8 in range(bb // 8):
        r0 = pl.multiple_of(i * bb + t8 * 8, 8)
        eg_t = eg_s[pl.ds(r0, 8), :]
        beta_t = beta_s[pl.ds(r0, 8), :]
        for h in range(DN_HEADS):
            sl = slice(h * DN_DK, (h + 1) * DN_DK)
            k_t = k_s[pl.ds(r0, 8), sl]
            q_t = q_s[pl.ds(r0, 8), sl]
            kt = k_t.T
            v_t = v_s[pl.ds(r0, 8), sl]
            qk_t = jnp.sum(q_t * k_t, axis=-1, keepdims=True)
            first = _iota((8, DN_DK), 0) == 0
            rows = []
            for j in range(8):
                kcol = kt[:, j:j + 1]
                s0 = sdn_ref[t8 * 8 + j, h]
                eg = eg_t[j:j + 1, h:h + 1]
                prod = _mm(jnp.where(first, k_t[j:j + 1, :], q_t[j:j + 1, :]), s0)
                ks = prod[0:1]
                qs = prod[1:2]
                qk = qk_t[j:j + 1]
                uu = beta_t[j:j + 1, h:h + 1] * (v_t[j:j + 1, :] - eg * ks)
                rows.append(eg * qs + qk * uu)
                sdn_o_ref[layer, t8 * 8 + j, h] = eg * s0 + kcol * uu
            odn_s[pl.ds(r0, 8), sl] = jnp.concatenate(rows, axis=0)

    @pl.when(i == pl.num_programs(0) - 1)
    def _():
        nw = dnnw_ref[...]
        for h in range(DN_HEADS):
            sl = slice(h * DN_DK, (h + 1) * DN_DK)
            oh = odn_s[:, sl]
            oh = oh * lax.rsqrt(jnp.mean(oh * oh, axis=-1, keepdims=True) + EPS) * nw
            omix_o_ref[:, sl] = (oh * _silu(dnz_s[:, sl])).astype(BF16)


def _mix_step(x, lw, cos, sin, layer, sdn, dnc, hl, lrc, prev_sdn, *, bb):
    n = x.shape[0]
    const = lambda shape: pl.BlockSpec(shape, lambda i: (0,) * len(shape))
    in_specs = [
        const((n, D_MODEL)),
        const((1, D_MODEL)), const((D_MODEL, 3 * DN_W)), const((D_MODEL, 256)), const((D_MODEL, 2048)),
        const((CONV_W, 3 * DN_W)), const((1, 128)), const((1, 128)), const((1, DN_DK)),
        const((CONV_W, LRU_W)), const((1, LRU_W)), const((LRU_W, 2 * LRU_W)), const((1, 2 * LRU_W)),
        const((1, LRU_W)), const((1, RET_W)), const((1, RET_W)),
        pl.BlockSpec((None, bb, DN_HEADS, DN_DK, DN_DK), lambda i: (layer, i, 0, 0, 0)),
        const((CONV_W - 1, n, 3 * DN_W)),
        pl.BlockSpec((None, n, LRU_W), lambda i: (layer, 0, 0)),
        const((CONV_W - 1, n, LRU_W)),
    ]
    if layer > 0:
        in_specs.append(pl.BlockSpec((layer, bb, DN_HEADS, DN_DK, DN_DK), lambda i: (0, i, 0, 0, 0)))
    out_shape = [
        jax.ShapeDtypeStruct((n, DN_W + LRU_W), BF16),
        jax.ShapeDtypeStruct((n, 4 * RET_W), F32),
        jax.ShapeDtypeStruct((layer + 1, n, DN_HEADS, DN_DK, DN_DK), F32),
        jax.ShapeDtypeStruct((CONV_W - 1, n, 3 * DN_W), F32),
        jax.ShapeDtypeStruct((n, LRU_W), F32),
        jax.ShapeDtypeStruct((CONV_W - 1, n, LRU_W), F32),
    ]
    out_specs = [
        const((n, DN_W + LRU_W)),
        const((n, 4 * RET_W)),
        pl.BlockSpec((layer + 1, bb, DN_HEADS, DN_DK, DN_DK), lambda i: (0, i, 0, 0, 0)),
        const((CONV_W - 1, n, 3 * DN_W)),
        const((n, LRU_W)),
        const((CONV_W - 1, n, LRU_W)),
    ]
    scratch = [
        pltpu.VMEM((n, DN_W), F32), pltpu.VMEM((n, DN_W), F32), pltpu.VMEM((n, DN_W), F32),
        pltpu.VMEM((n, 128), F32), pltpu.VMEM((n, 128), F32),
        pltpu.VMEM((n, DN_W), F32), pltpu.VMEM((n, DN_W), F32),
    ]
    args = [x, lw["norm_mix"], lw["wqkv"], lw["wab"], lw["wrest"], lw["dn_conv_w"], lw["alog_r"],
            lw["dtb_r"], lw["dn_norm_w"], lw["lru_conv_w"], lw["lru_conv_b"], lw["wgate"], lw["bgate"],
            lw["lam"], cos, sin, sdn, dnc, hl, lrc]
    if layer > 0:
        args.append(prev_sdn)
    return pl.pallas_call(
        functools.partial(_mix_step_body, bb=bb, layer=layer),
        grid=(n // bb,),
        in_specs=in_specs,
        out_specs=out_specs,
        out_shape=out_shape,
        scratch_shapes=scratch,
        compiler_params=pltpu.CompilerParams(
            dimension_semantics=("arbitrary",), vmem_limit_bytes=VMEM_LIMIT_BYTES),
        name="mix_step",
    )(*args)


def _ret_step_body(*refs, layer):
    x_ref, omix_ref, retin_ref, wout_ref, s_ref = refs[:5]
    refs = refs[5:]
    if layer > 0:
        prev_ref = refs[0]
        refs = refs[1:]
    y_ref, s_o_ref, qt_s, kt_s, vt_s, ot_s = refs
    h = pl.program_id(0)
    if layer > 0:
        s_o_ref[0:layer] = prev_ref[...]

    @pl.when(h == 0)
    def _():
        qt_s[...] = retin_ref[:, 0:RET_W].T
        kt_s[...] = retin_ref[:, RET_W:2 * RET_W].T
        vt_s[...] = retin_ref[:, 2 * RET_W:3 * RET_W].T

    rows = pl.ds(pl.multiple_of(h * RET_DK, RET_DK), RET_DK)
    q = qt_s[rows, :]
    k = kt_s[rows, :]
    v = vt_s[rows, :]
    gamma = jnp.float32(math.exp(LOG_GAMMA[0]))
    for i in range(1, RET_HEADS):
        gamma = jnp.where(h == i, jnp.float32(math.exp(LOG_GAMMA[i])), gamma)
    acc = jnp.zeros_like(v)
    for d in range(RET_DK):
        s_d = s_ref[d]
        acc = acc + s_d * q[d:d + 1, :]
        s_o_ref[layer, 0, d] = gamma * s_d + k[d:d + 1, :] * v
    ot_s[rows, :] = gamma * acc + jnp.sum(q * k, axis=0, keepdims=True) * v

    @pl.when(h == pl.num_programs(0) - 1)
    def _():
        o_ret = _ret_norm_gate(ot_s[...].T, retin_ref[:, 3 * RET_W:], _head_avg_matrix())
        omix = jnp.concatenate([omix_ref[...], o_ret.astype(BF16)], axis=1)
        y_ref[...] = x_ref[...] + jnp.dot(omix, wout_ref[...], preferred_element_type=F32)


def _ret_step(x, omix_part, retin, w_out, sret_t, layer, prev, *, n):
    const = lambda shape: pl.BlockSpec(shape, lambda h: (0,) * len(shape))
    in_specs = [
        const((n, D_MODEL)), const((n, DN_W + LRU_W)), const((n, 4 * RET_W)), const((D_MODEL, D_MODEL)),
        pl.BlockSpec((None, None, RET_DK, RET_DK, n), lambda h: (layer, h, 0, 0, 0)),
    ]
    args = [x, omix_part, retin, w_out, sret_t]
    if layer > 0:
        in_specs.append(pl.BlockSpec((layer, 1, RET_DK, RET_DK, n), lambda h: (0, h, 0, 0, 0)))
        args.append(prev)
    return pl.pallas_call(
        functools.partial(_ret_step_body, layer=layer),
        grid=(RET_HEADS,),
        in_specs=in_specs,
        out_specs=[const((n, D_MODEL)),
                   pl.BlockSpec((layer + 1, 1, RET_DK, RET_DK, n), lambda h: (0, h, 0, 0, 0))],
        out_shape=[jax.ShapeDtypeStruct((n, D_MODEL), F32),
                   jax.ShapeDtypeStruct((layer + 1, RET_HEADS, RET_DK, RET_DK, n), F32)],
        scratch_shapes=[pltpu.VMEM((RET_W, n), F32)] * 4,
        compiler_params=pltpu.CompilerParams(
            dimension_semantics=("arbitrary",), vmem_limit_bytes=VMEM_LIMIT_BYTES),
        name="ret_step",
    )(*args)


def _rope_tables(pos):
    half = RET_DK // 2
    inv = ROPE_BASE ** (-jnp.arange(half, dtype=F32) / half)
    ang = pos.astype(F32)[:, None] * inv[None, :]
    cos = jnp.cos(ang)
    sin = jnp.sin(ang)
    cos_full = jnp.tile(jnp.concatenate([cos, cos], axis=-1), (1, RET_HEADS))
    sin_signed = jnp.tile(jnp.concatenate([-sin, sin], axis=-1), (1, RET_HEADS))
    return cos_full, sin_signed


def _block_diag(w):
    n, d, e = w.shape
    eye = jnp.eye(n, dtype=w.dtype)
    return (eye[:, None, :, None] * w[:, :, None, :]).reshape(n * d, n * e)


def _layer_weights(l, norm_mix, w_in, dn_conv_w, dn_a_log, dn_dt_bias, dn_norm_w, lru_conv_w, lru_conv_b,
                   lru_wa, lru_ba, lru_wx, lru_bx, lru_lambda, w_out):
    wl = w_in[l]
    o_a = 3 * DN_W
    wa = wl[:, o_a:o_a + DN_HEADS]
    wb = wl[:, o_a + DN_HEADS:o_a + 2 * DN_HEADS]
    pad_cols = lambda w: jnp.pad(w, ((0, 0), (0, 128 - DN_HEADS)))
    pad_lane = lambda v: jnp.pad(v, (0, 128 - DN_HEADS)).reshape(1, 128)
    pad_sub = lambda v: jnp.pad(v, (0, 8 - DN_HEADS)).reshape(8, 1)
    return {
        "norm_mix": norm_mix[l].reshape(1, D_MODEL),
        "wqkv": wl[:, :o_a].astype(BF16),
        "wab": jnp.concatenate([pad_cols(wa), pad_cols(wb)], axis=1).astype(BF16),
        "wabt": jnp.concatenate([wa, wb], axis=1).T.astype(BF16),
        "wrest": wl[:, o_a + 2 * DN_HEADS:].astype(BF16),
        "dn_conv_w": dn_conv_w[l],
        "alog_r": pad_lane(dn_a_log[l]), "dtb_r": pad_lane(dn_dt_bias[l]),
        "alog_c": pad_sub(dn_a_log[l]), "dtb_c": pad_sub(dn_dt_bias[l]),
        "dn_norm_w": dn_norm_w[l].reshape(1, DN_DK),
        "lru_conv_w": lru_conv_w[l], "lru_conv_b": lru_conv_b[l].reshape(1, LRU_W),
        "wgate": jnp.concatenate([_block_diag(lru_wa[l]), _block_diag(lru_wx[l])], axis=1).astype(BF16),
        "bgate": jnp.concatenate([lru_ba[l], lru_bx[l]]).reshape(1, 2 * LRU_W),
        "lam": lru_lambda[l].reshape(1, LRU_W),
        "w_out": w_out[l].astype(BF16),
    }


def kernel(x_prompt, x_sample, state_dn, state_dn_conv, state_lru, state_lru_conv, state_ret, meta_tokens, norm_ffn1, w_ffn1_in, w_ffn1_out, norm_mix, w_in, dn_conv_w, dn_a_log, dn_dt_bias, dn_norm_w, lru_conv_w, lru_conv_b, lru_wa, lru_ba, lru_wx, lru_bx, lru_lambda, w_out, norm_ffn2, w_ffn2_in, w_ffn2_out, norm_final):
    batch, seq, _ = x_prompt.shape
    n_dec = x_sample.shape[0]
    depth = w_in.shape[0]
    lws = [_layer_weights(l, norm_mix, w_in, dn_conv_w, dn_a_log, dn_dt_bias, dn_norm_w, lru_conv_w,
                          lru_conv_b, lru_wa, lru_ba, lru_wx, lru_bx, lru_lambda, w_out)
           for l in range(depth)]
    cos_m, sin_m = _rope_tables(jnp.arange(N_META))
    cos_p, sin_p = _rope_tables(N_META + jnp.arange(seq))
    cos_s, sin_s = _rope_tables(PAST_LEN + jnp.arange(1))

    n_small = n_dec + N_META
    xs = jnp.concatenate([x_sample[:, 0, :], meta_tokens.astype(F32)], axis=0)
    zero_init = (jnp.zeros((DN_HEADS, DN_DK, DN_DK), F32), jnp.zeros((8, 3 * DN_W), F32),
                 jnp.zeros((1, LRU_W), F32), jnp.zeros((8, LRU_W), F32), jnp.zeros((RET_W, RET_W), F32))
    new_s, meta_state, ffn_w = [], [], []
    sdn_all = sret_all = None
    sret_t = jnp.transpose(state_ret, (0, 2, 3, 4, 1))
    for l in range(depth):
        last = l == depth - 1
        xs, *f1 = _ffn_cast(xs, norm_ffn1[l], w_ffn1_in, w_ffn1_out, l, tf=FFN_DFF_SLICE)
        omix_part, retin, sdn_all, dnc, hl, lrc = _mix_step(
            xs[:n_dec], lws[l], cos_s, sin_s, l, state_dn, jnp.swapaxes(state_dn_conv[l], 0, 1),
            state_lru, jnp.swapaxes(state_lru_conv[l], 0, 1), sdn_all, bb=DECODE_BATCH_TILE)
        ys, sret_all = _ret_step(xs[:n_dec], omix_part, retin, lws[l]["w_out"], sret_t, l, sret_all,
                                 n=n_dec)
        ym, m_sdn, m_dnt, m_h, m_lrt, m_sretbd, _ = _mix_pipe(
            xs[n_dec:], lws[l], cos_m, sin_m, zero_init, batch=1, seq=N_META, tc=N_META, ck=N_META)
        new_s.append((jnp.swapaxes(dnc, 0, 1), hl, jnp.swapaxes(lrc, 0, 1)))
        meta_state.append((m_sdn[0], m_dnt[0], m_h[0], m_lrt[0], m_sretbd[0]))
        xs = jnp.concatenate([ys, ym], axis=0)
        xs, *f2 = _ffn_cast(xs, norm_ffn2[l], w_ffn2_in, w_ffn2_out, l, norm_final if last else None,
                            tf=FFN_DFF_SLICE)
        ffn_w.append((f1, f2))
    y_sample = xs[:n_dec].reshape(n_dec, 1, D_MODEL)

    xp = x_prompt.reshape(batch * seq, D_MODEL)
    new_p = []
    for l in range(depth):
        last = l == depth - 1
        xp = _ffn(xp, norm_ffn1[l], *ffn_w[l][0], tm=FFN_TOKEN_TILE, nf=D_FF // FFN_DFF_SLICE)
        xp, sdn, dnt, hl, lrt, _, sret = _mix_pipe(
            xp, lws[l], cos_p, sin_p, meta_state[l], batch=batch, seq=seq, tc=MIX_TIME_BLOCK, ck=CHUNK)
        new_p.append((sdn, dnt[:, 5:8], hl[:, 0], lrt[:, 5:8], sret))
        xp = _ffn(xp, norm_ffn2[l], *ffn_w[l][1], norm_final if last else None, tm=FFN_TOKEN_TILE,
                  nf=D_FF // FFN_DFF_SLICE)
    y_prompt = xp.reshape(batch, seq, D_MODEL)

    outs_p = [jnp.stack([s[j] for s in new_p]) for j in range(5)]
    dnc_s, lru_s, lrc_s = [jnp.stack([s[j] for s in new_s]) for j in range(3)]
    sret_s = jnp.transpose(sret_all, (0, 4, 1, 2, 3))
    return (y_prompt, y_sample, *outs_p, sdn_all, dnc_s, lru_s, lrc_s, sret_s)
```
